```python
import math
import jax
import jax.numpy as jnp
from jax import lax
import numpy as np

D_MODEL = 4096
BATCH = 4
SEQ = 2048
DEPTH = 2
DEC_BATCH = 128
DEC_SEQ = 4
PAST_LEN = 16384
PAGE_SIZE = 128

N_BRANCH = 4
MIX_WIDTH = D_MODEL // 4
HEAD_DIM = 128
ML_HEADS = MIX_WIDTH // HEAD_DIM
DN_HEADS = MIX_WIDTH // HEAD_DIM
LRU_WIDTH = MIX_WIDTH
LRU_BLOCKS = 16
LRU_BLOCK = LRU_WIDTH // LRU_BLOCKS
LRU_C = 8.0
RW_HEAD_DIM = 64
RW_HEADS = MIX_WIDTH // RW_HEAD_DIM
RW_DECAY_RANK = 128
RW_A_RANK = 128
RW_GATE_RANK = 480
CONV_WIDTH = 4
CHUNK = 64
D_FF = -(-(8 * D_MODEL) // (3 * 256)) * 256
NORM_EPS = 1e-6
RW_LN_EPS = 64e-5
F32 = jnp.float32

ML_COLS = 4 * MIX_WIDTH + 2 * ML_HEADS
DN_COLS = 4 * MIX_WIDTH + 2 * DN_HEADS
LRU_COLS = 2 * LRU_WIDTH
RW_COLS = 3 * MIX_WIDTH + RW_DECAY_RANK + RW_A_RANK + RW_GATE_RANK
GATE_COLS = N_BRANCH * D_MODEL
IN_COLS = ML_COLS + DN_COLS + LRU_COLS + RW_COLS + GATE_COLS
IN_SPLITS = (ML_COLS, ML_COLS + DN_COLS, ML_COLS + DN_COLS + LRU_COLS,
             ML_COLS + DN_COLS + LRU_COLS + RW_COLS)
RW_SPLITS = (MIX_WIDTH, 2 * MIX_WIDTH, 3 * MIX_WIDTH, 3 * MIX_WIDTH + RW_DECAY_RANK,
             3 * MIX_WIDTH + RW_DECAY_RANK + RW_A_RANK)

STATE_NAMES = ('state_mlstm_C', 'state_mlstm_n', 'state_mlstm_m', 'state_delta_S', 'state_delta_conv',
               'state_rglru_h', 'state_rglru_conv', 'state_rwkv_S', 'state_rwkv_shift')
STATE_SHAPES = (
    (ML_HEADS, HEAD_DIM, HEAD_DIM),
    (ML_HEADS, HEAD_DIM),
    (ML_HEADS,),
    (DN_HEADS, HEAD_DIM, HEAD_DIM),
    (CONV_WIDTH - 1, 3 * MIX_WIDTH),
    (LRU_WIDTH,),
    (CONV_WIDTH - 1, LRU_WIDTH),
    (RW_HEADS, RW_HEAD_DIM, RW_HEAD_DIM),
    (RW_COLS,),
)
STATE_SCALES = (0.1, 0.5, 1.0, 0.1, 1.0, 1.0, 1.0, 0.1, 1.0)

kernel_name = 'hybrid_mlstm_deltanet_rglru_rwkv7_step'


def _rmsnorm(x, g):
    xf = x.astype(F32)
    y = xf * lax.rsqrt(jnp.mean(xf * xf, axis=-1, keepdims=True) + NORM_EPS)
    return (y * g.astype(F32)).astype(x.dtype)


def _head_rmsnorm(x, g):
    return x * lax.rsqrt(jnp.mean(x * x, axis=-1, keepdims=True) + NORM_EPS) * g


def _l2norm(x):
    return x * lax.rsqrt(jnp.sum(x * x, axis=-1, keepdims=True) + NORM_EPS)


def _split_heads(t, d):
    return t.reshape(t.shape[:-1] + (-1, d))


def _chunk_len(T):
    return math.gcd(T, CHUNK)


def _to_chunks(a, L):
    B, T, H = a.shape[:3]
    a = a.reshape((B, T // L, L, H) + a.shape[3:])
    return jnp.swapaxes(jnp.moveaxis(a, 1, 0), 2, 3)


def _from_chunks(a):
    a = jnp.moveaxis(jnp.swapaxes(a, 2, 3), 0, 1)
    return a.reshape((a.shape[0], a.shape[1] * a.shape[2]) + a.shape[3:])


def _causal_conv(x, buf, w):
    T = x.shape[1]
    xp = jnp.concatenate([buf, x], axis=1)
    y = xp[:, 0:T] * w[0]
    for j in range(1, CONV_WIDTH):
        y = y + xp[:, j:j + T] * w[j]
    return y, xp[:, T:]


def _linear_combine(e1, e2):
    a1, b1 = e1
    a2, b2 = e2
    return a1 * a2, a2 * b1 + b2


def _mlstm_chunked(q, k, v, ig, lf, C0, n0, m0):
    L = _chunk_len(q.shape[1])
    causal = jnp.tril(jnp.ones((L, L), bool))

    def step(carry, xs):
        C, n, m = carry
        qc, kc, vc, ic, fc = xs
        b = jnp.cumsum(fc, axis=-1)
        logD = jnp.where(causal, b[..., :, None] - b[..., None, :] + ic[..., None, :], -jnp.inf)
        state_w = b + m[..., None]
        m_t = jnp.maximum(state_w, jnp.max(logD, axis=-1))
        scores = jnp.einsum('bhtd,bhsd->bhts', qc, kc) * jnp.exp(logD - m_t[..., None])
        sw = jnp.exp(state_w - m_t)
        num = jnp.einsum('bhts,bhsv->bhtv', scores, vc) + sw[..., None] * jnp.einsum('bhtd,bhdv->bhtv', qc, C)
        den = jnp.sum(scores, axis=-1) + sw * jnp.einsum('bhtd,bhd->bht', qc, n)
        h = num / jnp.maximum(jnp.abs(den), jnp.exp(-m_t))[..., None]
        m_L = m_t[..., -1]
        wk = jnp.exp(b[..., -1:] - b + ic - m_L[..., None])
        carry_decay = jnp.exp(b[..., -1] + m - m_L)
        C = carry_decay[..., None, None] * C + jnp.einsum('bhsd,bhsv->bhdv', kc * wk[..., None], vc)
        n = carry_decay[..., None] * n + jnp.einsum('bhs,bhsd->bhd', wk, kc)
        return (C, n, m_L), h

    xs = tuple(_to_chunks(a, L) for a in (q, k, v, ig, lf))
    (C, n, m), h = lax.scan(step, (C0, n0, m0), xs)
    return _from_chunks(h), C, n, m


def _gated_delta_chunked(q, k, v, g, beta, S0):
    L = _chunk_len(q.shape[1])
    strict = jnp.tril(jnp.ones((L, L), bool), -1)
    incl = jnp.tril(jnp.ones((L, L), bool))
    qc, kc, vc, gc, bc = (_to_chunks(a, L) for a in (q, k, v, g, beta))
    G = jnp.cumsum(gc, axis=-1)
    diff = G[..., :, None] - G[..., None, :]
    dec_strict = jnp.exp(jnp.where(strict, diff, -jnp.inf))
    dec_incl = jnp.exp(jnp.where(incl, diff, -jnp.inf))
    A = bc[..., None] * jnp.einsum('nbhtd,nbhsd->nbhts', kc, kc) * dec_strict
    rhs = jnp.concatenate([vc * bc[..., None], kc * (bc * jnp.exp(G))[..., None]], axis=-1)
    sol = lax.linalg.triangular_solve(A + jnp.eye(L, dtype=A.dtype), rhs, left_side=True,
                                      lower=True, unit_diagonal=True)
    u, w = sol[..., :HEAD_DIM], sol[..., HEAD_DIM:]
    qk = jnp.einsum('nbhtd,nbhsd->nbhts', qc, kc) * dec_incl

    def step(S, xs):
        u_c, w_c, qk_c, q_c, k_c, G_c = xs
        delta = u_c - jnp.einsum('bhld,bhdv->bhlv', w_c, S)
        o = jnp.einsum('bhld,bhdv->bhlv', q_c * jnp.exp(G_c)[..., None], S) + jnp.einsum('bhts,bhsv->bhtv', qk_c, delta)
        G_L = G_c[..., -1:]
        S = jnp.exp(G_L)[..., None] * S + jnp.einsum('bhsd,bhsv->bhdv', k_c * jnp.exp(G_L - G_c)[..., None], delta)
        return S, o

    S, o = lax.scan(step, S0, (u, w, qk, qc, kc, G))
    return _from_chunks(o), S


def _rglru(p_lru, conv_buf, h0, p):
    B, T, _ = p_lru.shape
    xb, gb = jnp.split(p_lru, 2, axis=-1)
    xc, conv_buf = _causal_conv(xb, conv_buf, p['lru_conv_w'])
    xc = xc + p['lru_conv_b']
    blk = xc.reshape(B, T, LRU_BLOCKS, LRU_BLOCK)
    r = jax.nn.sigmoid(jnp.einsum('btni,nij->btnj', blk, p['lru_wa']).reshape(B, T, LRU_WIDTH) + p['lru_ba'])
    i = jax.nn.sigmoid(jnp.einsum('btni,nij->btnj', blk, p['lru_wx']).reshape(B, T, LRU_WIDTH) + p['lru_bx'])
    log_a = -LRU_C * r * jax.nn.softplus(-p['lru_lambda'].astype(F32))
    a = jnp.exp(log_a)
    u = jnp.sqrt(-jnp.expm1(2.0 * log_a)) * (i * xc)
    a_cum, h_zero = lax.associative_scan(_linear_combine, (a, u), axis=1)
    h = a_cum * h0[:, None] + h_zero
    return h * jax.nn.gelu(gb), conv_buf, h[:, -1]


def _rwkv7(pr, shift, S0, p):
    B, T, _ = pr.shape
    prev = jnp.concatenate([shift[:, None], pr[:, :-1]], axis=1)
    z = pr + (prev - pr) * p['rw_mu']
    r, k, v, zw, za, zg = jnp.split(z, RW_SPLITS, axis=-1)
    w_log = -jax.nn.softplus(-(p['rw_w0'] + jnp.tanh(zw) @ p['rw_w2'])) - 0.5
    a = jax.nn.sigmoid(p['rw_a0'] + za @ p['rw_a2'])
    gate = jax.nn.sigmoid(zg) @ p['rw_g2']
    kk = _l2norm(_split_heads(k * p['rw_k_k'], RW_HEAD_DIM))
    k = k * (1.0 + (a - 1.0) * p['rw_k_a'])
    r_h, k_h, v_h, a_h = (_split_heads(t, RW_HEAD_DIM) for t in (r, k, v, a))
    decay = jnp.exp(-jnp.exp(_split_heads(w_log, RW_HEAD_DIM)))

    def step(S, xs):
        r_t, k_t, v_t, kk_t, a_t, d_t = xs
        Skk = jnp.einsum('bhij,bhj->bhi', S, kk_t)
        S = (S * d_t[..., None, :] - Skk[..., :, None] * (kk_t * a_t)[..., None, :]
             + v_t[..., :, None] * k_t[..., None, :])
        return S, jnp.einsum('bhij,bhj->bhi', S, r_t)

    xs = tuple(jnp.moveaxis(t, 1, 0) for t in (r_h, k_h, v_h, kk, a_h, decay))
    S, y = lax.scan(step, S0, xs)
    y = jnp.moveaxis(y, 0, 1)
    mean = jnp.mean(y, axis=-1, keepdims=True)
    var = jnp.mean(jnp.square(y - mean), axis=-1, keepdims=True)
    y = ((y - mean) * lax.rsqrt(var + RW_LN_EPS) * p['rw_ln_w'].reshape(RW_HEADS, RW_HEAD_DIM)
         + p['rw_ln_b'].reshape(RW_HEADS, RW_HEAD_DIM))
    y = y + jnp.sum(r_h * k_h * p['rw_r_k'], axis=-1, keepdims=True) * v_h
    return y.reshape(B, T, MIX_WIDTH) * gate, S, pr[:, -1]


def _layer(x, st, p):
    ml_C, ml_n, ml_m, dn_S, dn_conv, lru_h, lru_conv, rw_S, rw_shift = (s.astype(F32) for s in st)
    B, T, _ = x.shape
    W = MIX_WIDTH
    h = _rmsnorm(x, p['norm_mix'])
    proj = jnp.matmul(h, p['w_in']).astype(F32)
    p_ml, p_dn, p_lru, p_rw, p_gate = jnp.split(proj, IN_SPLITS, axis=-1)

    q, k, v, o_pre, i_pre, f_pre = jnp.split(p_ml, (W, 2 * W, 3 * W, 4 * W, 4 * W + ML_HEADS), axis=-1)
    h_ml, ml_C, ml_n, ml_m = _mlstm_chunked(
        _split_heads(q, HEAD_DIM) * HEAD_DIM ** -0.5, _split_heads(k, HEAD_DIM), _split_heads(v, HEAD_DIM),
        i_pre + p['ml_b_i'], jax.nn.log_sigmoid(f_pre + p['ml_b_f']), ml_C, ml_n, ml_m)
    h_ml = _head_rmsnorm(h_ml, p['ml_norm'].reshape(ML_HEADS, HEAD_DIM))
    out_ml = jax.nn.sigmoid(o_pre) * h_ml.reshape(B, T, W)

    qkv, z, a_pre, b_pre = jnp.split(p_dn, (3 * W, 4 * W, 4 * W + DN_HEADS), axis=-1)
    qkv, dn_conv = _causal_conv(qkv, dn_conv, p['dn_conv'])
    q, k, v = jnp.split(jax.nn.silu(qkv), 3, axis=-1)
    g = -jnp.exp(p['dn_A_log'].astype(F32)) * jax.nn.softplus(a_pre + p['dn_dt_bias'])
    o_dn, dn_S = _gated_delta_chunked(
        _l2norm(_split_heads(q, HEAD_DIM)) * HEAD_DIM ** -0.5, _l2norm(_split_heads(k, HEAD_DIM)),
        _split_heads(v, HEAD_DIM), g, jax.nn.sigmoid(b_pre), dn_S)
    out_dn = _head_rmsnorm(o_dn, p['dn_norm']).reshape(B, T, W) * jax.nn.silu(z)

    out_lru, lru_conv, lru_h = _rglru(p_lru, lru_conv, lru_h, p)

    out_rw, rw_S, rw_shift = _rwkv7(p_rw, rw_shift, rw_S, p)

    gates = jax.nn.sigmoid(p_gate).reshape(B, T, N_BRANCH, D_MODEL)
    branches = (out_ml, out_dn, out_lru, out_rw)
    mix = gates[:, :, 0] * (branches[0] @ p['w_branch'][0])
    for n in range(1, N_BRANCH):
        mix = mix + gates[:, :, n] * (branches[n] @ p['w_branch'][n])
    x = x + (mix @ p['w_out']).astype(x.dtype)

    h2 = _rmsnorm(x, p['norm_ffn'])
    ffn = (jax.nn.silu(h2 @ p['w_ffn_gate']) * (h2 @ p['w_ffn_up'])) @ p['w_ffn_down']
    x = x + ffn.astype(x.dtype)
    return x, (ml_C, ml_n, ml_m, dn_S, dn_conv, lru_h, lru_conv, rw_S, rw_shift)


def _trunk(x, states, params, norm_final):
    outs = [[] for _ in states]
    for l in range(DEPTH):
        p = {name: w[l] for name, w in params.items()}
        x, new = _layer(x, tuple(s[l] for s in states), p)
        for lst, s in zip(outs, new):
            lst.append(s)
    return _rmsnorm(x, norm_final), tuple(jnp.stack(lst) for lst in outs)


def _empty_state(batch):
    return tuple(jnp.zeros((DEPTH, batch) + shp, F32) for shp in STATE_SHAPES)


def setup_inputs(seed: int = 0) -> dict:
    key = jax.random.key(seed)
    ks = jax.random.split(key, 64)
    ctr = [0]

    def nxt():
        ctr[0] += 1
        return ks[ctr[0] - 1]

    def nrm(shape, scale):
        return jax.random.normal(nxt(), shape, F32) * scale

    def unif(shape, lo, hi):
        return jax.random.uniform(nxt(), shape, F32, lo, hi)

    def gain(shape):
        return 1.0 + nrm(shape, 0.01)

    inp = {}
    inp['x_prompt'] = nrm((BATCH, SEQ, D_MODEL), 1.0)
    inp['x_sample'] = nrm((DEC_BATCH, DEC_SEQ, D_MODEL), 1.0)
    for name, shp, sc in zip(STATE_NAMES, STATE_SHAPES, STATE_SCALES):
        inp[name] = nrm((DEPTH, DEC_BATCH) + shp, sc)

    inp['norm_mix'] = gain((DEPTH, D_MODEL))
    inp['w_in'] = nrm((DEPTH, D_MODEL, IN_COLS), D_MODEL ** -0.5)
    inp['ml_b_i'] = nrm((DEPTH, ML_HEADS), 0.1)
    inp['ml_b_f'] = jnp.linspace(3.0, 6.0, ML_HEADS, dtype=F32) + nrm((DEPTH, ML_HEADS), 0.1)
    inp['ml_norm'] = gain((DEPTH, MIX_WIDTH))
    inp['dn_conv'] = nrm((DEPTH, CONV_WIDTH, 3 * MIX_WIDTH), CONV_WIDTH ** -0.5)
    inp['dn_A_log'] = jnp.log(unif((DEPTH, DN_HEADS), 1.0, 16.0))
    dt = jnp.exp(unif((DEPTH, DN_HEADS), math.log(1e-3), math.log(1e-1)))
    inp['dn_dt_bias'] = dt + jnp.log(-jnp.expm1(-dt))
    inp['dn_norm'] = gain((DEPTH, HEAD_DIM))
    inp['lru_conv_w'] = nrm((DEPTH, CONV_WIDTH, LRU_WIDTH), CONV_WIDTH ** -0.5)
    inp['lru_conv_b'] = nrm((DEPTH, LRU_WIDTH), 0.01)
    inp['lru_wa'] = nrm((DEPTH, LRU_BLOCKS, LRU_BLOCK, LRU_BLOCK), LRU_BLOCK ** -0.5)
    inp['lru_ba'] = nrm((DEPTH, LRU_WIDTH), 0.01)
    inp['lru_wx'] = nrm((DEPTH, LRU_BLOCKS, LRU_BLOCK, LRU_BLOCK), LRU_BLOCK ** -0.5)
    inp['lru_bx'] = nrm((DEPTH, LRU_WIDTH), 0.01)
    a_root = unif((DEPTH, LRU_WIDTH), 0.9, 0.999) ** (1.0 / LRU_C)
    inp['lru_lambda'] = jnp.log(a_root) - jnp.log1p(-a_root)
    inp['rw_mu'] = unif((DEPTH, RW_COLS), 0.0, 1.0)
    inp['rw_w0'] = jnp.linspace(-6.0, -1.0, MIX_WIDTH, dtype=F32) + nrm((DEPTH, MIX_WIDTH), 0.1)
    inp['rw_w2'] = nrm((DEPTH, RW_DECAY_RANK, MIX_WIDTH), 0.1 * RW_DECAY_RANK ** -0.5)
    inp['rw_a0'] = nrm((DEPTH, MIX_WIDTH), 0.1)
    inp['rw_a2'] = nrm((DEPTH, RW_A_RANK, MIX_WIDTH), 0.1 * RW_A_RANK ** -0.5)
    inp['rw_g2'] = nrm((DEPTH, RW_GATE_RANK, MIX_WIDTH), RW_GATE_RANK ** -0.5)
    inp['rw_k_k'] = 0.85 + nrm((DEPTH, MIX_WIDTH), 0.01)
    inp['rw_k_a'] = gain((DEPTH, MIX_WIDTH))
    inp['rw_r_k'] = nrm((DEPTH, RW_HEADS, RW_HEAD_DIM), 0.1)
    inp['rw_ln_w'] = gain((DEPTH, MIX_WIDTH))
    inp['rw_ln_b'] = nrm((DEPTH, MIX_WIDTH), 0.01)
    inp['w_branch'] = nrm((DEPTH, N_BRANCH, MIX_WIDTH, D_MODEL), MIX_WIDTH ** -0.5)
    inp['w_out'] = nrm((DEPTH, D_MODEL, D_MODEL), D_MODEL ** -0.5)
    inp['norm_ffn'] = gain((DEPTH, D_MODEL))
    inp['w_ffn_gate'] = nrm((DEPTH, D_MODEL, D_FF), D_MODEL ** -0.5)
    inp['w_ffn_up'] = nrm((DEPTH, D_MODEL, D_FF), D_MODEL ** -0.5)
    inp['w_ffn_down'] = nrm((DEPTH, D_FF, D_MODEL), D_FF ** -0.5)
    inp['norm_final'] = gain((D_MODEL,))
    return inp


def reference(x_prompt, x_sample, state_mlstm_C, state_mlstm_n, state_mlstm_m, state_delta_S,
              state_delta_conv, state_rglru_h, state_rglru_conv, state_rwkv_S, state_rwkv_shift,
              norm_mix, w_in, ml_b_i, ml_b_f, ml_norm, dn_conv, dn_A_log, dn_dt_bias, dn_norm,
              lru_conv_w, lru_conv_b, lru_wa, lru_ba, lru_wx, lru_bx, lru_lambda,
              rw_mu, rw_w0, rw_w2, rw_a0, rw_a2, rw_g2, rw_k_k, rw_k_a, rw_r_k, rw_ln_w, rw_ln_b,
              w_branch, w_out, norm_ffn, w_ffn_gate, w_ffn_up, w_ffn_down, norm_final):
    params = {
        'norm_mix': norm_mix, 'w_in': w_in,
        'ml_b_i': ml_b_i, 'ml_b_f': ml_b_f, 'ml_norm': ml_norm,
        'dn_conv': dn_conv, 'dn_A_log': dn_A_log, 'dn_dt_bias': dn_dt_bias, 'dn_norm': dn_norm,
        'lru_conv_w': lru_conv_w, 'lru_conv_b': lru_conv_b, 'lru_wa': lru_wa, 'lru_ba': lru_ba,
        'lru_wx': lru_wx, 'lru_bx': lru_bx, 'lru_lambda': lru_lambda,
        'rw_mu': rw_mu, 'rw_w0': rw_w0, 'rw_w2': rw_w2, 'rw_a0': rw_a0, 'rw_a2': rw_a2, 'rw_g2': rw_g2,
        'rw_k_k': rw_k_k, 'rw_k_a': rw_k_a, 'rw_r_k': rw_r_k, 'rw_ln_w': rw_ln_w, 'rw_ln_b': rw_ln_b,
        'w_branch': w_branch, 'w_out': w_out,
        'norm_ffn': norm_ffn, 'w_ffn_gate': w_ffn_gate, 'w_ffn_up': w_ffn_up, 'w_ffn_down': w_ffn_down,
    }
    y_prompt, (p_mlC, p_mln, p_mlm, p_dnS, p_dnconv, p_lruh, p_lruconv, p_rwS, p_rwshift) = _trunk(
        x_prompt, _empty_state(x_prompt.shape[0]), params, norm_final)
    sample_states = (state_mlstm_C, state_mlstm_n, state_mlstm_m, state_delta_S, state_delta_conv,
                     state_rglru_h, state_rglru_conv, state_rwkv_S, state_rwkv_shift)
    y_sample, (s_mlC, s_mln, s_mlm, s_dnS, s_dnconv, s_lruh, s_lruconv, s_rwS, s_rwshift) = _trunk(
        x_sample, sample_states, params, norm_final)
    return (y_prompt, y_sample,
            p_mlC, p_mln, p_mlm, p_dnS, p_dnconv, p_lruh, p_lruconv, p_rwS, p_rwshift,
            s_mlC, s_mln, s_mlm, s_dnS, s_dnconv, s_lruh, s_lruconv, s_rwS, s_rwshift)
```

```python
import functools
import math

import jax
import jax.numpy as jnp
from jax import lax
from jax.experimental import pallas as pl
from jax.experimental.pallas import tpu as pltpu

F32 = jnp.float32
BF16 = jnp.bfloat16

D_MODEL = 4096
MIX = D_MODEL // 4
HEAD_DIM = 128
N_HEADS = MIX // HEAD_DIM
RW_HD = 64
RW_HEADS = MIX // RW_HD
RW_DECAY_RANK = 128
RW_A_RANK = 128
RW_GATE_RANK = 480
CONV_W = 4
LRU_C = 8.0
D_FF = -(-(8 * D_MODEL) // (3 * 256)) * 256
NORM_EPS = 1e-6
RW_LN_EPS = 64e-5

_ML_COLS = 4 * MIX + 2 * N_HEADS
_DN_COLS = 4 * MIX + 2 * N_HEADS
_LRU_COLS = 2 * MIX
_RW_COLS = 3 * MIX + RW_DECAY_RANK + RW_A_RANK + RW_GATE_RANK
_O_ML = 0
_O_DN = _ML_COLS
_O_LRU = _O_DN + _DN_COLS
_O_RW = _O_LRU + _LRU_COLS
_O_GATE = _O_RW + _RW_COLS

C_GATE = 0
C_ML = 4 * D_MODEL
C_DN = C_ML + 4 * MIX
C_LRU = C_DN + 4 * MIX
C_RW = C_LRU + 2 * MIX
C_RWX = C_RW + 3 * MIX
RWX_W = 0
RWX_A = RW_DECAY_RANK
RWX_G = RW_DECAY_RANK + RW_A_RANK
RWX_SMALL = 768
N_PROJ = C_RWX + 1024
D_FF_PAD = -(-D_FF // 1024) * 1024

VMEM_LIMIT_BYTES = 56 * 1024 * 1024
LANE = 128


def _cparams(sem):
    return pltpu.CompilerParams(dimension_semantics=sem, vmem_limit_bytes=VMEM_LIMIT_BYTES)


def _rmsnorm_body(x_ref, g_ref, o_ref):
    x = x_ref[...]
    y = x * lax.rsqrt(jnp.mean(x * x, axis=-1, keepdims=True) + NORM_EPS)
    o_ref[...] = (y * g_ref[...]).astype(o_ref.dtype)


def _rmsnorm(x, g, out_dtype, tm):
    m, d = x.shape
    return pl.pallas_call(
        _rmsnorm_body,
        grid=(m // tm,),
        in_specs=[pl.BlockSpec((tm, d), lambda i: (i, 0)), pl.BlockSpec((1, d), lambda i: (0, 0))],
        out_specs=pl.BlockSpec((tm, d), lambda i: (i, 0)),
        out_shape=jax.ShapeDtypeStruct((m, d), out_dtype),
        compiler_params=_cparams(("parallel",)),
        name="rmsnorm",
    )(x, g.reshape(1, d))


def _mm_body(x_ref, w_ref, o_ref):
    o_ref[...] = jnp.dot(x_ref[...], w_ref[...], preferred_element_type=F32).astype(o_ref.dtype)


def _matmul(x, w, tm, tn, out_dtype=F32):
    m, k = x.shape
    n = w.shape[1]
    return pl.pallas_call(
        _mm_body,
        grid=(n // tn, m // tm),
        in_specs=[pl.BlockSpec((tm, k), lambda j, i: (i, 0)), pl.BlockSpec((k, tn), lambda j, i: (0, j))],
        out_specs=pl.BlockSpec((tm, tn), lambda j, i: (i, j)),
        out_shape=jax.ShapeDtypeStruct((m, n), out_dtype),
        compiler_params=_cparams(("parallel", "parallel")),
        name="matmul",
    )(x, w)


def _mm_res_body(x_ref, w_ref, r_ref, o_ref, acc_ref):
    kk = pl.program_id(2)

    @pl.when(kk == 0)
    def _():
        acc_ref[...] = r_ref[...]

    acc_ref[...] += jnp.dot(x_ref[...], w_ref[...], preferred_element_type=F32)

    @pl.when(kk == pl.num_programs(2) - 1)
    def _():
        o_ref[...] = acc_ref[...]


def _matmul_residual(x, w, res, tm, tn, tk):
    m, k = x.shape
    n = w.shape[1]
    return pl.pallas_call(
        _mm_res_body,
        grid=(n // tn, m // tm, k // tk),
        in_specs=[pl.BlockSpec((tm, tk), lambda j, i, kk: (i, kk)),
                  pl.BlockSpec((tk, tn), lambda j, i, kk: (kk, j)),
                  pl.BlockSpec((tm, tn), lambda j, i, kk: (i, j))],
        out_specs=pl.BlockSpec((tm, tn), lambda j, i, kk: (i, j)),
        out_shape=jax.ShapeDtypeStruct((m, n), F32),
        scratch_shapes=[pltpu.VMEM((tm, tn), F32)],
        compiler_params=_cparams(("parallel", "parallel", "arbitrary")),
        name="matmul_residual",
    )(x, w, res)


def _ffn_up_body(x_ref, wg_ref, wu_ref, o_ref):
    x = x_ref[...]
    g = jnp.dot(x, wg_ref[...], preferred_element_type=F32)
    u = jnp.dot(x, wu_ref[...], preferred_element_type=F32)
    o_ref[...] = (jax.nn.silu(g) * u).astype(o_ref.dtype)


def _ffn_up(x, wg, wu, tm, tn):
    m, k = x.shape
    n = wg.shape[1]
    return pl.pallas_call(
        _ffn_up_body,
        grid=(n // tn, m // tm),
        in_specs=[pl.BlockSpec((tm, k), lambda j, i: (i, 0)),
                  pl.BlockSpec((k, tn), lambda j, i: (0, j)),
                  pl.BlockSpec((k, tn), lambda j, i: (0, j))],
        out_specs=pl.BlockSpec((tm, tn), lambda j, i: (i, j)),
        out_shape=jax.ShapeDtypeStruct((m, n), BF16),
        compiler_params=_cparams(("parallel", "parallel")),
        name="ffn_up",
    )(x, wg, wu)


def _merge_body(g_ref, b_ref, w_ref, o_ref, acc_ref):
    n = pl.program_id(2)
    term = jax.nn.sigmoid(g_ref[...]) * jnp.dot(b_ref[0], w_ref[0], preferred_element_type=F32)

    @pl.when(n == 0)
    def _():
        acc_ref[...] = term

    @pl.when(n > 0)
    def _():
        acc_ref[...] += term

    @pl.when(n == pl.num_programs(2) - 1)
    def _():
        o_ref[...] = acc_ref[...].astype(o_ref.dtype)


def _merge(proj, branches, wb, tm, tn):
    nb, m, k = branches.shape
    n = wb.shape[2]
    gblk = n // tn
    return pl.pallas_call(
        _merge_body,
        grid=(n // tn, m // tm, nb),
        in_specs=[pl.BlockSpec((tm, tn), lambda j, i, b: (i, b * gblk + j)),
                  pl.BlockSpec((1, tm, k), lambda j, i, b: (b, i, 0)),
                  pl.BlockSpec((1, k, tn), lambda j, i, b: (b, 0, j))],
        out_specs=pl.BlockSpec((tm, tn), lambda j, i, b: (i, j)),
        out_shape=jax.ShapeDtypeStruct((m, n), BF16),
        scratch_shapes=[pltpu.VMEM((tm, tn), F32)],
        compiler_params=_cparams(("parallel", "parallel", "arbitrary")),
        name="merge",
    )(proj, branches, wb)


def _tri_masks(L):
    row = lax.broadcasted_iota(jnp.int32, (L, L), 0)
    col = lax.broadcasted_iota(jnp.int32, (L, L), 1)
    return col <= row, col < row


def _dot(a, b):
    return jnp.dot(a.astype(BF16), b.astype(BF16), preferred_element_type=F32)


def _dot_nt(a, b):
    return lax.dot_general(a.astype(BF16), b.astype(BF16), (((1,), (1,)), ((), ())), preferred_element_type=F32)


def _dot_tn(a, b):
    return lax.dot_general(a.astype(BF16), b.astype(BF16), (((0,), (0,)), ((), ())), preferred_element_type=F32)


def _dot_exact(a, b):
    return jnp.dot(a, b, precision=lax.Precision.HIGHEST, preferred_element_type=F32)


def _unit_lower_inverse(n_strict, L):
    eye = (lax.broadcasted_iota(jnp.int32, (L, L), 0) == lax.broadcasted_iota(jnp.int32, (L, L), 1)).astype(F32)
    m = -n_strict
    t = eye + m
    span = 2
    while span < L:
        m = _dot(m, m)
        t = t + _dot(t, m)
        span *= 2
    return t


def _mlstm_body(q_ref, k_ref, v_ref, o_ref, gc_ref, gr_ref, bc_ref, br_ref, norm_ref, c0_ref, n0_ref, m0_ref,
                out_ref, cout_ref, nout_ref, mout_ref, c_s, n_s, m_s, *, L):
    c = pl.program_id(1)

    @pl.when(c == 0)
    def _():
        c_s[...] = c0_ref[0]
        n_s[...] = n0_ref[0]
        m_s[...] = m0_ref[0]

    incl, _ = _tri_masks(L)
    tril = incl.astype(F32)
    gc = gc_ref[0]
    gr = gr_ref[0]
    ig_c = gc[:, 0:N_HEADS] + bc_ref[:, 0:N_HEADS]
    lf_c = jax.nn.log_sigmoid(gc[:, N_HEADS:2 * N_HEADS] + bc_ref[:, N_HEADS:2 * N_HEADS])
    ig_r = gr[0:N_HEADS, :] + br_ref[0:N_HEADS, :]
    lf_r = jax.nn.log_sigmoid(gr[N_HEADS:2 * N_HEADS, :] + br_ref[N_HEADS:2 * N_HEADS, :])
    b_c = _dot_exact(tril, lf_c)
    b_r = _dot_exact(lf_r, tril.T)
    m_all = m_s[...]
    m_new = []
    for h in range(N_HEADS):
        sl = slice(h * HEAD_DIM, (h + 1) * HEAD_DIM)
        qh = q_ref[0, :, sl] * (HEAD_DIM ** -0.5)
        kh = k_ref[0, :, sl]
        vh = v_ref[0, :, sl]
        bc = b_c[:, h:h + 1]
        m_prev = m_all[:, h:h + 1]
        log_d = jnp.where(incl, bc - b_r[h:h + 1, :] + ig_r[h:h + 1, :], -jnp.inf)
        state_w = bc + m_prev
        m_t = jnp.maximum(state_w, jnp.max(log_d, axis=-1, keepdims=True))
        scores = _dot_nt(qh, kh) * jnp.exp(log_d - m_t)
        sw = jnp.exp(state_w - m_t)
        c_h = c_s[h]
        n_h = n_s[h:h + 1, :]
        num = _dot(scores, vh) + sw * _dot(qh, c_h)
        den = jnp.sum(scores, axis=-1, keepdims=True) + sw * jnp.sum(qh * n_h, axis=-1, keepdims=True)
        hid = num / jnp.maximum(jnp.abs(den), jnp.exp(-m_t))
        m_last = m_t[L - 1:L, :]
        b_last = bc[L - 1:L, :]
        wk = jnp.exp(b_last - bc + ig_c[:, h:h + 1] - m_last)
        decay = jnp.exp(b_last + m_prev - m_last)
        kw = kh * wk
        c_s[h] = decay * c_h + _dot_tn(kw, vh)
        n_s[h:h + 1, :] = decay * n_h + jnp.sum(kw, axis=0, keepdims=True)
        m_new.append(m_last)
        hn = hid * lax.rsqrt(jnp.mean(hid * hid, axis=-1, keepdims=True) + NORM_EPS) * norm_ref[:, sl]
        out_ref[0, :, sl] = (jax.nn.sigmoid(o_ref[0, :, sl]) * hn).astype(out_ref.dtype)
    m_s[...] = jnp.concatenate(m_new, axis=1)

    @pl.when(c == pl.num_programs(1) - 1)
    def _():
        cout_ref[0] = c_s[...]
        nout_ref[0] = n_s[...]
        mout_ref[0] = m_s[...]


def _mlstm(proj3, small_c, small_r, blk0, n_seq, n_chunks, L, b_i, b_f, ml_norm, c0, n0, m0):
    cb = C_ML // MIX
    bias_c = jnp.zeros((1, LANE), F32).at[0, 0:N_HEADS].set(b_i).at[0, N_HEADS:2 * N_HEADS].set(b_f)
    bias_r = jnp.zeros((32, 1), F32).at[0:N_HEADS, 0].set(b_i).at[N_HEADS:2 * N_HEADS, 0].set(b_f)

    def col(j):
        return pl.BlockSpec((1, L, MIX), lambda b, c: (blk0 + b * n_chunks + c, 0, cb + j))

    full2 = lambda shape: pl.BlockSpec(shape, lambda b, c: (0, 0))
    outs = pl.pallas_call(
        functools.partial(_mlstm_body, L=L),
        grid=(n_seq, n_chunks),
        in_specs=[col(0), col(1), col(2), col(3),
                  pl.BlockSpec((1, L, LANE), lambda b, c: (blk0 + b * n_chunks + c, 0, 0)),
                  pl.BlockSpec((1, 32, L), lambda b, c: (blk0 + b * n_chunks + c, 0, 0)),
                  full2((1, LANE)), full2((32, 1)), full2((1, MIX)),
                  pl.BlockSpec((1, N_HEADS, HEAD_DIM, HEAD_DIM), lambda b, c: (b, 0, 0, 0)),
                  pl.BlockSpec((1, N_HEADS, HEAD_DIM), lambda b, c: (b, 0, 0)),
                  pl.BlockSpec((1, 1, N_HEADS), lambda b, c: (b, 0, 0))],
        out_specs=[pl.BlockSpec((1, L, MIX), lambda b, c: (b * n_chunks + c, 0, 0)),
                   pl.BlockSpec((1, N_HEADS, HEAD_DIM, HEAD_DIM), lambda b, c: (b, 0, 0, 0)),
                   pl.BlockSpec((1, N_HEADS, HEAD_DIM), lambda b, c: (b, 0, 0)),
                   pl.BlockSpec((1, 1, N_HEADS), lambda b, c: (b, 0, 0))],
        out_shape=[jax.ShapeDtypeStruct((n_seq * n_chunks, L, MIX), BF16),
                   jax.ShapeDtypeStruct((n_seq, N_HEADS, HEAD_DIM, HEAD_DIM), F32),
                   jax.ShapeDtypeStruct((n_seq, N_HEADS, HEAD_DIM), F32),
                   jax.ShapeDtypeStruct((n_seq, 1, N_HEADS), F32)],
        scratch_shapes=[pltpu.VMEM((N_HEADS, HEAD_DIM, HEAD_DIM), F32),
                        pltpu.VMEM((N_HEADS, HEAD_DIM), F32),
                        pltpu.VMEM((1, N_HEADS), F32)],
        compiler_params=_cparams(("parallel", "arbitrary")),
        name="mlstm",
    )(proj3, proj3, proj3, proj3, small_c, small_r, bias_c, bias_r, ml_norm.reshape(1, MIX),
      c0, n0, m0.reshape(n_seq, 1, N_HEADS))
    out, c_new, n_new, m_new = outs
    return out, c_new, n_new, m_new.reshape(n_seq, N_HEADS)


HIST = 8


def _delta_body(q_ref, k_ref, v_ref, z_ref, gc_ref, gr_ref, bc_ref, br_ref, cw_ref, hist_ref, norm_ref, s0_ref,
                out_ref, sout_ref, s_s, ext_s, *, L):
    c = pl.program_id(1)

    @pl.when(c == 0)
    def _():
        s_s[...] = s0_ref[0]
        ext_s[0:HIST, :] = hist_ref[0]

    ext_s[HIST:HIST + L, 0:MIX] = q_ref[0]
    ext_s[HIST:HIST + L, MIX:2 * MIX] = k_ref[0]
    ext_s[HIST:HIST + L, 2 * MIX:3 * MIX] = v_ref[0]
    base = HIST - (CONV_W - 1)
    y = ext_s[base:base + L, :] * cw_ref[0:1, :]
    for j in range(1, CONV_W):
        y = y + ext_s[base + j:base + j + L, :] * cw_ref[j:j + 1, :]
    ext_s[0:HIST, :] = ext_s[L:L + HIST, :]
    qkv = jax.nn.silu(y)

    incl, strict = _tri_masks(L)
    tril = incl.astype(F32)
    gc = gc_ref[0]
    gr = gr_ref[0]
    a_log_c, dt_c = bc_ref[:, 0:N_HEADS], bc_ref[:, N_HEADS:2 * N_HEADS]
    a_log_r, dt_r = br_ref[0:N_HEADS, :], br_ref[N_HEADS:2 * N_HEADS, :]
    g_c = -jnp.exp(a_log_c) * jax.nn.softplus(gc[:, 2 * N_HEADS:3 * N_HEADS] + dt_c)
    g_r = -jnp.exp(a_log_r) * jax.nn.softplus(gr[2 * N_HEADS:3 * N_HEADS, :] + dt_r)
    beta_c = jax.nn.sigmoid(gc[:, 3 * N_HEADS:4 * N_HEADS])
    gcum_c = _dot_exact(tril, g_c)
    gcum_r = _dot_exact(g_r, tril.T)
    for h in range(N_HEADS):
        sl = slice(h * HEAD_DIM, (h + 1) * HEAD_DIM)
        qh = qkv[:, h * HEAD_DIM:(h + 1) * HEAD_DIM]
        kh = qkv[:, MIX + h * HEAD_DIM:MIX + (h + 1) * HEAD_DIM]
        vh = qkv[:, 2 * MIX + h * HEAD_DIM:2 * MIX + (h + 1) * HEAD_DIM]
        qh = qh * lax.rsqrt(jnp.sum(qh * qh, axis=-1, keepdims=True) + NORM_EPS) * (HEAD_DIM ** -0.5)
        kh = kh * lax.rsqrt(jnp.sum(kh * kh, axis=-1, keepdims=True) + NORM_EPS)
        gch = gcum_c[:, h:h + 1]
        beta = beta_c[:, h:h + 1]
        dec_incl = jnp.exp(jnp.where(incl, gch - gcum_r[h:h + 1, :], -jnp.inf))
        dec_strict = jnp.where(strict, dec_incl, 0.0)
        a_mat = beta * _dot_nt(kh, kh) * dec_strict
        t_inv = _unit_lower_inverse(a_mat, L)
        eg = jnp.exp(gch)
        u = _dot(t_inv, vh * beta)
        w = _dot(t_inv, kh * (beta * eg))
        qk = _dot_nt(qh, kh) * dec_incl
        s_h = s_s[h]
        delta = u - _dot(w, s_h)
        o = _dot(qh * eg, s_h) + _dot(qk, delta)
        g_last = gch[L - 1:L, :]
        s_s[h] = jnp.exp(g_last) * s_h + _dot_tn(kh * jnp.exp(g_last - gch), delta)
        on = o * lax.rsqrt(jnp.mean(o * o, axis=-1, keepdims=True) + NORM_EPS) * norm_ref[...]
        out_ref[0, :, sl] = (on * jax.nn.silu(z_ref[0, :, sl])).astype(out_ref.dtype)

    @pl.when(c == pl.num_programs(1) - 1)
    def _():
        sout_ref[0] = s_s[...]


def _hist_rows(state_rows, n_seq, width):
    k = state_rows.shape[1]
    return jnp.concatenate([jnp.zeros((n_seq, HIST - k, width), F32), state_rows], axis=1)


def _delta(proj3, small_c, small_r, blk0, n_seq, n_chunks, L, conv_w, a_log, dt_bias, dn_norm, conv0, s0):
    cb = C_DN // MIX
    bias_c = jnp.zeros((1, LANE), F32).at[0, 0:N_HEADS].set(a_log).at[0, N_HEADS:2 * N_HEADS].set(dt_bias)
    bias_r = jnp.zeros((32, 1), F32).at[0:N_HEADS, 0].set(a_log).at[N_HEADS:2 * N_HEADS, 0].set(dt_bias)
    hist = _hist_rows(conv0, n_seq, 3 * MIX)

    def col(j):
        return pl.BlockSpec((1, L, MIX), lambda b, c: (blk0 + b * n_chunks + c, 0, cb + j))

    full2 = lambda shape: pl.BlockSpec(shape, lambda b, c: (0, 0))
    out, s_new = pl.pallas_call(
        functools.partial(_delta_body, L=L),
        grid=(n_seq, n_chunks),
        in_specs=[col(0), col(1), col(2), col(3),
                  pl.BlockSpec((1, L, LANE), lambda b, c: (blk0 + b * n_chunks + c, 0, 0)),
                  pl.BlockSpec((1, 32, L), lambda b, c: (blk0 + b * n_chunks + c, 0, 0)),
                  full2((1, LANE)), full2((32, 1)), full2((CONV_W, 3 * MIX)),
                  pl.BlockSpec((1, HIST, 3 * MIX), lambda b, c: (b, 0, 0)),
                  full2((1, HEAD_DIM)),
                  pl.BlockSpec((1, N_HEADS, HEAD_DIM, HEAD_DIM), lambda b, c: (b, 0, 0, 0))],
        out_specs=[pl.BlockSpec((1, L, MIX), lambda b, c: (b * n_chunks + c, 0, 0)),
                   pl.BlockSpec((1, N_HEADS, HEAD_DIM, HEAD_DIM), lambda b, c: (b, 0, 0, 0))],
        out_shape=[jax.ShapeDtypeStruct((n_seq * n_chunks, L, MIX), BF16),
                   jax.ShapeDtypeStruct((n_seq, N_HEADS, HEAD_DIM, HEAD_DIM), F32)],
        scratch_shapes=[pltpu.VMEM((N_HEADS, HEAD_DIM, HEAD_DIM), F32),
                        pltpu.VMEM((HIST + max(L, HIST), 3 * MIX), F32)],
        compiler_params=_cparams(("parallel", "arbitrary")),
        name="deltanet",
    )(proj3, proj3, proj3, proj3, small_c, small_r, bias_c, bias_r, conv_w, hist, dn_norm.reshape(1, HEAD_DIM), s0)
    return out, s_new


RWX_BLOCK = 1024
RW_GATE_PAD = 512


def _rwkv_body(r_ref, k_ref, v_ref, x_ref, hist_ref, mu_ref, w0_ref, w2_ref, a0_ref, a2_ref, g2_ref,
               kk_ref, ka_ref, rk_ref, lnw_ref, lnb_ref, s0_ref,
               out_ref, sout_ref, s_s, ext_s, *, L):
    c = pl.program_id(1)
    width = 3 * MIX + RWX_BLOCK

    @pl.when(c == 0)
    def _():
        s_s[...] = s0_ref[0]
        ext_s[0:HIST, :] = hist_ref[0]

    ext_s[HIST:HIST + L, 0:MIX] = r_ref[0]
    ext_s[HIST:HIST + L, MIX:2 * MIX] = k_ref[0]
    ext_s[HIST:HIST + L, 2 * MIX:3 * MIX] = v_ref[0]
    ext_s[HIST:HIST + L, 3 * MIX:width] = x_ref[0]
    cur = ext_s[HIST:HIST + L, :]
    prev = ext_s[HIST - 1:HIST - 1 + L, :]
    ext_s[0:HIST, :] = ext_s[L:L + HIST, :]
    z = cur + (prev - cur) * mu_ref[...]
    r = z[:, 0:MIX]
    k = z[:, MIX:2 * MIX]
    v = z[:, 2 * MIX:3 * MIX]
    zw = z[:, 3 * MIX + RWX_W:3 * MIX + RWX_W + RW_DECAY_RANK]
    za = z[:, 3 * MIX + RWX_A:3 * MIX + RWX_A + RW_A_RANK]
    zg = z[:, 3 * MIX + RWX_G:3 * MIX + RWX_G + RW_GATE_PAD]
    w_log = -jax.nn.softplus(-(w0_ref[...] + _dot(jnp.tanh(zw), w2_ref[...]))) - 0.5
    a = jax.nn.sigmoid(a0_ref[...] + _dot(za, a2_ref[...]))
    gate = _dot(jax.nn.sigmoid(zg), g2_ref[...])
    kkk = k * kk_ref[...]
    k2 = k * (1.0 + (a - 1.0) * ka_ref[...])
    log_decay = -jnp.exp(w_log)

    incl, strict = _tri_masks(L)
    cum = _dot_exact(incl.astype(F32), log_decay)
    e_cum = jnp.exp(cum)
    e_prev = jnp.exp(cum - log_decay)
    e_neg = jnp.exp(-cum)
    e_last = e_cum[L - 1:L, :]
    for h in range(RW_HEADS):
        sl = slice(h * RW_HD, (h + 1) * RW_HD)
        kkh = kkk[:, sl]
        kkh = kkh * lax.rsqrt(jnp.sum(kkh * kkh, axis=-1, keepdims=True) + NORM_EPS)
        rh, kh, vh = r[:, sl], k2[:, sl], v[:, sl]
        k_til = kkh * e_prev[:, sl]
        r_til = rh * e_cum[:, sl]
        k_hat = kh * e_neg[:, sl]
        b_hat = kkh * a[:, sl] * e_neg[:, sl]
        a_k = jnp.where(strict, _dot_nt(k_til, k_hat), 0.0)
        a_b = jnp.where(strict, _dot_nt(k_til, b_hat), 0.0)
        b_k = jnp.where(incl, _dot_nt(r_til, k_hat), 0.0)
        b_b = jnp.where(incl, _dot_nt(r_til, b_hat), 0.0)
        t_inv = _unit_lower_inverse(a_b, L)
        s_h = s_s[h]
        u = _dot(t_inv, _dot_nt(k_til, s_h) + _dot(a_k, vh))
        y = _dot_nt(r_til, s_h) + _dot(b_k, vh) - _dot(b_b, u)
        el = e_last[:, sl]
        s_s[h] = s_h * el + _dot_tn(vh, k_hat * el) - _dot_tn(u, b_hat * el)
        mean = jnp.mean(y, axis=-1, keepdims=True)
        var = jnp.mean(jnp.square(y - mean), axis=-1, keepdims=True)
        y = (y - mean) * lax.rsqrt(var + RW_LN_EPS) * lnw_ref[:, sl] + lnb_ref[:, sl]
        y = y + jnp.sum(rh * kh * rk_ref[:, sl], axis=-1, keepdims=True) * vh
        out_ref[0, :, sl] = (y * gate[:, sl]).astype(out_ref.dtype)

    @pl.when(c == pl.num_programs(1) - 1)
    def _():
        sout_ref[0] = s_s[...]


def _rw_cols(vec):
    pad = jnp.zeros((RWX_BLOCK - (_RW_COLS - 3 * MIX),), F32)
    return jnp.concatenate([vec, pad]).reshape(1, 3 * MIX + RWX_BLOCK)


def _rwkv(proj3, blk0, n_seq, n_chunks, L, p, shift0, s0):
    cb = C_RW // MIX
    width = 3 * MIX + RWX_BLOCK
    hist = jnp.concatenate([jnp.zeros((n_seq, HIST - 1, width), F32),
                            jnp.pad(shift0, ((0, 0), (0, width - _RW_COLS)))[:, None, :]], axis=1)
    g2 = jnp.pad(p['rw_g2'], ((0, RW_GATE_PAD - RW_GATE_RANK), (0, 0))).astype(BF16)
    row = lambda vec: vec.reshape(1, MIX)

    def col(j):
        return pl.BlockSpec((1, L, MIX), lambda b, c: (blk0 + b * n_chunks + c, 0, cb + j))

    full2 = lambda shape: pl.BlockSpec(shape, lambda b, c: (0, 0))
    out, s_new = pl.pallas_call(
        functools.partial(_rwkv_body, L=L),
        grid=(n_seq, n_chunks),
        in_specs=[col(0), col(1), col(2), col(3),
                  pl.BlockSpec((1, HIST, width), lambda b, c: (b, 0, 0)),
                  full2((1, width)),
                  full2((1, MIX)), full2((RW_DECAY_RANK, MIX)),
                  full2((1, MIX)), full2((RW_A_RANK, MIX)),
                  full2((RW_GATE_PAD, MIX)),
                  full2((1, MIX)), full2((1, MIX)), full2((1, MIX)), full2((1, MIX)), full2((1, MIX)),
                  pl.BlockSpec((1, RW_HEADS, RW_HD, RW_HD), lambda b, c: (b, 0, 0, 0))],
        out_specs=[pl.BlockSpec((1, L, MIX), lambda b, c: (b * n_chunks + c, 0, 0)),
                   pl.BlockSpec((1, RW_HEADS, RW_HD, RW_HD), lambda b, c: (b, 0, 0, 0))],
        out_shape=[jax.ShapeDtypeStruct((n_seq * n_chunks, L, MIX), BF16),
                   jax.ShapeDtypeStruct((n_seq, RW_HEADS, RW_HD, RW_HD), F32)],
        scratch_shapes=[pltpu.VMEM((RW_HEADS, RW_HD, RW_HD), F32),
                        pltpu.VMEM((HIST + max(L, HIST), width), F32)],
        compiler_params=_cparams(("parallel", "arbitrary")),
        name="rwkv7",
    )(proj3, proj3, proj3, proj3, hist, _rw_cols(p['rw_mu']),
      row(p['rw_w0']), p['rw_w2'].astype(BF16), row(p['rw_a0']), p['rw_a2'].astype(BF16), g2,
      row(p['rw_k_k']), row(p['rw_k_a']), row(p['rw_r_k'].reshape(-1)), row(p['rw_ln_w']), row(p['rw_ln_b']), s0)
    return out, s_new


def _shift_rows(x, d, fill):
    rows = lax.broadcasted_iota(jnp.int32, x.shape, 0)
    return jnp.where(rows >= d, pltpu.roll(x, d, axis=0), fill)


def _lru_body(x_ref, g_ref, hist_ref, cw_ref, cb_ref, wa_ref, ba_ref, wx_ref, bx_ref, lam_ref, h0_ref,
              out_ref, hout_ref, h_s, ext_s, *, L):
    c = pl.program_id(1)

    @pl.when(c == 0)
    def _():
        h_s[...] = h0_ref[0]
        ext_s[0:HIST, :] = hist_ref[0]

    ext_s[HIST:HIST + L, :] = x_ref[0]
    base = HIST - (CONV_W - 1)
    xc = ext_s[base:base + L, :] * cw_ref[0:1, :]
    for j in range(1, CONV_W):
        xc = xc + ext_s[base + j:base + j + L, :] * cw_ref[j:j + 1, :]
    ext_s[0:HIST, :] = ext_s[L:L + HIST, :]
    xc = xc + cb_ref[...]
    r = jax.nn.sigmoid(_dot(xc, wa_ref[...]) + ba_ref[...])
    i = jax.nn.sigmoid(_dot(xc, wx_ref[...]) + bx_ref[...])
    log_a = -LRU_C * r * jax.nn.softplus(-lam_ref[...])
    a = jnp.exp(log_a)
    u = jnp.sqrt(-jnp.tanh(log_a) * (a * a + 1.0)) * (i * xc)
    h_prev = h_s[...]
    if L % 8 == 0:
        acc_a, acc_h = a, u
        d = 1
        while d < L:
            acc_h = acc_a * _shift_rows(acc_h, d, 0.0) + acc_h
            acc_a = acc_a * _shift_rows(acc_a, d, 1.0)
            d *= 2
        hs = acc_a * h_prev + acc_h
        h_s[...] = hs[L - 1:L, :]
    else:
        rows = []
        for t in range(L):
            h_prev = a[t:t + 1, :] * h_prev + u[t:t + 1, :]
            rows.append(h_prev)
        hs = jnp.concatenate(rows, axis=0)
        h_s[...] = h_prev
    out_ref[0] = (hs * jax.nn.gelu(g_ref[0])).astype(out_ref.dtype)

    @pl.when(c == pl.num_programs(1) - 1)
    def _():
        hout_ref[0] = h_s[...]


def _block_diag(w):
    nb, bs, _ = w.shape
    eye = jnp.eye(nb, dtype=w.dtype)
    return (eye[:, None, :, None] * w[:, :, None, :]).reshape(nb * bs, nb * bs)


def _lru(proj3, blk0, n_seq, n_chunks, L, p, conv0, h0):
    cb = C_LRU // MIX
    hist = _hist_rows(conv0, n_seq, MIX)
    row = lambda vec: vec.reshape(1, MIX)
    full2 = lambda shape: pl.BlockSpec(shape, lambda b, c: (0, 0))
    out, h_new = pl.pallas_call(
        functools.partial(_lru_body, L=L),
        grid=(n_seq, n_chunks),
        in_specs=[pl.BlockSpec((1, L, MIX), lambda b, c: (blk0 + b * n_chunks + c, 0, cb)),
                  pl.BlockSpec((1, L, MIX), lambda b, c: (blk0 + b * n_chunks + c, 0, cb + 1)),
                  pl.BlockSpec((1, HIST, MIX), lambda b, c: (b, 0, 0)),
                  full2((CONV_W, MIX)), full2((1, MIX)),
                  full2((MIX, MIX)), full2((1, MIX)), full2((MIX, MIX)), full2((1, MIX)), full2((1, MIX)),
                  pl.BlockSpec((1, 1, MIX), lambda b, c: (b, 0, 0))],
        out_specs=[pl.BlockSpec((1, L, MIX), lambda b, c: (b * n_chunks + c, 0, 0)),
                   pl.BlockSpec((1, 1, MIX), lambda b, c: (b, 0, 0))],
        out_shape=[jax.ShapeDtypeStruct((n_seq * n_chunks, L, MIX), BF16),
                   jax.ShapeDtypeStruct((n_seq, 1, MIX), F32)],
        scratch_shapes=[pltpu.VMEM((1, MIX), F32),
                        pltpu.VMEM((HIST + max(L, HIST), MIX), F32)],
        compiler_params=_cparams(("parallel", "arbitrary")),
        name="rglru",
    )(proj3, proj3, hist, p['lru_conv_w'], row(p['lru_conv_b']),
      _block_diag(p['lru_wa']).astype(BF16), row(p['lru_ba']),
      _block_diag(p['lru_wx']).astype(BF16), row(p['lru_bx']), row(p['lru_lambda']),
      h0.reshape(n_seq, 1, MIX))
    return out, h_new.reshape(n_seq, MIX)


L_MLSTM = 128
L_DELTA = 64
L_RWKV = 64
L_LRU = 256


def _relayout_w_in(w):
    d = w.shape[0]
    w = w.astype(BF16)
    z = lambda n: jnp.zeros((d, n), BF16)
    segs = [w[:, _O_GATE:_O_GATE + 4 * D_MODEL],
            w[:, _O_ML:_O_ML + 4 * MIX],
            w[:, _O_DN:_O_DN + 4 * MIX],
            w[:, _O_LRU:_O_LRU + 2 * MIX],
            w[:, _O_RW:_O_RW + _RW_COLS],
            z(RWX_SMALL - (_RW_COLS - 3 * MIX)),
            w[:, _O_ML + 4 * MIX:_O_ML + 4 * MIX + 2 * N_HEADS],
            w[:, _O_DN + 4 * MIX:_O_DN + 4 * MIX + 2 * N_HEADS],
            z(RWX_BLOCK - RWX_SMALL - 4 * N_HEADS)]
    return jnp.concatenate(segs, axis=1)


def _row_tile(m, cap):
    return max(t for t in range(16, cap + 1, 16) if m % t == 0)


def _chunk_len(t, want):
    return math.gcd(t, want)


def _small_views(small, L):
    m = small.shape[0]
    small_c = small.reshape(m // L, L, LANE)
    small_r = jnp.swapaxes(small_c[:, :, 0:32], 1, 2)
    return small_c, small_r


def _mixers(proj, blk_row0, n_seq, t_len, p, st):
    m = proj.shape[0]
    ml_c, ml_n, ml_m, dn_s, dn_conv, lru_h, lru_conv, rw_s, rw_shift = st
    small = proj[:, C_RWX + RWX_SMALL:C_RWX + RWX_SMALL + LANE]

    def view(L):
        return proj.reshape(m // L, L, N_PROJ), blk_row0 // L, t_len // L

    L = _chunk_len(t_len, L_MLSTM)
    p3, blk0, nch = view(L)
    sc, sr = _small_views(small, L)
    out_ml, ml_c, ml_n, ml_m = _mlstm(p3, sc, sr, blk0, n_seq, nch, L, p['ml_b_i'], p['ml_b_f'], p['ml_norm'],
                                      ml_c, ml_n, ml_m)
    L = _chunk_len(t_len, L_DELTA)
    p3, blk0, nch = view(L)
    sc, sr = _small_views(small, L)
    out_dn, dn_s = _delta(p3, sc, sr, blk0, n_seq, nch, L, p['dn_conv'], p['dn_A_log'], p['dn_dt_bias'], p['dn_norm'],
                          dn_conv, dn_s)
    L = _chunk_len(t_len, L_LRU)
    p3, blk0, nch = view(L)
    out_lru, lru_h_new = _lru(p3, blk0, n_seq, nch, L, p, lru_conv, lru_h)
    L = _chunk_len(t_len, L_RWKV)
    p3, blk0, nch = view(L)
    out_rw, rw_s = _rwkv(p3, blk0, n_seq, nch, L, p, rw_shift, rw_s)

    rows = n_seq * t_len
    keep = CONV_W - 1
    assert t_len >= keep
    tail = jnp.stack([lax.slice(proj, (blk_row0 + t_len - keep + j, 0), (blk_row0 + rows, N_PROJ), (t_len, 1))
                      for j in range(keep)], axis=1)
    dn_conv = jnp.concatenate([dn_conv, tail[:, :, C_DN:C_DN + 3 * MIX]], axis=1)[:, -(CONV_W - 1):]
    lru_conv = jnp.concatenate([lru_conv, tail[:, :, C_LRU:C_LRU + MIX]], axis=1)[:, -(CONV_W - 1):]
    rw_shift = jnp.concatenate([tail[:, -1, C_RW:C_RW + 3 * MIX], tail[:, -1, C_RWX:C_RWX + _RW_COLS - 3 * MIX]], axis=-1)
    outs = tuple(o.reshape(rows, MIX) for o in (out_ml, out_dn, out_lru, out_rw))
    return outs, (ml_c, ml_n, ml_m, dn_s, dn_conv, lru_h_new, lru_conv, rw_s, rw_shift)


_STATE_SHAPES = ((N_HEADS, HEAD_DIM, HEAD_DIM), (N_HEADS, HEAD_DIM), (N_HEADS,), (N_HEADS, HEAD_DIM, HEAD_DIM),
                 (CONV_W - 1, 3 * MIX), (MIX,), (CONV_W - 1, MIX), (RW_HEADS, RW_HD, RW_HD), (_RW_COLS,))


def kernel(x_prompt, x_sample, state_mlstm_C, state_mlstm_n, state_mlstm_m, state_delta_S, state_delta_conv, state_rglru_h, state_rglru_conv, state_rwkv_S, state_rwkv_shift, norm_mix, w_in, ml_b_i, ml_b_f, ml_norm, dn_conv, dn_A_log, dn_dt_bias, dn_norm, lru_conv_w, lru_conv_b, lru_wa, lru_ba, lru_wx, lru_bx, lru_lambda, rw_mu, rw_w0, rw_w2, rw_a0, rw_a2, rw_g2, rw_k_k, rw_k_a, rw_r_k, rw_ln_w, rw_ln_b, w_branch, w_out, norm_ffn, w_ffn_gate, w_ffn_up, w_ffn_down, norm_final):
    params = {
        'ml_b_i': ml_b_i, 'ml_b_f': ml_b_f, 'ml_norm': ml_norm,
        'dn_conv': dn_conv, 'dn_A_log': dn_A_log, 'dn_dt_bias': dn_dt_bias, 'dn_norm': dn_norm,
        'lru_conv_w': lru_conv_w, 'lru_conv_b': lru_conv_b, 'lru_wa': lru_wa, 'lru_ba': lru_ba,
        'lru_wx': lru_wx, 'lru_bx': lru_bx, 'lru_lambda': lru_lambda,
        'rw_mu': rw_mu, 'rw_w0': rw_w0, 'rw_w2': rw_w2, 'rw_a0': rw_a0, 'rw_a2': rw_a2, 'rw_g2': rw_g2,
        'rw_k_k': rw_k_k, 'rw_k_a': rw_k_a, 'rw_r_k': rw_r_k, 'rw_ln_w': rw_ln_w, 'rw_ln_b': rw_ln_b,
    }
    depth = w_in.shape[0]
    bp, tp, d = x_prompt.shape
    bs, ts, _ = x_sample.shape
    n_p, n_s = bp * tp, bs * ts
    m = n_p + n_s
    x = jnp.concatenate([x_prompt.reshape(n_p, d), x_sample.reshape(n_s, d)], axis=0)
    tm = _row_tile(m, 1088)
    tm_norm = _row_tile(m, 544)
    sample_states = (state_mlstm_C, state_mlstm_n, state_mlstm_m, state_delta_S, state_delta_conv,
                     state_rglru_h, state_rglru_conv, state_rwkv_S, state_rwkv_shift)
    new_p = [[] for _ in _STATE_SHAPES]
    new_s = [[] for _ in _STATE_SHAPES]
    for l in range(depth):
        p = {name: w[l] for name, w in params.items()}
        h = _rmsnorm(x, norm_mix[l], BF16, tm_norm)
        proj = _matmul(h, _relayout_w_in(w_in[l]), tm, 1024)
        st_p = tuple(jnp.zeros((bp,) + shp, F32) for shp in _STATE_SHAPES)
        st_s = tuple(s[l] for s in sample_states)
        outs_p, st_p = _mixers(proj, 0, bp, tp, p, st_p)
        outs_s, st_s = _mixers(proj, n_p, bs, ts, p, st_s)
        for lst, s in zip(new_p, st_p):
            lst.append(s)
        for lst, s in zip(new_s, st_s):
            lst.append(s)
        branches = jnp.stack([jnp.concatenate([a, b], axis=0) for a, b in zip(outs_p, outs_s)])
        mix = _merge(proj, branches, w_branch[l].astype(BF16), tm, 1024)
        x = _matmul_residual(mix, w_out[l].astype(BF16), x, tm, 1024, 2048)
        h2 = _rmsnorm(x, norm_ffn[l], BF16, tm_norm)
        pad_ff = D_FF_PAD - D_FF
        wg = jnp.pad(w_ffn_gate[l].astype(BF16), ((0, 0), (0, pad_ff)))
        wu = jnp.pad(w_ffn_up[l].astype(BF16), ((0, 0), (0, pad_ff)))
        wd = jnp.pad(w_ffn_down[l].astype(BF16), ((0, pad_ff), (0, 0)))
        act = _ffn_up(h2, wg, wu, tm, 512)
        x = _matmul_residual(act, wd, x, tm, 1024, D_FF_PAD // 4)
    y = _rmsnorm(x, norm_final, F32, tm_norm)
    y_prompt = y[:n_p].reshape(bp, tp, d)
    y_sample = y[n_p:].reshape(bs, ts, d)
    return ((y_prompt, y_sample) + tuple(jnp.stack(lst) for lst in new_p)
            + tuple(jnp.stack(lst) for lst in new_s))
```

```python
import functools
import math

import jax
import jax.numpy as jnp
from jax import lax
from jax.experimental import pallas as pl
from jax.experimental.pallas import tpu as pltpu

F32 = jnp.float32
BF16 = jnp.bfloat16

D_MODEL = 4096
MIX = D_MODEL // 4
HEAD_DIM = 128
N_HEADS = MIX // HEAD_DIM
RW_HD = 64
RW_HEADS = MIX // RW_HD
RW_DECAY_RANK = 128
RW_A_RANK = 128
RW_GATE_RANK = 480
CONV_W = 4
LRU_C = 8.0
D_FF = -(-(8 * D_MODEL) // (3 * 256)) * 256
NORM_EPS = 1e-6
RW_LN_EPS = 64e-5

_ML_COLS = 4 * MIX + 2 * N_HEADS
_DN_COLS = 4 * MIX + 2 * N_HEADS
_LRU_COLS = 2 * MIX
_RW_COLS = 3 * MIX + RW_DECAY_RANK + RW_A_RANK + RW_GATE_RANK
_O_ML = 0
_O_DN = _ML_COLS
_O_LRU = _O_DN + _DN_COLS
_O_RW = _O_LRU + _LRU_COLS
_O_GATE = _O_RW + _RW_COLS

C_GATE = 0
C_ML = 4 * D_MODEL
C_DN = C_ML + 4 * MIX
C_LRU = C_DN + 4 * MIX
C_RW = C_LRU + 2 * MIX
C_RWX = C_RW + 3 * MIX
RWX_W = 0
RWX_A = RW_DECAY_RANK
RWX_G = RW_DECAY_RANK + RW_A_RANK
RWX_SMALL = 768
N_PROJ = C_RWX + 1024
D_FF_PAD = -(-D_FF // 1024) * 1024

VMEM_LIMIT_BYTES = 56 * 1024 * 1024
LANE = 128


def _cparams(sem):
    return pltpu.CompilerParams(dimension_semantics=sem, vmem_limit_bytes=VMEM_LIMIT_BYTES)


def _rmsnorm_body(x_ref, g_ref, o_ref):
    x = x_ref[...]
    y = x * lax.rsqrt(jnp.mean(x * x, axis=-1, keepdims=True) + NORM_EPS)
    o_ref[...] = (y * g_ref[...]).astype(o_ref.dtype)


def _rmsnorm(x, g, out_dtype, tm):
    m, d = x.shape
    return pl.pallas_call(
        _rmsnorm_body,
        grid=(m // tm,),
        in_specs=[pl.BlockSpec((tm, d), lambda i: (i, 0)), pl.BlockSpec((1, d), lambda i: (0, 0))],
        out_specs=pl.BlockSpec((tm, d), lambda i: (i, 0)),
        out_shape=jax.ShapeDtypeStruct((m, d), out_dtype),
        compiler_params=_cparams(("parallel",)),
        name="rmsnorm",
    )(x, g.reshape(1, d))


def _mm_body(x_ref, w_ref, o_ref):
    o_ref[...] = jnp.dot(x_ref[...], w_ref[0], preferred_element_type=F32).astype(o_ref.dtype)


def _matmul(x, w, layer, tm, tn, out_dtype=F32):
    m, k = x.shape
    n = w.shape[2]
    return pl.pallas_call(
        _mm_body,
        grid=(n // tn, m // tm),
        in_specs=[pl.BlockSpec((tm, k), lambda j, i: (i, 0)), pl.BlockSpec((1, k, tn), lambda j, i: (layer, 0, j))],
        out_specs=pl.BlockSpec((tm, tn), lambda j, i: (i, j)),
        out_shape=jax.ShapeDtypeStruct((m, n), out_dtype),
        compiler_params=_cparams(("parallel", "parallel")),
        name="matmul",
    )(x, w)


def _mm_res_body(x_ref, w_ref, r_ref, o_ref, acc_ref):
    kk = pl.program_id(2)

    @pl.when(kk == 0)
    def _():
        acc_ref[...] = r_ref[...]

    acc_ref[...] += jnp.dot(x_ref[...], w_ref[0], preferred_element_type=F32)

    @pl.when(kk == pl.num_programs(2) - 1)
    def _():
        o_ref[...] = acc_ref[...]


def _matmul_residual(x, w, layer, res, tm, tn, tk):
    m, k = x.shape
    n = w.shape[2]
    return pl.pallas_call(
        _mm_res_body,
        grid=(n // tn, m // tm, k // tk),
        in_specs=[pl.BlockSpec((tm, tk), lambda j, i, kk: (i, kk)),
                  pl.BlockSpec((1, tk, tn), lambda j, i, kk: (layer, kk, j)),
                  pl.BlockSpec((tm, tn), lambda j, i, kk: (i, j))],
        out_specs=pl.BlockSpec((tm, tn), lambda j, i, kk: (i, j)),
        out_shape=jax.ShapeDtypeStruct((m, n), F32),
        scratch_shapes=[pltpu.VMEM((tm, tn), F32)],
        compiler_params=_cparams(("parallel", "parallel", "arbitrary")),
        name="matmul_residual",
    )(x, w, res)


def _ffn_up_body(x_ref, wg_ref, wu_ref, o_ref):
    x = x_ref[...]
    g = jnp.dot(x, wg_ref[0], preferred_element_type=F32)
    u = jnp.dot(x, wu_ref[0], preferred_element_type=F32)
    o_ref[...] = (jax.nn.silu(g) * u).astype(o_ref.dtype)


def _ffn_up(x, wg, wu, layer, tm, tn):
    m, k = x.shape
    n = wg.shape[2]
    return pl.pallas_call(
        _ffn_up_body,
        grid=(n // tn, m // tm),
        in_specs=[pl.BlockSpec((tm, k), lambda j, i: (i, 0)),
                  pl.BlockSpec((1, k, tn), lambda j, i: (layer, 0, j)),
                  pl.BlockSpec((1, k, tn), lambda j, i: (layer, 0, j))],
        out_specs=pl.BlockSpec((tm, tn), lambda j, i: (i, j)),
        out_shape=jax.ShapeDtypeStruct((m, n), BF16),
        compiler_params=_cparams(("parallel", "parallel")),
        name="ffn_up",
    )(x, wg, wu)


def _merge_body(g_ref, b_ref, w_ref, o_ref, acc_ref):
    n = pl.program_id(2)
    term = jax.nn.sigmoid(g_ref[...]) * jnp.dot(b_ref[0], w_ref[0, 0], preferred_element_type=F32)

    @pl.when(n == 0)
    def _():
        acc_ref[...] = term

    @pl.when(n > 0)
    def _():
        acc_ref[...] += term

    @pl.when(n == pl.num_programs(2) - 1)
    def _():
        o_ref[...] = acc_ref[...].astype(o_ref.dtype)


def _merge(proj, branches, wb, layer, tm, tn):
    nb, m, k = branches.shape
    n = wb.shape[3]
    gblk = n // tn
    return pl.pallas_call(
        _merge_body,
        grid=(n // tn, m // tm, nb),
        in_specs=[pl.BlockSpec((tm, tn), lambda j, i, b: (i, b * gblk + j)),
                  pl.BlockSpec((1, tm, k), lambda j, i, b: (b, i, 0)),
                  pl.BlockSpec((1, 1, k, tn), lambda j, i, b: (layer, b, 0, j))],
        out_specs=pl.BlockSpec((tm, tn), lambda j, i, b: (i, j)),
        out_shape=jax.ShapeDtypeStruct((m, n), BF16),
        scratch_shapes=[pltpu.VMEM((tm, tn), F32)],
        compiler_params=_cparams(("parallel", "parallel", "arbitrary")),
        name="merge",
    )(proj, branches, wb)


def _tri_masks(L):
    row = lax.broadcasted_iota(jnp.int32, (L, L), 0)
    col = lax.broadcasted_iota(jnp.int32, (L, L), 1)
    return col <= row, col < row


def _dot(a, b):
    return jnp.dot(a.astype(BF16), b.astype(BF16), preferred_element_type=F32)


def _dot_nt(a, b):
    return lax.dot_general(a.astype(BF16), b.astype(BF16), (((1,), (1,)), ((), ())), preferred_element_type=F32)


def _dot_tn(a, b):
    return lax.dot_general(a.astype(BF16), b.astype(BF16), (((0,), (0,)), ((), ())), preferred_element_type=F32)


def _dot_exact(a, b):
    return jnp.dot(a, b, precision=lax.Precision.HIGHEST, preferred_element_type=F32)


def _unit_lower_inverse(ns, L):
    eye = (lax.broadcasted_iota(jnp.int32, (L, L), 0) == lax.broadcasted_iota(jnp.int32, (L, L), 1)).astype(F32)
    ms = [-n for n in ns]
    ts = [eye + m for m in ms]
    span = 2
    while span < L:
        ms = [_dot(m, m) for m in ms]
        ts = [t + _dot(t, m) for t, m in zip(ts, ms)]
        span *= 2
    return ts


def _mlstm_body(q_ref, k_ref, v_ref, o_ref, gc_ref, gr_ref, bc_ref, br_ref, norm_ref, c0_ref, n0_ref, m0_ref,
                out_ref, cout_ref, nout_ref, mout_ref, c_s, n_s, m_s, *, L):
    c = pl.program_id(1)

    @pl.when(c == 0)
    def _():
        c_s[...] = c0_ref[0]
        n_s[...] = n0_ref[0]
        m_s[...] = m0_ref[0]

    incl, _ = _tri_masks(L)
    tril = incl.astype(F32)
    gc = gc_ref[0]
    gr = gr_ref[0]
    ig_c = gc[:, 0:N_HEADS] + bc_ref[:, 0:N_HEADS]
    lf_c = jax.nn.log_sigmoid(gc[:, N_HEADS:2 * N_HEADS] + bc_ref[:, N_HEADS:2 * N_HEADS])
    ig_r = gr[0:N_HEADS, :] + br_ref[0:N_HEADS, :]
    lf_r = jax.nn.log_sigmoid(gr[N_HEADS:2 * N_HEADS, :] + br_ref[N_HEADS:2 * N_HEADS, :])
    b_c = _dot_exact(tril, lf_c)
    b_r = _dot_exact(lf_r, tril.T)
    m_all = m_s[...]
    heads = range(N_HEADS)
    sls = [slice(h * HEAD_DIM, (h + 1) * HEAD_DIM) for h in heads]
    qs = [q_ref[0, :, sl] * (HEAD_DIM ** -0.5) for sl in sls]
    ks = [k_ref[0, :, sl] for sl in sls]
    vs = [v_ref[0, :, sl] for sl in sls]
    c_old = [c_s[h] for h in heads]
    n_old = [n_s[h:h + 1, :] for h in heads]
    qk = [_dot_nt(qs[h], ks[h]) for h in heads]
    qc = [_dot(qs[h], c_old[h]) for h in heads]
    bcs = [b_c[:, h:h + 1] for h in heads]
    m_prev = [m_all[:, h:h + 1] for h in heads]
    log_d = [jnp.where(incl, bcs[h] - b_r[h:h + 1, :] + ig_r[h:h + 1, :], -jnp.inf) for h in heads]
    state_w = [bcs[h] + m_prev[h] for h in heads]
    m_t = [jnp.maximum(state_w[h], jnp.max(log_d[h], axis=-1, keepdims=True)) for h in heads]
    scores = [qk[h] * jnp.exp(log_d[h] - m_t[h]) for h in heads]
    sw = [jnp.exp(state_w[h] - m_t[h]) for h in heads]
    sv = [_dot(scores[h], vs[h]) for h in heads]
    m_last = [m_t[h][L - 1:L, :] for h in heads]
    b_last = [bcs[h][L - 1:L, :] for h in heads]
    kw = [ks[h] * jnp.exp(b_last[h] - bcs[h] + ig_c[:, h:h + 1] - m_last[h]) for h in heads]
    kv = [_dot_tn(kw[h], vs[h]) for h in heads]
    for h in heads:
        decay = jnp.exp(b_last[h] + m_prev[h] - m_last[h])
        c_s[h] = decay * c_old[h] + kv[h]
        n_s[h:h + 1, :] = decay * n_old[h] + jnp.sum(kw[h], axis=0, keepdims=True)
        num = sv[h] + sw[h] * qc[h]
        den = jnp.sum(scores[h], axis=-1, keepdims=True) + sw[h] * jnp.sum(qs[h] * n_old[h], axis=-1, keepdims=True)
        hid = num / jnp.maximum(jnp.abs(den), jnp.exp(-m_t[h]))
        hn = hid * lax.rsqrt(jnp.mean(hid * hid, axis=-1, keepdims=True) + NORM_EPS) * norm_ref[:, sls[h]]
        out_ref[0, :, sls[h]] = (jax.nn.sigmoid(o_ref[0, :, sls[h]]) * hn).astype(out_ref.dtype)
    m_s[...] = jnp.concatenate(m_last, axis=1)

    @pl.when(c == pl.num_programs(1) - 1)
    def _():
        cout_ref[0] = c_s[...]
        nout_ref[0] = n_s[...]
        mout_ref[0] = m_s[...]


def _mlstm(proj3, small_c, small_r, blk0, n_seq, n_chunks, L, b_i, b_f, ml_norm, c0, n0, m0):
    cb = C_ML // MIX
    bias_c = jnp.zeros((1, LANE), F32).at[0, 0:N_HEADS].set(b_i).at[0, N_HEADS:2 * N_HEADS].set(b_f)
    bias_r = jnp.zeros((32, 1), F32).at[0:N_HEADS, 0].set(b_i).at[N_HEADS:2 * N_HEADS, 0].set(b_f)

    def col(j):
        return pl.BlockSpec((1, L, MIX), lambda b, c: (blk0 + b * n_chunks + c, 0, cb + j))

    full2 = lambda shape: pl.BlockSpec(shape, lambda b, c: (0, 0))
    outs = pl.pallas_call(
        functools.partial(_mlstm_body, L=L),
        grid=(n_seq, n_chunks),
        in_specs=[col(0), col(1), col(2), col(3),
                  pl.BlockSpec((1, L, LANE), lambda b, c: (blk0 + b * n_chunks + c, 0, 0)),
                  pl.BlockSpec((1, 32, L), lambda b, c: (blk0 + b * n_chunks + c, 0, 0)),
                  full2((1, LANE)), full2((32, 1)), full2((1, MIX)),
                  pl.BlockSpec((1, N_HEADS, HEAD_DIM, HEAD_DIM), lambda b, c: (b, 0, 0, 0)),
                  pl.BlockSpec((1, N_HEADS, HEAD_DIM), lambda b, c: (b, 0, 0)),
                  pl.BlockSpec((1, 1, N_HEADS), lambda b, c: (b, 0, 0))],
        out_specs=[pl.BlockSpec((1, L, MIX), lambda b, c: (b * n_chunks + c, 0, 0)),
                   pl.BlockSpec((1, N_HEADS, HEAD_DIM, HEAD_DIM), lambda b, c: (b, 0, 0, 0)),
                   pl.BlockSpec((1, N_HEADS, HEAD_DIM), lambda b, c: (b, 0, 0)),
                   pl.BlockSpec((1, 1, N_HEADS), lambda b, c: (b, 0, 0))],
        out_shape=[jax.ShapeDtypeStruct((n_seq * n_chunks, L, MIX), BF16),
                   jax.ShapeDtypeStruct((n_seq, N_HEADS, HEAD_DIM, HEAD_DIM), F32),
                   jax.ShapeDtypeStruct((n_seq, N_HEADS, HEAD_DIM), F32),
                   jax.ShapeDtypeStruct((n_seq, 1, N_HEADS), F32)],
        scratch_shapes=[pltpu.VMEM((N_HEADS, HEAD_DIM, HEAD_DIM), F32),
                        pltpu.VMEM((N_HEADS, HEAD_DIM), F32),
                        pltpu.VMEM((1, N_HEADS), F32)],
        compiler_params=_cparams(("parallel", "arbitrary")),
        name="mlstm",
    )(proj3, proj3, proj3, proj3, small_c, small_r, bias_c, bias_r, ml_norm.reshape(1, MIX),
      c0, n0, m0.reshape(n_seq, 1, N_HEADS))
    out, c_new, n_new, m_new = outs
    return out, c_new, n_new, m_new.reshape(n_seq, N_HEADS)


HIST = 8


def _delta_body(q_ref, k_ref, v_ref, z_ref, gc_ref, gr_ref, bc_ref, br_ref, cw_ref, hist_ref, norm_ref, s0_ref,
                out_ref, sout_ref, cout_ref, s_s, ext_s, *, L):
    c = pl.program_id(1)

    @pl.when(c == 0)
    def _():
        s_s[...] = s0_ref[0]
        ext_s[0:HIST, :] = hist_ref[0]

    ext_s[HIST:HIST + L, 0:MIX] = q_ref[0]
    ext_s[HIST:HIST + L, MIX:2 * MIX] = k_ref[0]
    ext_s[HIST:HIST + L, 2 * MIX:3 * MIX] = v_ref[0]
    base = HIST - (CONV_W - 1)
    y = ext_s[base:base + L, :] * cw_ref[0:1, :]
    for j in range(1, CONV_W):
        y = y + ext_s[base + j:base + j + L, :] * cw_ref[j:j + 1, :]
    ext_s[0:HIST, :] = ext_s[L:L + HIST, :]
    qkv = jax.nn.silu(y)

    incl, strict = _tri_masks(L)
    tril = incl.astype(F32)
    gc = gc_ref[0]
    gr = gr_ref[0]
    a_log_c, dt_c = bc_ref[:, 0:N_HEADS], bc_ref[:, N_HEADS:2 * N_HEADS]
    a_log_r, dt_r = br_ref[0:N_HEADS, :], br_ref[N_HEADS:2 * N_HEADS, :]
    g_c = -jnp.exp(a_log_c) * jax.nn.softplus(gc[:, 2 * N_HEADS:3 * N_HEADS] + dt_c)
    g_r = -jnp.exp(a_log_r) * jax.nn.softplus(gr[2 * N_HEADS:3 * N_HEADS, :] + dt_r)
    beta_c = jax.nn.sigmoid(gc[:, 3 * N_HEADS:4 * N_HEADS])
    gcum_c = _dot_exact(tril, g_c)
    gcum_r = _dot_exact(g_r, tril.T)
    heads = range(N_HEADS)
    sls = [slice(h * HEAD_DIM, (h + 1) * HEAD_DIM) for h in heads]
    qs = [qkv[:, h * HEAD_DIM:(h + 1) * HEAD_DIM] for h in heads]
    ks = [qkv[:, MIX + h * HEAD_DIM:MIX + (h + 1) * HEAD_DIM] for h in heads]
    vs = [qkv[:, 2 * MIX + h * HEAD_DIM:2 * MIX + (h + 1) * HEAD_DIM] for h in heads]
    qs = [x * lax.rsqrt(jnp.sum(x * x, axis=-1, keepdims=True) + NORM_EPS) * (HEAD_DIM ** -0.5) for x in qs]
    ks = [x * lax.rsqrt(jnp.sum(x * x, axis=-1, keepdims=True) + NORM_EPS) for x in ks]
    gch = [gcum_c[:, h:h + 1] for h in heads]
    beta = [beta_c[:, h:h + 1] for h in heads]
    eg = [jnp.exp(g) for g in gch]
    dec_incl = [jnp.exp(jnp.where(incl, gch[h] - gcum_r[h:h + 1, :], -jnp.inf)) for h in heads]
    s_old = [s_s[h] for h in heads]
    if L % 8 == 0:
        prod = [_dot_nt(jnp.concatenate([ks[h], qs[h]], axis=0), ks[h]) for h in heads]
        kk = [x[0:L] for x in prod]
        qk = [x[L:2 * L] for x in prod]
    else:
        kk = [_dot_nt(ks[h], ks[h]) for h in heads]
        qk = [_dot_nt(qs[h], ks[h]) for h in heads]
    a_mat = [jnp.where(strict, beta[h] * kk[h] * dec_incl[h], 0.0) for h in heads]
    t_inv = _unit_lower_inverse(a_mat, L)
    uw = [_dot(t_inv[h], jnp.concatenate([vs[h] * beta[h], ks[h] * (beta[h] * eg[h])], axis=1)) for h in heads]
    u = [x[:, 0:HEAD_DIM] for x in uw]
    w = [x[:, HEAD_DIM:2 * HEAD_DIM] for x in uw]
    if L % 8 == 0:
        ws = [_dot(jnp.concatenate([w[h], qs[h] * eg[h]], axis=0), s_old[h]) for h in heads]
        w_s = [x[0:L] for x in ws]
        q_s = [x[L:2 * L] for x in ws]
    else:
        w_s = [_dot(w[h], s_old[h]) for h in heads]
        q_s = [_dot(qs[h] * eg[h], s_old[h]) for h in heads]
    delta = [u[h] - w_s[h] for h in heads]
    qkd = [_dot(qk[h] * dec_incl[h], delta[h]) for h in heads]
    g_last = [g[L - 1:L, :] for g in gch]
    kd = [_dot_tn(ks[h] * jnp.exp(g_last[h] - gch[h]), delta[h]) for h in heads]
    for h in heads:
        s_s[h] = jnp.exp(g_last[h]) * s_old[h] + kd[h]
        o = q_s[h] + qkd[h]
        on = o * lax.rsqrt(jnp.mean(o * o, axis=-1, keepdims=True) + NORM_EPS) * norm_ref[...]
        out_ref[0, :, sls[h]] = (on * jax.nn.silu(z_ref[0, :, sls[h]])).astype(out_ref.dtype)

    @pl.when(c == pl.num_programs(1) - 1)
    def _():
        sout_ref[0] = s_s[...]
        cout_ref[0] = ext_s[HIST + L - (CONV_W - 1):HIST + L, :]


def _hist_rows(state_rows, n_seq, width):
    k = state_rows.shape[1]
    return jnp.concatenate([jnp.zeros((n_seq, HIST - k, width), F32), state_rows], axis=1)


def _delta(proj3, small_c, small_r, blk0, n_seq, n_chunks, L, conv_w, a_log, dt_bias, dn_norm, conv0, s0):
    cb = C_DN // MIX
    bias_c = jnp.zeros((1, LANE), F32).at[0, 0:N_HEADS].set(a_log).at[0, N_HEADS:2 * N_HEADS].set(dt_bias)
    bias_r = jnp.zeros((32, 1), F32).at[0:N_HEADS, 0].set(a_log).at[N_HEADS:2 * N_HEADS, 0].set(dt_bias)
    hist = _hist_rows(conv0, n_seq, 3 * MIX)

    def col(j):
        return pl.BlockSpec((1, L, MIX), lambda b, c: (blk0 + b * n_chunks + c, 0, cb + j))

    full2 = lambda shape: pl.BlockSpec(shape, lambda b, c: (0, 0))
    out, s_new, conv_new = pl.pallas_call(
        functools.partial(_delta_body, L=L),
        grid=(n_seq, n_chunks),
        in_specs=[col(0), col(1), col(2), col(3),
                  pl.BlockSpec((1, L, LANE), lambda b, c: (blk0 + b * n_chunks + c, 0, 0)),
                  pl.BlockSpec((1, 32, L), lambda b, c: (blk0 + b * n_chunks + c, 0, 0)),
                  full2((1, LANE)), full2((32, 1)), full2((CONV_W, 3 * MIX)),
                  pl.BlockSpec((1, HIST, 3 * MIX), lambda b, c: (b, 0, 0)),
                  full2((1, HEAD_DIM)),
                  pl.BlockSpec((1, N_HEADS, HEAD_DIM, HEAD_DIM), lambda b, c: (b, 0, 0, 0))],
        out_specs=[pl.BlockSpec((1, L, MIX), lambda b, c: (b * n_chunks + c, 0, 0)),
                   pl.BlockSpec((1, N_HEADS, HEAD_DIM, HEAD_DIM), lambda b, c: (b, 0, 0, 0)),
                   pl.BlockSpec((1, CONV_W - 1, 3 * MIX), lambda b, c: (b, 0, 0))],
        out_shape=[jax.ShapeDtypeStruct((n_seq * n_chunks, L, MIX), BF16),
                   jax.ShapeDtypeStruct((n_seq, N_HEADS, HEAD_DIM, HEAD_DIM), F32),
                   jax.ShapeDtypeStruct((n_seq, CONV_W - 1, 3 * MIX), F32)],
        scratch_shapes=[pltpu.VMEM((N_HEADS, HEAD_DIM, HEAD_DIM), F32),
                        pltpu.VMEM((HIST + max(L, HIST), 3 * MIX), F32)],
        compiler_params=_cparams(("parallel", "arbitrary")),
        name="deltanet",
    )(proj3, proj3, proj3, proj3, small_c, small_r, bias_c, bias_r, conv_w, hist, dn_norm.reshape(1, HEAD_DIM), s0)
    return out, s_new, conv_new


RWX_BLOCK = 1024
RW_GATE_PAD = 512


def _rwkv_body(r_ref, k_ref, v_ref, x_ref, hist_ref, mu_ref, w0_ref, w2_ref, a0_ref, a2_ref, g2_ref,
               kk_ref, ka_ref, rk_ref, lnw_ref, lnb_ref, s0_ref,
               out_ref, sout_ref, shout_ref, s_s, ext_s, *, L):
    c = pl.program_id(1)
    width = 3 * MIX + RWX_BLOCK

    @pl.when(c == 0)
    def _():
        s_s[...] = s0_ref[0]
        ext_s[0:HIST, :] = hist_ref[0]

    ext_s[HIST:HIST + L, 0:MIX] = r_ref[0]
    ext_s[HIST:HIST + L, MIX:2 * MIX] = k_ref[0]
    ext_s[HIST:HIST + L, 2 * MIX:3 * MIX] = v_ref[0]
    ext_s[HIST:HIST + L, 3 * MIX:width] = x_ref[0]
    cur = ext_s[HIST:HIST + L, :]
    prev = ext_s[HIST - 1:HIST - 1 + L, :]
    ext_s[0:HIST, :] = ext_s[L:L + HIST, :]
    z = cur + (prev - cur) * mu_ref[...]
    r = z[:, 0:MIX]
    k = z[:, MIX:2 * MIX]
    v = z[:, 2 * MIX:3 * MIX]
    zw = z[:, 3 * MIX + RWX_W:3 * MIX + RWX_W + RW_DECAY_RANK]
    za = z[:, 3 * MIX + RWX_A:3 * MIX + RWX_A + RW_A_RANK]
    zg = z[:, 3 * MIX + RWX_G:3 * MIX + RWX_G + RW_GATE_PAD]
    w_log = -jax.nn.softplus(-(w0_ref[...] + _dot(jnp.tanh(zw), w2_ref[...]))) - 0.5
    a = jax.nn.sigmoid(a0_ref[...] + _dot(za, a2_ref[...]))
    gate = _dot(jax.nn.sigmoid(zg), g2_ref[...])
    kkk = k * kk_ref[...]
    k2 = k * (1.0 + (a - 1.0) * ka_ref[...])
    log_decay = -jnp.exp(w_log)

    incl, strict = _tri_masks(L)
    cum = _dot_exact(incl.astype(F32), log_decay)
    e_cum = jnp.exp(cum)
    e_prev = jnp.exp(cum - log_decay)
    e_neg = jnp.exp(-cum)
    e_last = e_cum[L - 1:L, :]
    heads = range(RW_HEADS)
    sls = [slice(h * RW_HD, (h + 1) * RW_HD) for h in heads]
    kkn = [kkk[:, sl] for sl in sls]
    kkn = [x * lax.rsqrt(jnp.sum(x * x, axis=-1, keepdims=True) + NORM_EPS) for x in kkn]
    rs = [r[:, sl] for sl in sls]
    ks = [k2[:, sl] for sl in sls]
    vs = [v[:, sl] for sl in sls]
    k_til = [kkn[h] * e_prev[:, sls[h]] for h in heads]
    r_til = [rs[h] * e_cum[:, sls[h]] for h in heads]
    k_hat = [ks[h] * e_neg[:, sls[h]] for h in heads]
    b_hat = [kkn[h] * a[:, sls[h]] * e_neg[:, sls[h]] for h in heads]
    s_old = [s_s[h] for h in heads]
    if L % 8 == 0:
        kr = [jnp.concatenate([k_til[h], r_til[h]], axis=0) for h in heads]
        mask2 = jnp.concatenate([strict, incl], axis=0)
        p_k = [jnp.where(mask2, _dot_nt(kr[h], k_hat[h]), 0.0) for h in heads]
        p_b = [jnp.where(mask2, _dot_nt(kr[h], b_hat[h]), 0.0) for h in heads]
        a_b = [x[0:L] for x in p_b]
        b_b = [x[L:2 * L] for x in p_b]
        kv = [_dot(p_k[h], vs[h]) for h in heads]
        ksd = [_dot_nt(kr[h], s_old[h]) for h in heads]
        inner = [ksd[h][0:L] + kv[h][0:L] for h in heads]
        y_part = [ksd[h][L:2 * L] + kv[h][L:2 * L] for h in heads]
    else:
        a_k = [jnp.where(strict, _dot_nt(k_til[h], k_hat[h]), 0.0) for h in heads]
        a_b = [jnp.where(strict, _dot_nt(k_til[h], b_hat[h]), 0.0) for h in heads]
        b_k = [jnp.where(incl, _dot_nt(r_til[h], k_hat[h]), 0.0) for h in heads]
        b_b = [jnp.where(incl, _dot_nt(r_til[h], b_hat[h]), 0.0) for h in heads]
        inner = [_dot_nt(k_til[h], s_old[h]) + _dot(a_k[h], vs[h]) for h in heads]
        y_part = [_dot_nt(r_til[h], s_old[h]) + _dot(b_k[h], vs[h]) for h in heads]
    t_inv = _unit_lower_inverse(a_b, L)
    u = [_dot(t_inv[h], inner[h]) for h in heads]
    bu = [_dot(b_b[h], u[h]) for h in heads]
    el = [e_last[:, sl] for sl in sls]
    if L % 8 == 0:
        upd = [_dot_tn(jnp.concatenate([vs[h], -u[h]], axis=0),
                       jnp.concatenate([k_hat[h] * el[h], b_hat[h] * el[h]], axis=0)) for h in heads]
    else:
        upd = [_dot_tn(vs[h], k_hat[h] * el[h]) - _dot_tn(u[h], b_hat[h] * el[h]) for h in heads]
    for h in heads:
        s_s[h] = s_old[h] * el[h] + upd[h]
        y = y_part[h] - bu[h]
        mean = jnp.mean(y, axis=-1, keepdims=True)
        var = jnp.mean(jnp.square(y - mean), axis=-1, keepdims=True)
        y = (y - mean) * lax.rsqrt(var + RW_LN_EPS) * lnw_ref[:, sls[h]] + lnb_ref[:, sls[h]]
        y = y + jnp.sum(rs[h] * ks[h] * rk_ref[:, sls[h]], axis=-1, keepdims=True) * vs[h]
        out_ref[0, :, sls[h]] = (y * gate[:, sls[h]]).astype(out_ref.dtype)

    @pl.when(c == pl.num_programs(1) - 1)
    def _():
        sout_ref[0] = s_s[...]
        shout_ref[0] = ext_s[HIST + L - 1:HIST + L, 0:_RW_COLS]


def _rw_cols(vec):
    pad = jnp.zeros((RWX_BLOCK - (_RW_COLS - 3 * MIX),), F32)
    return jnp.concatenate([vec, pad]).reshape(1, 3 * MIX + RWX_BLOCK)


def _rwkv(proj3, blk0, n_seq, n_chunks, L, p, shift0, s0):
    cb = C_RW // MIX
    width = 3 * MIX + RWX_BLOCK
    hist = jnp.concatenate([jnp.zeros((n_seq, HIST - 1, width), F32),
                            jnp.pad(shift0, ((0, 0), (0, width - _RW_COLS)))[:, None, :]], axis=1)
    g2 = jnp.pad(p['rw_g2'], ((0, RW_GATE_PAD - RW_GATE_RANK), (0, 0))).astype(BF16)
    row = lambda vec: vec.reshape(1, MIX)

    def col(j):
        return pl.BlockSpec((1, L, MIX), lambda b, c: (blk0 + b * n_chunks + c, 0, cb + j))

    full2 = lambda shape: pl.BlockSpec(shape, lambda b, c: (0, 0))
    out, s_new, shift_new = pl.pallas_call(
        functools.partial(_rwkv_body, L=L),
        grid=(n_seq, n_chunks),
        in_specs=[col(0), col(1), col(2), col(3),
                  pl.BlockSpec((1, HIST, width), lambda b, c: (b, 0, 0)),
                  full2((1, width)),
                  full2((1, MIX)), full2((RW_DECAY_RANK, MIX)),
                  full2((1, MIX)), full2((RW_A_RANK, MIX)),
                  full2((RW_GATE_PAD, MIX)),
                  full2((1, MIX)), full2((1, MIX)), full2((1, MIX)), full2((1, MIX)), full2((1, MIX)),
                  pl.BlockSpec((1, RW_HEADS, RW_HD, RW_HD), lambda b, c: (b, 0, 0, 0))],
        out_specs=[pl.BlockSpec((1, L, MIX), lambda b, c: (b * n_chunks + c, 0, 0)),
                   pl.BlockSpec((1, RW_HEADS, RW_HD, RW_HD), lambda b, c: (b, 0, 0, 0)),
                   pl.BlockSpec((1, 1, _RW_COLS), lambda b, c: (b, 0, 0))],
        out_shape=[jax.ShapeDtypeStruct((n_seq * n_chunks, L, MIX), BF16),
                   jax.ShapeDtypeStruct((n_seq, RW_HEADS, RW_HD, RW_HD), F32),
                   jax.ShapeDtypeStruct((n_seq, 1, _RW_COLS), F32)],
        scratch_shapes=[pltpu.VMEM((RW_HEADS, RW_HD, RW_HD), F32),
                        pltpu.VMEM((HIST + max(L, HIST), width), F32)],
        compiler_params=_cparams(("parallel", "arbitrary")),
        name="rwkv7",
    )(proj3, proj3, proj3, proj3, hist, _rw_cols(p['rw_mu']),
      row(p['rw_w0']), p['rw_w2'].astype(BF16), row(p['rw_a0']), p['rw_a2'].astype(BF16), g2,
      row(p['rw_k_k']), row(p['rw_k_a']), row(p['rw_r_k'].reshape(-1)), row(p['rw_ln_w']), row(p['rw_ln_b']), s0)
    return out, s_new, shift_new.reshape(n_seq, _RW_COLS)


def _shift_rows(x, d, fill):
    rows = lax.broadcasted_iota(jnp.int32, x.shape, 0)
    return jnp.where(rows >= d, pltpu.roll(x, d, axis=0), fill)


def _lru_body(x_ref, g_ref, hist_ref, cw_ref, cb_ref, wa_ref, ba_ref, wx_ref, bx_ref, lam_ref, h0_ref,
              out_ref, hout_ref, cout_ref, h_s, ext_s, *, L):
    c = pl.program_id(1)

    @pl.when(c == 0)
    def _():
        h_s[...] = h0_ref[0]
        ext_s[0:HIST, :] = hist_ref[0]

    ext_s[HIST:HIST + L, :] = x_ref[0]
    base = HIST - (CONV_W - 1)
    xc = ext_s[base:base + L, :] * cw_ref[0:1, :]
    for j in range(1, CONV_W):
        xc = xc + ext_s[base + j:base + j + L, :] * cw_ref[j:j + 1, :]
    ext_s[0:HIST, :] = ext_s[L:L + HIST, :]
    xc = xc + cb_ref[...]
    r = jax.nn.sigmoid(_dot(xc, wa_ref[...]) + ba_ref[...])
    i = jax.nn.sigmoid(_dot(xc, wx_ref[...]) + bx_ref[...])
    log_a = -LRU_C * r * jax.nn.softplus(-lam_ref[...])
    a = jnp.exp(log_a)
    u = jnp.sqrt(-jnp.tanh(log_a) * (a * a + 1.0)) * (i * xc)
    h_prev = h_s[...]
    if L % 8 == 0:
        acc_a, acc_h = a, u
        d = 1
        while d < L:
            acc_h = acc_a * _shift_rows(acc_h, d, 0.0) + acc_h
            acc_a = acc_a * _shift_rows(acc_a, d, 1.0)
            d *= 2
        hs = acc_a * h_prev + acc_h
        h_s[...] = hs[L - 1:L, :]
    else:
        rows = []
        for t in range(L):
            h_prev = a[t:t + 1, :] * h_prev + u[t:t + 1, :]
            rows.append(h_prev)
        hs = jnp.concatenate(rows, axis=0)
        h_s[...] = h_prev
    out_ref[0] = (hs * jax.nn.gelu(g_ref[0])).astype(out_ref.dtype)

    @pl.when(c == pl.num_programs(1) - 1)
    def _():
        hout_ref[0] = h_s[...]
        cout_ref[0] = ext_s[HIST + L - (CONV_W - 1):HIST + L, :]


def _block_diag(w):
    nb, bs, _ = w.shape
    eye = jnp.eye(nb, dtype=w.dtype)
    return (eye[:, None, :, None] * w[:, :, None, :]).reshape(nb * bs, nb * bs)


def _lru(proj3, blk0, n_seq, n_chunks, L, p, conv0, h0):
    cb = C_LRU // MIX
    hist = _hist_rows(conv0, n_seq, MIX)
    row = lambda vec: vec.reshape(1, MIX)
    full2 = lambda shape: pl.BlockSpec(shape, lambda b, c: (0, 0))
    out, h_new, conv_new = pl.pallas_call(
        functools.partial(_lru_body, L=L),
        grid=(n_seq, n_chunks),
        in_specs=[pl.BlockSpec((1, L, MIX), lambda b, c: (blk0 + b * n_chunks + c, 0, cb)),
                  pl.BlockSpec((1, L, MIX), lambda b, c: (blk0 + b * n_chunks + c, 0, cb + 1)),
                  pl.BlockSpec((1, HIST, MIX), lambda b, c: (b, 0, 0)),
                  full2((CONV_W, MIX)), full2((1, MIX)),
                  full2((MIX, MIX)), full2((1, MIX)), full2((MIX, MIX)), full2((1, MIX)), full2((1, MIX)),
                  pl.BlockSpec((1, 1, MIX), lambda b, c: (b, 0, 0))],
        out_specs=[pl.BlockSpec((1, L, MIX), lambda b, c: (b * n_chunks + c, 0, 0)),
                   pl.BlockSpec((1, 1, MIX), lambda b, c: (b, 0, 0)),
                   pl.BlockSpec((1, CONV_W - 1, MIX), lambda b, c: (b, 0, 0))],
        out_shape=[jax.ShapeDtypeStruct((n_seq * n_chunks, L, MIX), BF16),
                   jax.ShapeDtypeStruct((n_seq, 1, MIX), F32),
                   jax.ShapeDtypeStruct((n_seq, CONV_W - 1, MIX), F32)],
        scratch_shapes=[pltpu.VMEM((1, MIX), F32),
                        pltpu.VMEM((HIST + max(L, HIST), MIX), F32)],
        compiler_params=_cparams(("parallel", "arbitrary")),
        name="rglru",
    )(proj3, proj3, hist, p['lru_conv_w'], row(p['lru_conv_b']),
      _block_diag(p['lru_wa']).astype(BF16), row(p['lru_ba']),
      _block_diag(p['lru_wx']).astype(BF16), row(p['lru_bx']), row(p['lru_lambda']),
      h0.reshape(n_seq, 1, MIX))
    return out, h_new.reshape(n_seq, MIX), conv_new


L_MLSTM = 128
L_DELTA = 64
L_RWKV = 64
L_LRU = 256


def _relayout_w_in(w):
    z = lambda n: jnp.zeros(w.shape[:2] + (n,), BF16)
    seg = lambda a, n: w[:, :, a:a + n].astype(BF16)
    segs = [seg(_O_GATE, 4 * D_MODEL),
            seg(_O_ML, 4 * MIX),
            seg(_O_DN, 4 * MIX),
            seg(_O_LRU, 2 * MIX),
            seg(_O_RW, _RW_COLS),
            z(RWX_SMALL - (_RW_COLS - 3 * MIX)),
            seg(_O_ML + 4 * MIX, 2 * N_HEADS),
            seg(_O_DN + 4 * MIX, 2 * N_HEADS),
            z(RWX_BLOCK - RWX_SMALL - 4 * N_HEADS)]
    return jnp.concatenate(segs, axis=2)


def _row_tile(m, cap):
    return max(t for t in range(16, cap + 1, 16) if m % t == 0)


def _chunk_len(t, want):
    return math.gcd(t, want)


def _small_views(small, L):
    m = small.shape[0]
    small_c = small.reshape(m // L, L, LANE)
    small_r = jnp.swapaxes(small_c[:, :, 0:32], 1, 2)
    return small_c, small_r


def _mixers(proj, n_seq, t_len, p, st):
    m = proj.shape[0]
    ml_c, ml_n, ml_m, dn_s, dn_conv, lru_h, lru_conv, rw_s, rw_shift = st
    small = proj[:, C_RWX + RWX_SMALL:C_RWX + RWX_SMALL + LANE]

    def view(want):
        L = _chunk_len(t_len, want)
        assert m % L == 0
        return proj.reshape(m // L, L, N_PROJ), L, t_len // L

    p3, L, nch = view(L_MLSTM)
    sc, sr = _small_views(small, L)
    out_ml, ml_c, ml_n, ml_m = _mlstm(p3, sc, sr, 0, n_seq, nch, L, p['ml_b_i'], p['ml_b_f'], p['ml_norm'],
                                      ml_c, ml_n, ml_m)
    p3, L, nch = view(L_DELTA)
    sc, sr = _small_views(small, L)
    out_dn, dn_s, dn_conv = _delta(p3, sc, sr, 0, n_seq, nch, L, p['dn_conv'], p['dn_A_log'], p['dn_dt_bias'],
                                   p['dn_norm'], dn_conv, dn_s)
    p3, L, nch = view(L_LRU)
    out_lru, lru_h, lru_conv = _lru(p3, 0, n_seq, nch, L, p, lru_conv, lru_h)
    p3, L, nch = view(L_RWKV)
    out_rw, rw_s, rw_shift = _rwkv(p3, 0, n_seq, nch, L, p, rw_shift, rw_s)
    outs = tuple(o.reshape(n_seq * t_len, MIX) for o in (out_ml, out_dn, out_lru, out_rw))
    return outs, (ml_c, ml_n, ml_m, dn_s, dn_conv, lru_h, lru_conv, rw_s, rw_shift)


_STATE_SHAPES = ((N_HEADS, HEAD_DIM, HEAD_DIM), (N_HEADS, HEAD_DIM), (N_HEADS,), (N_HEADS, HEAD_DIM, HEAD_DIM),
                 (CONV_W - 1, 3 * MIX), (MIX,), (CONV_W - 1, MIX), (RW_HEADS, RW_HD, RW_HD), (_RW_COLS,))


def kernel(x_prompt, x_sample, state_mlstm_C, state_mlstm_n, state_mlstm_m, state_delta_S, state_delta_conv, state_rglru_h, state_rglru_conv, state_rwkv_S, state_rwkv_shift, norm_mix, w_in, ml_b_i, ml_b_f, ml_norm, dn_conv, dn_A_log, dn_dt_bias, dn_norm, lru_conv_w, lru_conv_b, lru_wa, lru_ba, lru_wx, lru_bx, lru_lambda, rw_mu, rw_w0, rw_w2, rw_a0, rw_a2, rw_g2, rw_k_k, rw_k_a, rw_r_k, rw_ln_w, rw_ln_b, w_branch, w_out, norm_ffn, w_ffn_gate, w_ffn_up, w_ffn_down, norm_final):
    params = {
        'ml_b_i': ml_b_i, 'ml_b_f': ml_b_f, 'ml_norm': ml_norm,
        'dn_conv': dn_conv, 'dn_A_log': dn_A_log, 'dn_dt_bias': dn_dt_bias, 'dn_norm': dn_norm,
        'lru_conv_w': lru_conv_w, 'lru_conv_b': lru_conv_b, 'lru_wa': lru_wa, 'lru_ba': lru_ba,
        'lru_wx': lru_wx, 'lru_bx': lru_bx, 'lru_lambda': lru_lambda,
        'rw_mu': rw_mu, 'rw_w0': rw_w0, 'rw_w2': rw_w2, 'rw_a0': rw_a0, 'rw_a2': rw_a2, 'rw_g2': rw_g2,
        'rw_k_k': rw_k_k, 'rw_k_a': rw_k_a, 'rw_r_k': rw_r_k, 'rw_ln_w': rw_ln_w, 'rw_ln_b': rw_ln_b,
    }
    depth = w_in.shape[0]
    bp, tp, d = x_prompt.shape
    bs, ts, _ = x_sample.shape
    n_p, n_s = bp * tp, bs * ts
    m = n_p + n_s
    x = jnp.concatenate([x_prompt.reshape(n_p, d), x_sample.reshape(n_s, d)], axis=0)
    tm = _row_tile(m, 1088)
    tm_norm = _row_tile(m, 544)
    sample_states = (state_mlstm_C, state_mlstm_n, state_mlstm_m, state_delta_S, state_delta_conv,
                     state_rglru_h, state_rglru_conv, state_rwkv_S, state_rwkv_shift)
    new_p = [[] for _ in _STATE_SHAPES]
    new_s = [[] for _ in _STATE_SHAPES]
    pad_ff = D_FF_PAD - D_FF
    w_in_b = _relayout_w_in(w_in)
    w_branch_b = w_branch.astype(BF16)
    w_out_b = w_out.astype(BF16)
    wg_b = jnp.pad(w_ffn_gate.astype(BF16), ((0, 0), (0, 0), (0, pad_ff)))
    wu_b = jnp.pad(w_ffn_up.astype(BF16), ((0, 0), (0, 0), (0, pad_ff)))
    wd_b = jnp.pad(w_ffn_down.astype(BF16), ((0, 0), (0, pad_ff), (0, 0)))
    for l in range(depth):
        p = {name: w[l] for name, w in params.items()}
        h = _rmsnorm(x, norm_mix[l], BF16, tm_norm)
        proj = _matmul(h, w_in_b, l, tm, 1024)
        st_p = tuple(jnp.zeros((bp,) + shp, F32) for shp in _STATE_SHAPES)
        st_s = tuple(s[l] for s in sample_states)
        outs_p, st_p = _mixers(proj, bp, tp, p, st_p)
        outs_s, st_s = _mixers(proj[n_p:], bs, ts, p, st_s)
        for lst, s in zip(new_p, st_p):
            lst.append(s)
        for lst, s in zip(new_s, st_s):
            lst.append(s)
        branches = jnp.stack([jnp.concatenate([a, b], axis=0) for a, b in zip(outs_p, outs_s)])
        mix = _merge(proj, branches, w_branch_b, l, tm, 1024)
        x = _matmul_residual(mix, w_out_b, l, x, tm, 1024, 2048)
        h2 = _rmsnorm(x, norm_ffn[l], BF16, tm_norm)
        act = _ffn_up(h2, wg_b, wu_b, l, tm, 512)
        x = _matmul_residual(act, wd_b, l, x, tm, 1024, D_FF_PAD // 4)
    y = _rmsnorm(x, norm_final, F32, tm_norm)
    y_prompt = y[:n_p].reshape(bp, tp, d)
    y_sample = y[n_p:].reshape(bs, ts, d)
    return ((y_prompt, y_sample) + tuple(jnp.stack(lst) for lst in new_p)
            + tuple(jnp.stack(lst) for lst in new_s))
```

```python
import functools
import math

import jax
import jax.numpy as jnp
from jax import lax
from jax.experimental import pallas as pl
from jax.experimental.pallas import tpu as pltpu

F32 = jnp.float32
BF16 = jnp.bfloat16

D_MODEL = 4096
MIX = D_MODEL // 4
HEAD_DIM = 128
N_HEADS = MIX // HEAD_DIM
RW_HD = 64
RW_HEADS = MIX // RW_HD
RW_DECAY_RANK = 128
RW_A_RANK = 128
RW_GATE_RANK = 480
CONV_W = 4
LRU_C = 8.0
D_FF = -(-(8 * D_MODEL) // (3 * 256)) * 256
NORM_EPS = 1e-6
RW_LN_EPS = 64e-5

_ML_COLS = 4 * MIX + 2 * N_HEADS
_DN_COLS = 4 * MIX + 2 * N_HEADS
_LRU_COLS = 2 * MIX
_RW_COLS = 3 * MIX + RW_DECAY_RANK + RW_A_RANK + RW_GATE_RANK
C_ML = 0
C_DN = _ML_COLS
C_LRU = C_DN + _DN_COLS
C_RW = C_LRU + _LRU_COLS
C_RWX = C_RW + 3 * MIX
C_GATE = C_RW + _RW_COLS
N_PROJ = C_GATE + 4 * D_MODEL
RW_CODES = _RW_COLS - 3 * MIX
RWX_W = 0
RWX_A = RW_DECAY_RANK
RWX_G = RW_DECAY_RANK + RW_A_RANK
D_FF_PAD = -(-D_FF // 1024) * 1024

VMEM_LIMIT_BYTES = 56 * 1024 * 1024
LANE = 128


def _cparams(sem):
    return pltpu.CompilerParams(dimension_semantics=sem, vmem_limit_bytes=VMEM_LIMIT_BYTES)


def _rmsnorm_body(x_ref, g_ref, o_ref):
    x = x_ref[...]
    y = x * lax.rsqrt(jnp.mean(x * x, axis=-1, keepdims=True) + NORM_EPS)
    o_ref[...] = (y * g_ref[...]).astype(o_ref.dtype)


def _rmsnorm(x, g, out_dtype, tm):
    m, d = x.shape
    return pl.pallas_call(
        _rmsnorm_body,
        grid=(m // tm,),
        in_specs=[pl.BlockSpec((tm, d), lambda i: (i, 0)), pl.BlockSpec((1, d), lambda i: (0, 0))],
        out_specs=pl.BlockSpec((tm, d), lambda i: (i, 0)),
        out_shape=jax.ShapeDtypeStruct((m, d), out_dtype),
        compiler_params=_cparams(("parallel",)),
        name="rmsnorm",
    )(x, g.reshape(1, d))


def _mm_body(x_ref, w_ref, o_ref):
    o_ref[...] = jnp.dot(x_ref[...], w_ref[0].astype(BF16), preferred_element_type=F32).astype(o_ref.dtype)


def _matmul(x, w, layer, tm, tn, out_dtype=F32):
    m, k = x.shape
    n = w.shape[2]
    return pl.pallas_call(
        _mm_body,
        grid=(m // tm, n // tn),
        in_specs=[pl.BlockSpec((tm, k), lambda i, j: (i, 0), pipeline_mode=pl.Buffered(1)),
                  pl.BlockSpec((1, k, tn), lambda i, j: (layer, 0, j))],
        out_specs=pl.BlockSpec((tm, tn), lambda i, j: (i, j)),
        out_shape=jax.ShapeDtypeStruct((m, n), out_dtype),
        compiler_params=_cparams(("parallel", "arbitrary")),
        name="matmul",
    )(x, w)


def _mm_res_body(x_ref, w_ref, r_ref, o_ref, acc_ref):
    kk = pl.program_id(2)

    @pl.when(kk == 0)
    def _():
        acc_ref[...] = r_ref[...]

    acc_ref[...] += jnp.dot(x_ref[...], w_ref[0].astype(BF16), preferred_element_type=F32)

    @pl.when(kk == pl.num_programs(2) - 1)
    def _():
        o_ref[...] = acc_ref[...]


def _matmul_residual(x, w, layer, res, tm, tn, tk):
    m, k = x.shape
    n = w.shape[2]
    return pl.pallas_call(
        _mm_res_body,
        grid=(n // tn, m // tm, k // tk),
        in_specs=[pl.BlockSpec((tm, tk), lambda j, i, kk: (i, kk)),
                  pl.BlockSpec((1, tk, tn), lambda j, i, kk: (layer, kk, j)),
                  pl.BlockSpec((tm, tn), lambda j, i, kk: (i, j))],
        out_specs=pl.BlockSpec((tm, tn), lambda j, i, kk: (i, j)),
        out_shape=jax.ShapeDtypeStruct((m, n), F32),
        scratch_shapes=[pltpu.VMEM((tm, tn), F32)],
        compiler_params=_cparams(("parallel", "parallel", "arbitrary")),
        name="matmul_residual",
    )(x, w, res)


def _ffn_up_body(x_ref, wg_ref, wu_ref, o_ref, *, n_valid):
    x = x_ref[...]
    g = jnp.dot(x, wg_ref[0].astype(BF16), preferred_element_type=F32)
    u = jnp.dot(x, wu_ref[0].astype(BF16), preferred_element_type=F32)
    act = jax.nn.silu(g) * u
    o_ref[...] = jnp.where(pl.program_id(1) < n_valid, act, 0.0).astype(o_ref.dtype)


def _ffn_up(x, wg, wu, layer, tm, tn, n_out):
    m, k = x.shape
    n = wg.shape[2]
    assert n % tn == 0 and n_out % tn == 0
    n_valid = n // tn
    wspec = pl.BlockSpec((1, k, tn), lambda i, j: (layer, 0, jnp.minimum(j, n_valid - 1)))
    return pl.pallas_call(
        functools.partial(_ffn_up_body, n_valid=n_valid),
        grid=(m // tm, n_out // tn),
        in_specs=[pl.BlockSpec((tm, k), lambda i, j: (i, 0), pipeline_mode=pl.Buffered(1)), wspec, wspec],
        out_specs=pl.BlockSpec((tm, tn), lambda i, j: (i, j)),
        out_shape=jax.ShapeDtypeStruct((m, n_out), BF16),
        compiler_params=_cparams(("parallel", "arbitrary")),
        name="ffn_up",
    )(x, wg, wu)


def _merge_body(g_ref, b_ref, w_ref, o_ref, acc_ref):
    n = pl.program_id(2)
    term = jax.nn.sigmoid(g_ref[...]) * jnp.dot(b_ref[0], w_ref[0, 0].astype(BF16), preferred_element_type=F32)

    @pl.when(n == 0)
    def _():
        acc_ref[...] = term

    @pl.when(n > 0)
    def _():
        acc_ref[...] += term

    @pl.when(n == pl.num_programs(2) - 1)
    def _():
        o_ref[...] = acc_ref[...].astype(o_ref.dtype)


def _merge(proj, branches, wb, layer, tm, tn):
    nb, m, k = branches.shape
    n = wb.shape[3]
    return pl.pallas_call(
        _merge_body,
        grid=(n // tn, m // tm, nb),
        in_specs=[pl.BlockSpec((pl.Element(tm), pl.Element(tn)),
                               lambda j, i, b: (i * tm, (C_GATE // LANE + b * (n // LANE) + j * (tn // LANE)) * LANE)),
                  pl.BlockSpec((1, tm, k), lambda j, i, b: (b, i, 0)),
                  pl.BlockSpec((1, 1, k, tn), lambda j, i, b: (layer, b, 0, j))],
        out_specs=pl.BlockSpec((tm, tn), lambda j, i, b: (i, j)),
        out_shape=jax.ShapeDtypeStruct((m, n), BF16),
        scratch_shapes=[pltpu.VMEM((tm, tn), F32)],
        compiler_params=_cparams(("parallel", "parallel", "arbitrary")),
        name="merge",
    )(proj, branches, wb)


def _seg(L, n_chunks, blk0, col, width=MIX):
    start = col // LANE * LANE
    off = col - start
    win = -(-(off + width) // LANE) * LANE
    spec = pl.BlockSpec((pl.Element(1), pl.Element(L), pl.Element(win)),
                        lambda b, c: (blk0 + b * n_chunks + c, 0, start))
    return spec, off


def _state_in(state, tail):
    zeros = (0,) * len(tail)
    if isinstance(state, tuple):
        arr, layer = state
        return arr, pl.BlockSpec((None, 1) + tail, lambda b, c: (layer, b) + zeros)
    return state, pl.BlockSpec((1,) + tail, lambda b, c: (b,) + zeros)


def _tri_masks(L):
    row = lax.broadcasted_iota(jnp.int32, (L, L), 0)
    col = lax.broadcasted_iota(jnp.int32, (L, L), 1)
    return col <= row, col < row


def _dot(a, b):
    return jnp.dot(a.astype(BF16), b.astype(BF16), preferred_element_type=F32)


def _dot_nt(a, b):
    return lax.dot_general(a.astype(BF16), b.astype(BF16), (((1,), (1,)), ((), ())), preferred_element_type=F32)


def _dot_tn(a, b):
    return lax.dot_general(a.astype(BF16), b.astype(BF16), (((0,), (0,)), ((), ())), preferred_element_type=F32)


def _dot_exact(a, b):
    return jnp.dot(a, b, precision=lax.Precision.HIGHEST, preferred_element_type=F32)


def _unit_lower_inverse(ns, L):
    eye = (lax.broadcasted_iota(jnp.int32, (L, L), 0) == lax.broadcasted_iota(jnp.int32, (L, L), 1)).astype(F32)
    ms = [-n for n in ns]
    ts = [eye + m for m in ms]
    span = 2
    while span < L:
        ms = [_dot(m, m) for m in ms]
        ts = [t + _dot(t, m) for t, m in zip(ts, ms)]
        span *= 2
    return ts


def _mlstm_body(q_ref, k_ref, v_ref, o_ref, gc_ref, gr_ref, bc_ref, br_ref, norm_ref, c0_ref, n0_ref, m0_ref,
                out_ref, cout_ref, nout_ref, mout_ref, c_s, n_s, m_s, *, L):
    c = pl.program_id(1)

    @pl.when(c == 0)
    def _():
        c_s[...] = c0_ref[0]
        n_s[...] = n0_ref[0]
        m_s[...] = m0_ref[0]

    incl, _ = _tri_masks(L)
    tril = incl.astype(F32)
    gc = gc_ref[0]
    gr = gr_ref[0]
    ig_c = gc[:, 0:N_HEADS] + bc_ref[:, 0:N_HEADS]
    lf_c = jax.nn.log_sigmoid(gc[:, N_HEADS:2 * N_HEADS] + bc_ref[:, N_HEADS:2 * N_HEADS])
    ig_r = gr[0:N_HEADS, :] + br_ref[0:N_HEADS, :]
    lf_r = jax.nn.log_sigmoid(gr[N_HEADS:2 * N_HEADS, :] + br_ref[N_HEADS:2 * N_HEADS, :])
    b_c = _dot_exact(tril, lf_c)
    b_r = _dot_exact(lf_r, tril.T)
    m_all = m_s[...]
    heads = range(N_HEADS)
    sls = [slice(h * HEAD_DIM, (h + 1) * HEAD_DIM) for h in heads]
    qs = [q_ref[0, :, sl] * (HEAD_DIM ** -0.5) for sl in sls]
    ks = [k_ref[0, :, sl] for sl in sls]
    vs = [v_ref[0, :, sl] for sl in sls]
    c_old = [c_s[h] for h in heads]
    n_old = [n_s[h:h + 1, :] for h in heads]
    qk = [_dot_nt(qs[h], ks[h]) for h in heads]
    qc = [_dot(qs[h], c_old[h]) for h in heads]
    bcs = [b_c[:, h:h + 1] for h in heads]
    m_prev = [m_all[:, h:h + 1] for h in heads]
    log_d = [jnp.where(incl, bcs[h] - b_r[h:h + 1, :] + ig_r[h:h + 1, :], -jnp.inf) for h in heads]
    state_w = [bcs[h] + m_prev[h] for h in heads]
    m_t = [jnp.maximum(state_w[h], jnp.max(log_d[h], axis=-1, keepdims=True)) for h in heads]
    scores = [qk[h] * jnp.exp(log_d[h] - m_t[h]) for h in heads]
    sw = [jnp.exp(state_w[h] - m_t[h]) for h in heads]
    sv = [_dot(scores[h], vs[h]) for h in heads]
    m_last = [m_t[h][L - 1:L, :] for h in heads]
    b_last = [bcs[h][L - 1:L, :] for h in heads]
    kw = [ks[h] * jnp.exp(b_last[h] - bcs[h] + ig_c[:, h:h + 1] - m_last[h]) for h in heads]
    kv = [_dot_tn(kw[h], vs[h]) for h in heads]
    for h in heads:
        decay = jnp.exp(b_last[h] + m_prev[h] - m_last[h])
        c_s[h] = decay * c_old[h] + kv[h]
        n_s[h:h + 1, :] = decay * n_old[h] + jnp.sum(kw[h], axis=0, keepdims=True)
        num = sv[h] + sw[h] * qc[h]
        den = jnp.sum(scores[h], axis=-1, keepdims=True) + sw[h] * jnp.sum(qs[h] * n_old[h], axis=-1, keepdims=True)
        hid = num / jnp.maximum(jnp.abs(den), jnp.exp(-m_t[h]))
        hn = hid * lax.rsqrt(jnp.mean(hid * hid, axis=-1, keepdims=True) + NORM_EPS) * norm_ref[:, sls[h]]
        out_ref[0, :, sls[h]] = (jax.nn.sigmoid(o_ref[0, :, sls[h]]) * hn).astype(out_ref.dtype)
    m_s[...] = jnp.concatenate(m_last, axis=1)

    @pl.when(c == pl.num_programs(1) - 1)
    def _():
        cout_ref[0] = c_s[...]
        nout_ref[0] = n_s[...]
        mout_ref[0] = m_s[...]


def _mlstm(proj3, small_c, small_r, blk0, n_seq, n_chunks, L, b_i, b_f, ml_norm, c0, n0, m0):
    assert C_ML % LANE == 0
    bias_c = jnp.zeros((1, LANE), F32).at[0, 0:N_HEADS].set(b_i).at[0, N_HEADS:2 * N_HEADS].set(b_f)
    bias_r = jnp.zeros((32, 1), F32).at[0:N_HEADS, 0].set(b_i).at[N_HEADS:2 * N_HEADS, 0].set(b_f)

    def col(j):
        return _seg(L, n_chunks, blk0, C_ML + j * MIX)[0]

    full2 = lambda shape: pl.BlockSpec(shape, lambda b, c: (0, 0))
    c0, c0_spec = _state_in(c0, (N_HEADS, HEAD_DIM, HEAD_DIM))
    outs = pl.pallas_call(
        functools.partial(_mlstm_body, L=L),
        grid=(n_seq, n_chunks),
        in_specs=[col(0), col(1), col(2), col(3),
                  pl.BlockSpec((1, L, 32), lambda b, c: (blk0 + b * n_chunks + c, 0, 0)),
                  pl.BlockSpec((1, 32, L), lambda b, c: (blk0 + b * n_chunks + c, 0, 0)),
                  full2((1, LANE)), full2((32, 1)), full2((1, MIX)),
                  c0_spec,
                  pl.BlockSpec((1, N_HEADS, HEAD_DIM), lambda b, c: (b, 0, 0)),
                  pl.BlockSpec((1, 1, N_HEADS), lambda b, c: (b, 0, 0))],
        out_specs=[pl.BlockSpec((1, L, MIX), lambda b, c: (b * n_chunks + c, 0, 0)),
                   pl.BlockSpec((1, N_HEADS, HEAD_DIM, HEAD_DIM), lambda b, c: (b, 0, 0, 0)),
                   pl.BlockSpec((1, N_HEADS, HEAD_DIM), lambda b, c: (b, 0, 0)),
                   pl.BlockSpec((1, 1, N_HEADS), lambda b, c: (b, 0, 0))],
        out_shape=[jax.ShapeDtypeStruct((n_seq * n_chunks, L, MIX), BF16),
                   jax.ShapeDtypeStruct((n_seq, N_HEADS, HEAD_DIM, HEAD_DIM), F32),
                   jax.ShapeDtypeStruct((n_seq, N_HEADS, HEAD_DIM), F32),
                   jax.ShapeDtypeStruct((n_seq, 1, N_HEADS), F32)],
        scratch_shapes=[pltpu.VMEM((N_HEADS, HEAD_DIM, HEAD_DIM), F32),
                        pltpu.VMEM((N_HEADS, HEAD_DIM), F32),
                        pltpu.VMEM((1, N_HEADS), F32)],
        compiler_params=_cparams(("parallel", "arbitrary")),
        name="mlstm",
    )(proj3, proj3, proj3, proj3, small_c, small_r, bias_c, bias_r, ml_norm.reshape(1, MIX),
      c0, n0, m0.reshape(n_seq, 1, N_HEADS))
    out, c_new, n_new, m_new = outs
    return out, c_new, n_new, m_new.reshape(n_seq, N_HEADS)


HIST = 8


def _delta_body(q_ref, k_ref, v_ref, z_ref, gc_ref, gr_ref, bc_ref, br_ref, cw_ref, hist_ref, norm_ref, s0_ref,
                out_ref, sout_ref, cout_ref, s_s, ext_s, *, L, off):
    c = pl.program_id(1)

    @pl.when(c == 0)
    def _():
        s_s[...] = s0_ref[0]
        ext_s[0:HIST, :] = hist_ref[0]

    ext_s[HIST:HIST + L, 0:MIX] = q_ref[0, :, off:off + MIX]
    ext_s[HIST:HIST + L, MIX:2 * MIX] = k_ref[0, :, off:off + MIX]
    ext_s[HIST:HIST + L, 2 * MIX:3 * MIX] = v_ref[0, :, off:off + MIX]
    z = z_ref[0, :, off:off + MIX]
    base = HIST - (CONV_W - 1)
    y = ext_s[base:base + L, :] * cw_ref[0:1, :]
    for j in range(1, CONV_W):
        y = y + ext_s[base + j:base + j + L, :] * cw_ref[j:j + 1, :]
    ext_s[0:HIST, :] = ext_s[L:L + HIST, :]
    qkv = jax.nn.silu(y)

    incl, strict = _tri_masks(L)
    tril = incl.astype(F32)
    gc = gc_ref[0]
    gr = gr_ref[0]
    a_log_c, dt_c = bc_ref[:, 0:N_HEADS], bc_ref[:, N_HEADS:2 * N_HEADS]
    a_log_r, dt_r = br_ref[0:N_HEADS, :], br_ref[N_HEADS:2 * N_HEADS, :]
    g_c = -jnp.exp(a_log_c) * jax.nn.softplus(gc[:, 2 * N_HEADS:3 * N_HEADS] + dt_c)
    g_r = -jnp.exp(a_log_r) * jax.nn.softplus(gr[2 * N_HEADS:3 * N_HEADS, :] + dt_r)
    beta_c = jax.nn.sigmoid(gc[:, 3 * N_HEADS:4 * N_HEADS])
    gcum_c = _dot_exact(tril, g_c)
    gcum_r = _dot_exact(g_r, tril.T)
    heads = range(N_HEADS)
    sls = [slice(h * HEAD_DIM, (h + 1) * HEAD_DIM) for h in heads]
    qs = [qkv[:, h * HEAD_DIM:(h + 1) * HEAD_DIM] for h in heads]
    ks = [qkv[:, MIX + h * HEAD_DIM:MIX + (h + 1) * HEAD_DIM] for h in heads]
    vs = [qkv[:, 2 * MIX + h * HEAD_DIM:2 * MIX + (h + 1) * HEAD_DIM] for h in heads]
    qs = [x * lax.rsqrt(jnp.sum(x * x, axis=-1, keepdims=True) + NORM_EPS) * (HEAD_DIM ** -0.5) for x in qs]
    ks = [x * lax.rsqrt(jnp.sum(x * x, axis=-1, keepdims=True) + NORM_EPS) for x in ks]
    gch = [gcum_c[:, h:h + 1] for h in heads]
    beta = [beta_c[:, h:h + 1] for h in heads]
    eg = [jnp.exp(g) for g in gch]
    dec_incl = [jnp.exp(jnp.where(incl, gch[h] - gcum_r[h:h + 1, :], -jnp.inf)) for h in heads]
    s_old = [s_s[h] for h in heads]
    if L % 8 == 0:
        prod = [_dot_nt(jnp.concatenate([ks[h], qs[h]], axis=0), ks[h]) for h in heads]
        kk = [x[0:L] for x in prod]
        qk = [x[L:2 * L] for x in prod]
    else:
        kk = [_dot_nt(ks[h], ks[h]) for h in heads]
        qk = [_dot_nt(qs[h], ks[h]) for h in heads]
    a_mat = [jnp.where(strict, beta[h] * kk[h] * dec_incl[h], 0.0) for h in heads]
    t_inv = _unit_lower_inverse(a_mat, L)
    uw = [_dot(t_inv[h], jnp.concatenate([vs[h] * beta[h], ks[h] * (beta[h] * eg[h])], axis=1)) for h in heads]
    u = [x[:, 0:HEAD_DIM] for x in uw]
    w = [x[:, HEAD_DIM:2 * HEAD_DIM] for x in uw]
    if L % 8 == 0:
        ws = [_dot(jnp.concatenate([w[h], qs[h] * eg[h]], axis=0), s_old[h]) for h in heads]
        w_s = [x[0:L] for x in ws]
        q_s = [x[L:2 * L] for x in ws]
    else:
        w_s = [_dot(w[h], s_old[h]) for h in heads]
        q_s = [_dot(qs[h] * eg[h], s_old[h]) for h in heads]
    delta = [u[h] - w_s[h] for h in heads]
    qkd = [_dot(qk[h] * dec_incl[h], delta[h]) for h in heads]
    g_last = [g[L - 1:L, :] for g in gch]
    kd = [_dot_tn(ks[h] * jnp.exp(g_last[h] - gch[h]), delta[h]) for h in heads]
    for h in heads:
        s_s[h] = jnp.exp(g_last[h]) * s_old[h] + kd[h]
        o = q_s[h] + qkd[h]
        on = o * lax.rsqrt(jnp.mean(o * o, axis=-1, keepdims=True) + NORM_EPS) * norm_ref[...]
        out_ref[0, :, sls[h]] = (on * jax.nn.silu(z[:, sls[h]])).astype(out_ref.dtype)

    @pl.when(c == pl.num_programs(1) - 1)
    def _():
        sout_ref[0] = s_s[...]
        cout_ref[0] = ext_s[HIST + L - (CONV_W - 1):HIST + L, :]


def _hist_rows(state_rows, n_seq, width):
    k = state_rows.shape[1]
    return jnp.concatenate([jnp.zeros((n_seq, HIST - k, width), F32), state_rows], axis=1)


def _delta(proj3, small_c, small_r, blk0, n_seq, n_chunks, L, conv_w, a_log, dt_bias, dn_norm, conv0, s0):
    bias_c = jnp.zeros((1, LANE), F32).at[0, 0:N_HEADS].set(a_log).at[0, N_HEADS:2 * N_HEADS].set(dt_bias)
    bias_r = jnp.zeros((32, 1), F32).at[0:N_HEADS, 0].set(a_log).at[N_HEADS:2 * N_HEADS, 0].set(dt_bias)
    hist = _hist_rows(conv0, n_seq, 3 * MIX)
    segs = [_seg(L, n_chunks, blk0, C_DN + j * MIX) for j in range(4)]
    off = segs[0][1]
    assert all(s[1] == off for s in segs)

    full2 = lambda shape: pl.BlockSpec(shape, lambda b, c: (0, 0))
    s0, s0_spec = _state_in(s0, (N_HEADS, HEAD_DIM, HEAD_DIM))
    out, s_new, conv_new = pl.pallas_call(
        functools.partial(_delta_body, L=L, off=off),
        grid=(n_seq, n_chunks),
        in_specs=[segs[0][0], segs[1][0], segs[2][0], segs[3][0],
                  pl.BlockSpec((1, L, 32), lambda b, c: (blk0 + b * n_chunks + c, 0, 0)),
                  pl.BlockSpec((1, 32, L), lambda b, c: (blk0 + b * n_chunks + c, 0, 0)),
                  full2((1, LANE)), full2((32, 1)), full2((CONV_W, 3 * MIX)),
                  pl.BlockSpec((1, HIST, 3 * MIX), lambda b, c: (b, 0, 0)),
                  full2((1, HEAD_DIM)),
                  s0_spec],
        out_specs=[pl.BlockSpec((1, L, MIX), lambda b, c: (b * n_chunks + c, 0, 0)),
                   pl.BlockSpec((1, N_HEADS, HEAD_DIM, HEAD_DIM), lambda b, c: (b, 0, 0, 0)),
                   pl.BlockSpec((1, CONV_W - 1, 3 * MIX), lambda b, c: (b, 0, 0))],
        out_shape=[jax.ShapeDtypeStruct((n_seq * n_chunks, L, MIX), BF16),
                   jax.ShapeDtypeStruct((n_seq, N_HEADS, HEAD_DIM, HEAD_DIM), F32),
                   jax.ShapeDtypeStruct((n_seq, CONV_W - 1, 3 * MIX), F32)],
        scratch_shapes=[pltpu.VMEM((N_HEADS, HEAD_DIM, HEAD_DIM), F32),
                        pltpu.VMEM((HIST + max(L, HIST), 3 * MIX), F32)],
        compiler_params=_cparams(("parallel", "arbitrary")),
        name="deltanet",
    )(proj3, proj3, proj3, proj3, small_c, small_r, bias_c, bias_r, conv_w, hist, dn_norm.reshape(1, HEAD_DIM), s0)
    return out, s_new, conv_new


RWX_BLOCK = 768
RW_GATE_PAD = 512


def _rwkv_body(r_ref, k_ref, v_ref, x_ref, hist_ref, mu_ref, w0_ref, w2_ref, a0_ref, a2_ref, g2_ref,
               kk_ref, ka_ref, rk_ref, lnw_ref, lnb_ref, s0_ref,
               out_ref, sout_ref, shout_ref, s_s, ext_s, *, L, off):
    c = pl.program_id(1)
    width = 3 * MIX + RWX_BLOCK

    @pl.when(c == 0)
    def _():
        s_s[...] = s0_ref[0]
        ext_s[0:HIST, :] = hist_ref[0]

    ext_s[HIST:HIST + L, 0:MIX] = r_ref[0, :, off:off + MIX]
    ext_s[HIST:HIST + L, MIX:2 * MIX] = k_ref[0, :, off:off + MIX]
    ext_s[HIST:HIST + L, 2 * MIX:3 * MIX] = v_ref[0, :, off:off + MIX]
    ext_s[HIST:HIST + L, 3 * MIX:3 * MIX + RW_CODES] = x_ref[0, :, off:off + RW_CODES]
    ext_s[HIST:HIST + L, 3 * MIX + RW_CODES:width] = jnp.zeros((L, width - 3 * MIX - RW_CODES), F32)
    cur = ext_s[HIST:HIST + L, :]
    prev = ext_s[HIST - 1:HIST - 1 + L, :]
    ext_s[0:HIST, :] = ext_s[L:L + HIST, :]
    z = cur + (prev - cur) * mu_ref[...]
    r = z[:, 0:MIX]
    k = z[:, MIX:2 * MIX]
    v = z[:, 2 * MIX:3 * MIX]
    zw = z[:, 3 * MIX + RWX_W:3 * MIX + RWX_W + RW_DECAY_RANK]
    za = z[:, 3 * MIX + RWX_A:3 * MIX + RWX_A + RW_A_RANK]
    zg = z[:, 3 * MIX + RWX_G:3 * MIX + RWX_G + RW_GATE_PAD]
    w_log = -jax.nn.softplus(-(w0_ref[...] + _dot(jnp.tanh(zw), w2_ref[...]))) - 0.5
    a = jax.nn.sigmoid(a0_ref[...] + _dot(za, a2_ref[...]))
    gate = _dot(jax.nn.sigmoid(zg), g2_ref[...])
    kkk = k * kk_ref[...]
    k2 = k * (1.0 + (a - 1.0) * ka_ref[...])
    log_decay = -jnp.exp(w_log)

    incl, strict = _tri_masks(L)
    cum = _dot_exact(incl.astype(F32), log_decay)
    e_cum = jnp.exp(cum)
    e_prev = jnp.exp(cum - log_decay)
    e_neg = jnp.exp(-cum)
    e_last = e_cum[L - 1:L, :]
    heads = range(RW_HEADS)
    sls = [slice(h * RW_HD, (h + 1) * RW_HD) for h in heads]
    kkn = [kkk[:, sl] for sl in sls]
    kkn = [x * lax.rsqrt(jnp.sum(x * x, axis=-1, keepdims=True) + NORM_EPS) for x in kkn]
    rs = [r[:, sl] for sl in sls]
    ks = [k2[:, sl] for sl in sls]
    vs = [v[:, sl] for sl in sls]
    k_til = [kkn[h] * e_prev[:, sls[h]] for h in heads]
    r_til = [rs[h] * e_cum[:, sls[h]] for h in heads]
    k_hat = [ks[h] * e_neg[:, sls[h]] for h in heads]
    b_hat = [kkn[h] * a[:, sls[h]] * e_neg[:, sls[h]] for h in heads]
    s_old = [s_s[h] for h in heads]
    if L % 8 == 0:
        kr = [jnp.concatenate([k_til[h], r_til[h]], axis=0) for h in heads]
        mask2 = jnp.concatenate([strict, incl], axis=0)
        p_k = [jnp.where(mask2, _dot_nt(kr[h], k_hat[h]), 0.0) for h in heads]
        p_b = [jnp.where(mask2, _dot_nt(kr[h], b_hat[h]), 0.0) for h in heads]
        a_b = [x[0:L] for x in p_b]
        b_b = [x[L:2 * L] for x in p_b]
        kv = [_dot(p_k[h], vs[h]) for h in heads]
        ksd = [_dot_nt(kr[h], s_old[h]) for h in heads]
        inner = [ksd[h][0:L] + kv[h][0:L] for h in heads]
        y_part = [ksd[h][L:2 * L] + kv[h][L:2 * L] for h in heads]
    else:
        a_k = [jnp.where(strict, _dot_nt(k_til[h], k_hat[h]), 0.0) for h in heads]
        a_b = [jnp.where(strict, _dot_nt(k_til[h], b_hat[h]), 0.0) for h in heads]
        b_k = [jnp.where(incl, _dot_nt(r_til[h], k_hat[h]), 0.0) for h in heads]
        b_b = [jnp.where(incl, _dot_nt(r_til[h], b_hat[h]), 0.0) for h in heads]
        inner = [_dot_nt(k_til[h], s_old[h]) + _dot(a_k[h], vs[h]) for h in heads]
        y_part = [_dot_nt(r_til[h], s_old[h]) + _dot(b_k[h], vs[h]) for h in heads]
    t_inv = _unit_lower_inverse(a_b, L)
    u = [_dot(t_inv[h], inner[h]) for h in heads]
    bu = [_dot(b_b[h], u[h]) for h in heads]
    el = [e_last[:, sl] for sl in sls]
    if L % 8 == 0:
        upd = [_dot_tn(jnp.concatenate([vs[h], -u[h]], axis=0),
                       jnp.concatenate([k_hat[h] * el[h], b_hat[h] * el[h]], axis=0)) for h in heads]
    else:
        upd = [_dot_tn(vs[h], k_hat[h] * el[h]) - _dot_tn(u[h], b_hat[h] * el[h]) for h in heads]
    for h in heads:
        s_s[h] = s_old[h] * el[h] + upd[h]
        y = y_part[h] - bu[h]
        mean = jnp.mean(y, axis=-1, keepdims=True)
        var = jnp.mean(jnp.square(y - mean), axis=-1, keepdims=True)
        y = (y - mean) * lax.rsqrt(var + RW_LN_EPS) * lnw_ref[:, sls[h]] + lnb_ref[:, sls[h]]
        y = y + jnp.sum(rs[h] * ks[h] * rk_ref[:, sls[h]], axis=-1, keepdims=True) * vs[h]
        out_ref[0, :, sls[h]] = (y * gate[:, sls[h]]).astype(out_ref.dtype)

    @pl.when(c == pl.num_programs(1) - 1)
    def _():
        sout_ref[0] = s_s[...]
        shout_ref[0] = ext_s[HIST + L - 1:HIST + L, 0:_RW_COLS]


def _rw_cols(vec):
    pad = jnp.zeros((RWX_BLOCK - (_RW_COLS - 3 * MIX),), F32)
    return jnp.concatenate([vec, pad]).reshape(1, 3 * MIX + RWX_BLOCK)


def _rwkv(proj3, blk0, n_seq, n_chunks, L, p, shift0, s0):
    width = 3 * MIX + RWX_BLOCK
    hist = jnp.concatenate([jnp.zeros((n_seq, HIST - 1, width), F32),
                            jnp.pad(shift0, ((0, 0), (0, width - _RW_COLS)))[:, None, :]], axis=1)
    g2 = jnp.pad(p['rw_g2'], ((0, RW_GATE_PAD - RW_GATE_RANK), (0, 0))).astype(BF16)
    row = lambda vec: vec.reshape(1, MIX)
    segs = [_seg(L, n_chunks, blk0, C_RW + j * MIX) for j in range(3)] + [_seg(L, n_chunks, blk0, C_RWX, RW_CODES)]
    off = segs[0][1]
    assert all(s[1] == off for s in segs)

    full2 = lambda shape: pl.BlockSpec(shape, lambda b, c: (0, 0))
    s0, s0_spec = _state_in(s0, (RW_HEADS, RW_HD, RW_HD))
    out, s_new, shift_new = pl.pallas_call(
        functools.partial(_rwkv_body, L=L, off=off),
        grid=(n_seq, n_chunks),
        in_specs=[segs[0][0], segs[1][0], segs[2][0], segs[3][0],
                  pl.BlockSpec((1, HIST, width), lambda b, c: (b, 0, 0)),
                  full2((1, width)),
                  full2((1, MIX)), full2((RW_DECAY_RANK, MIX)),
                  full2((1, MIX)), full2((RW_A_RANK, MIX)),
                  full2((RW_GATE_PAD, MIX)),
                  full2((1, MIX)), full2((1, MIX)), full2((1, MIX)), full2((1, MIX)), full2((1, MIX)),
                  s0_spec],
        out_specs=[pl.BlockSpec((1, L, MIX), lambda b, c: (b * n_chunks + c, 0, 0)),
                   pl.BlockSpec((1, RW_HEADS, RW_HD, RW_HD), lambda b, c: (b, 0, 0, 0)),
                   pl.BlockSpec((1, 1, _RW_COLS), lambda b, c: (b, 0, 0))],
        out_shape=[jax.ShapeDtypeStruct((n_seq * n_chunks, L, MIX), BF16),
                   jax.ShapeDtypeStruct((n_seq, RW_HEADS, RW_HD, RW_HD), F32),
                   jax.ShapeDtypeStruct((n_seq, 1, _RW_COLS), F32)],
        scratch_shapes=[pltpu.VMEM((RW_HEADS, RW_HD, RW_HD), F32),
                        pltpu.VMEM((HIST + max(L, HIST), width), F32)],
        compiler_params=_cparams(("parallel", "arbitrary")),
        name="rwkv7",
    )(proj3, proj3, proj3, proj3, hist, _rw_cols(p['rw_mu']),
      row(p['rw_w0']), p['rw_w2'].astype(BF16), row(p['rw_a0']), p['rw_a2'].astype(BF16), g2,
      row(p['rw_k_k']), row(p['rw_k_a']), row(p['rw_r_k'].reshape(-1)), row(p['rw_ln_w']), row(p['rw_ln_b']), s0)
    return out, s_new, shift_new.reshape(n_seq, _RW_COLS)


def _shift_rows(x, d, fill):
    rows = lax.broadcasted_iota(jnp.int32, x.shape, 0)
    return jnp.where(rows >= d, pltpu.roll(x, d, axis=0), fill)


def _lru_body(x_ref, g_ref, hist_ref, cw_ref, cb_ref, wa_ref, ba_ref, wx_ref, bx_ref, lam_ref, h0_ref,
              out_ref, hout_ref, cout_ref, h_s, ext_s, *, L, off):
    c = pl.program_id(1)

    @pl.when(c == 0)
    def _():
        h_s[...] = h0_ref[0]
        ext_s[0:HIST, :] = hist_ref[0]

    ext_s[HIST:HIST + L, :] = x_ref[0, :, off:off + MIX]
    base = HIST - (CONV_W - 1)
    xc = ext_s[base:base + L, :] * cw_ref[0:1, :]
    for j in range(1, CONV_W):
        xc = xc + ext_s[base + j:base + j + L, :] * cw_ref[j:j + 1, :]
    ext_s[0:HIST, :] = ext_s[L:L + HIST, :]
    xc = xc + cb_ref[...]
    r = jax.nn.sigmoid(_dot(xc, wa_ref[...]) + ba_ref[...])
    i = jax.nn.sigmoid(_dot(xc, wx_ref[...]) + bx_ref[...])
    log_a = -LRU_C * r * jax.nn.softplus(-lam_ref[...])
    a = jnp.exp(log_a)
    u = jnp.sqrt(-jnp.tanh(log_a) * (a * a + 1.0)) * (i * xc)
    h_prev = h_s[...]
    if L % 8 == 0:
        acc_a, acc_h = a, u
        d = 1
        while d < L:
            acc_h = acc_a * _shift_rows(acc_h, d, 0.0) + acc_h
            acc_a = acc_a * _shift_rows(acc_a, d, 1.0)
            d *= 2
        hs = acc_a * h_prev + acc_h
        h_s[...] = hs[L - 1:L, :]
    else:
        rows = []
        for t in range(L):
            h_prev = a[t:t + 1, :] * h_prev + u[t:t + 1, :]
            rows.append(h_prev)
        hs = jnp.concatenate(rows, axis=0)
        h_s[...] = h_prev
    out_ref[0] = (hs * jax.nn.gelu(g_ref[0, :, off:off + MIX])).astype(out_ref.dtype)

    @pl.when(c == pl.num_programs(1) - 1)
    def _():
        hout_ref[0] = h_s[...]
        cout_ref[0] = ext_s[HIST + L - (CONV_W - 1):HIST + L, :]


def _block_diag(w):
    nb, bs, _ = w.shape
    eye = jnp.eye(nb, dtype=w.dtype)
    return (eye[:, None, :, None] * w[:, :, None, :]).reshape(nb * bs, nb * bs)


def _lru(proj3, blk0, n_seq, n_chunks, L, p, conv0, h0):
    hist = _hist_rows(conv0, n_seq, MIX)
    row = lambda vec: vec.reshape(1, MIX)
    full2 = lambda shape: pl.BlockSpec(shape, lambda b, c: (0, 0))
    segs = [_seg(L, n_chunks, blk0, C_LRU + j * MIX) for j in range(2)]
    off = segs[0][1]
    assert all(s[1] == off for s in segs)
    out, h_new, conv_new = pl.pallas_call(
        functools.partial(_lru_body, L=L, off=off),
        grid=(n_seq, n_chunks),
        in_specs=[segs[0][0], segs[1][0],
                  pl.BlockSpec((1, HIST, MIX), lambda b, c: (b, 0, 0)),
                  full2((CONV_W, MIX)), full2((1, MIX)),
                  full2((MIX, MIX)), full2((1, MIX)), full2((MIX, MIX)), full2((1, MIX)), full2((1, MIX)),
                  pl.BlockSpec((1, 1, MIX), lambda b, c: (b, 0, 0))],
        out_specs=[pl.BlockSpec((1, L, MIX), lambda b, c: (b * n_chunks + c, 0, 0)),
                   pl.BlockSpec((1, 1, MIX), lambda b, c: (b, 0, 0)),
                   pl.BlockSpec((1, CONV_W - 1, MIX), lambda b, c: (b, 0, 0))],
        out_shape=[jax.ShapeDtypeStruct((n_seq * n_chunks, L, MIX), BF16),
                   jax.ShapeDtypeStruct((n_seq, 1, MIX), F32),
                   jax.ShapeDtypeStruct((n_seq, CONV_W - 1, MIX), F32)],
        scratch_shapes=[pltpu.VMEM((1, MIX), F32),
                        pltpu.VMEM((HIST + max(L, HIST), MIX), F32)],
        compiler_params=_cparams(("parallel", "arbitrary")),
        name="rglru",
    )(proj3, proj3, hist, p['lru_conv_w'], row(p['lru_conv_b']),
      _block_diag(p['lru_wa']).astype(BF16), row(p['lru_ba']),
      _block_diag(p['lru_wx']).astype(BF16), row(p['lru_bx']), row(p['lru_lambda']),
      h0.reshape(n_seq, 1, MIX))
    return out, h_new.reshape(n_seq, MIX), conv_new


L_MLSTM = 128
L_DELTA = 64
L_RWKV = 64
L_LRU = 256


def _row_tile(m, cap):
    return max(t for t in range(16, cap + 1, 16) if m % t == 0)


def _chunk_len(t, want):
    return math.gcd(t, want)


def _small_views(small, L):
    m = small.shape[0]
    small_c = small.reshape(m // L, L, 32)
    small_r = jnp.swapaxes(small_c, 1, 2)
    return small_c, small_r


def _mixers(proj, n_seq, t_len, p, st):
    m = proj.shape[0]
    ml_c, ml_n, ml_m, dn_s, dn_conv, lru_h, lru_conv, rw_s, rw_shift = st
    small = jnp.concatenate([proj[:, C_ML + 4 * MIX:C_ML + 4 * MIX + 2 * N_HEADS],
                             proj[:, C_DN + 4 * MIX:C_DN + 4 * MIX + 2 * N_HEADS]], axis=1)

    def view(want):
        L = _chunk_len(t_len, want)
        assert m % L == 0
        return proj.reshape(m // L, L, N_PROJ), L, t_len // L

    p3, L, nch = view(L_MLSTM)
    sc, sr = _small_views(small, L)
    out_ml, ml_c, ml_n, ml_m = _mlstm(p3, sc, sr, 0, n_seq, nch, L, p['ml_b_i'], p['ml_b_f'], p['ml_norm'],
                                      ml_c, ml_n, ml_m)
    p3, L, nch = view(L_DELTA)
    sc, sr = _small_views(small, L)
    out_dn, dn_s, dn_conv = _delta(p3, sc, sr, 0, n_seq, nch, L, p['dn_conv'], p['dn_A_log'], p['dn_dt_bias'],
                                   p['dn_norm'], dn_conv, dn_s)
    p3, L, nch = view(L_LRU)
    out_lru, lru_h, lru_conv = _lru(p3, 0, n_seq, nch, L, p, lru_conv, lru_h)
    p3, L, nch = view(L_RWKV)
    out_rw, rw_s, rw_shift = _rwkv(p3, 0, n_seq, nch, L, p, rw_shift, rw_s)
    outs = tuple(o.reshape(n_seq * t_len, MIX) for o in (out_ml, out_dn, out_lru, out_rw))
    return outs, (ml_c, ml_n, ml_m, dn_s, dn_conv, lru_h, lru_conv, rw_s, rw_shift)


_STATE_SHAPES = ((N_HEADS, HEAD_DIM, HEAD_DIM), (N_HEADS, HEAD_DIM), (N_HEADS,), (N_HEADS, HEAD_DIM, HEAD_DIM),
                 (CONV_W - 1, 3 * MIX), (MIX,), (CONV_W - 1, MIX), (RW_HEADS, RW_HD, RW_HD), (_RW_COLS,))


def kernel(x_prompt, x_sample, state_mlstm_C, state_mlstm_n, state_mlstm_m, state_delta_S, state_delta_conv, state_rglru_h, state_rglru_conv, state_rwkv_S, state_rwkv_shift, norm_mix, w_in, ml_b_i, ml_b_f, ml_norm, dn_conv, dn_A_log, dn_dt_bias, dn_norm, lru_conv_w, lru_conv_b, lru_wa, lru_ba, lru_wx, lru_bx, lru_lambda, rw_mu, rw_w0, rw_w2, rw_a0, rw_a2, rw_g2, rw_k_k, rw_k_a, rw_r_k, rw_ln_w, rw_ln_b, w_branch, w_out, norm_ffn, w_ffn_gate, w_ffn_up, w_ffn_down, norm_final):
    params = {
        'ml_b_i': ml_b_i, 'ml_b_f': ml_b_f, 'ml_norm': ml_norm,
        'dn_conv': dn_conv, 'dn_A_log': dn_A_log, 'dn_dt_bias': dn_dt_bias, 'dn_norm': dn_norm,
        'lru_conv_w': lru_conv_w, 'lru_conv_b': lru_conv_b, 'lru_wa': lru_wa, 'lru_ba': lru_ba,
        'lru_wx': lru_wx, 'lru_bx': lru_bx, 'lru_lambda': lru_lambda,
        'rw_mu': rw_mu, 'rw_w0': rw_w0, 'rw_w2': rw_w2, 'rw_a0': rw_a0, 'rw_a2': rw_a2, 'rw_g2': rw_g2,
        'rw_k_k': rw_k_k, 'rw_k_a': rw_k_a, 'rw_r_k': rw_r_k, 'rw_ln_w': rw_ln_w, 'rw_ln_b': rw_ln_b,
    }
    depth = w_in.shape[0]
    bp, tp, d = x_prompt.shape
    bs, ts, _ = x_sample.shape
    n_p, n_s = bp * tp, bs * ts
    m = n_p + n_s
    x = jnp.concatenate([x_prompt.reshape(n_p, d), x_sample.reshape(n_s, d)], axis=0)
    tm = _row_tile(m, 1088)
    tm_panel = _row_tile(m, 2176)
    tm_norm = _row_tile(m, 544)
    sample_states = (state_mlstm_C, state_mlstm_n, state_mlstm_m, state_delta_S, state_delta_conv,
                     state_rglru_h, state_rglru_conv, state_rwkv_S, state_rwkv_shift)
    new_p = [[] for _ in _STATE_SHAPES]
    new_s = [[] for _ in _STATE_SHAPES]
    wd_b = jnp.pad(w_ffn_down.astype(BF16), ((0, 0), (0, D_FF_PAD - D_FF), (0, 0)))
    for l in range(depth):
        p = {name: w[l] for name, w in params.items()}
        h = _rmsnorm(x, norm_mix[l], BF16, tm_norm)
        proj = _matmul(h, w_in, l, tm_panel, 256)
        st_p = tuple(jnp.zeros((bp,) + shp, F32) for shp in _STATE_SHAPES)
        st_s = tuple((s, l) if s.ndim == 5 else s[l] for s in sample_states)
        outs_p, st_p = _mixers(proj, bp, tp, p, st_p)
        outs_s, st_s = _mixers(proj[n_p:], bs, ts, p, st_s)
        for lst, s in zip(new_p, st_p):
            lst.append(s)
        for lst, s in zip(new_s, st_s):
            lst.append(s)
        branches = jnp.stack([jnp.concatenate([a, b], axis=0) for a, b in zip(outs_p, outs_s)])
        mix = _merge(proj, branches, w_branch, l, tm, 1024)
        x = _matmul_residual(mix, w_out, l, x, tm, 512, 2048)
        h2 = _rmsnorm(x, norm_ffn[l], BF16, tm_norm)
        act = _ffn_up(h2, w_ffn_gate, w_ffn_up, l, tm_panel, 256, D_FF_PAD)
        x = _matmul_residual(act, wd_b, l, x, tm, 1024, D_FF_PAD // 4)
    y = _rmsnorm(x, norm_final, F32, tm_norm)
    y_prompt = y[:n_p].reshape(bp, tp, d)
    y_sample = y[n_p:].reshape(bs, ts, d)
    return ((y_prompt, y_sample) + tuple(jnp.stack(lst) for lst in new_p)
            + tuple(jnp.stack(lst) for lst in new_s))
```

```python
import functools
import math

import jax
import jax.numpy as jnp
from jax import lax
from jax.experimental import pallas as pl
from jax.experimental.pallas import tpu as pltpu

F32 = jnp.float32
BF16 = jnp.bfloat16

D_MODEL = 4096
MIX = D_MODEL // 4
HEAD_DIM = 128
N_HEADS = MIX // HEAD_DIM
RW_HD = 64
RW_HEADS = MIX // RW_HD
RW_DECAY_RANK = 128
RW_A_RANK = 128
RW_GATE_RANK = 480
CONV_W = 4
LRU_C = 8.0
D_FF = -(-(8 * D_MODEL) // (3 * 256)) * 256
NORM_EPS = 1e-6
RW_LN_EPS = 64e-5

_ML_COLS = 4 * MIX + 2 * N_HEADS
_DN_COLS = 4 * MIX + 2 * N_HEADS
_LRU_COLS = 2 * MIX
_RW_COLS = 3 * MIX + RW_DECAY_RANK + RW_A_RANK + RW_GATE_RANK
C_ML = 0
C_DN = _ML_COLS
C_LRU = C_DN + _DN_COLS
C_RW = C_LRU + _LRU_COLS
C_RWX = C_RW + 3 * MIX
C_GATE = C_RW + _RW_COLS
N_PROJ = C_GATE + 4 * D_MODEL
RW_CODES = _RW_COLS - 3 * MIX
RWX_W = 0
RWX_A = RW_DECAY_RANK
RWX_G = RW_DECAY_RANK + RW_A_RANK
D_FF_PAD = -(-D_FF // 1024) * 1024

VMEM_LIMIT_BYTES = 56 * 1024 * 1024
LANE = 128


def _cparams(sem):
    return pltpu.CompilerParams(dimension_semantics=sem, vmem_limit_bytes=VMEM_LIMIT_BYTES)


def _rmsnorm_body(x_ref, g_ref, o_ref):
    x = x_ref[...]
    y = x * lax.rsqrt(jnp.mean(x * x, axis=-1, keepdims=True) + NORM_EPS)
    o_ref[...] = (y * g_ref[...]).astype(o_ref.dtype)


def _rmsnorm(x, g, out_dtype, tm):
    m, d = x.shape
    return pl.pallas_call(
        _rmsnorm_body,
        grid=(m // tm,),
        in_specs=[pl.BlockSpec((tm, d), lambda i: (i, 0)), pl.BlockSpec((1, d), lambda i: (0, 0))],
        out_specs=pl.BlockSpec((tm, d), lambda i: (i, 0)),
        out_shape=jax.ShapeDtypeStruct((m, d), out_dtype),
        compiler_params=_cparams(("parallel",)),
        name="rmsnorm",
    )(x, g.reshape(1, d))


def _mm_body(x_ref, w_ref, o_ref):
    o_ref[...] = jnp.dot(x_ref[...], w_ref[0].astype(BF16), preferred_element_type=F32).astype(o_ref.dtype)


def _matmul(x, w, layer, tm, tn, out_dtype=F32):
    m, k = x.shape
    n = w.shape[2]
    return pl.pallas_call(
        _mm_body,
        grid=(m // tm, pl.cdiv(n, tn)),
        in_specs=[pl.BlockSpec((tm, k), lambda i, j: (i, 0), pipeline_mode=pl.Buffered(1)),
                  pl.BlockSpec((1, k, tn), lambda i, j: (layer, 0, j))],
        out_specs=pl.BlockSpec((tm, tn), lambda i, j: (i, j)),
        out_shape=jax.ShapeDtypeStruct((m, n), out_dtype),
        compiler_params=_cparams(("parallel", "arbitrary")),
        name="matmul",
    )(x, w)


def _mm_res_body(x_ref, w_ref, r_ref, o_ref, acc_ref):
    kk = pl.program_id(2)

    @pl.when(kk == 0)
    def _():
        acc_ref[...] = r_ref[...]

    acc_ref[...] += jnp.dot(x_ref[...], w_ref[0], preferred_element_type=F32)

    @pl.when(kk == pl.num_programs(2) - 1)
    def _():
        o_ref[...] = acc_ref[...]


def _matmul_residual(x, w, layer, res, tm, tn, tk):
    m, k = x.shape
    n = w.shape[2]
    return pl.pallas_call(
        _mm_res_body,
        grid=(n // tn, m // tm, k // tk),
        in_specs=[pl.BlockSpec((tm, tk), lambda j, i, kk: (i, kk)),
                  pl.BlockSpec((1, tk, tn), lambda j, i, kk: (layer, kk, j)),
                  pl.BlockSpec((tm, tn), lambda j, i, kk: (i, j))],
        out_specs=pl.BlockSpec((tm, tn), lambda j, i, kk: (i, j)),
        out_shape=jax.ShapeDtypeStruct((m, n), F32),
        scratch_shapes=[pltpu.VMEM((tm, tn), F32)],
        compiler_params=_cparams(("parallel", "parallel", "arbitrary")),
        name="matmul_residual",
    )(x, w, res)


def _ffn_up_body(x_ref, wg_ref, wu_ref, o_ref, *, n_valid):
    x = x_ref[...]
    g = jnp.dot(x, wg_ref[0].astype(BF16), preferred_element_type=F32)
    u = jnp.dot(x, wu_ref[0].astype(BF16), preferred_element_type=F32)
    act = jax.nn.silu(g) * u
    o_ref[...] = jnp.where(pl.program_id(1) < n_valid, act, 0.0).astype(o_ref.dtype)


def _ffn_up(x, wg, wu, layer, tm, tn, n_out):
    m, k = x.shape
    n = wg.shape[2]
    assert n % tn == 0 and n_out % tn == 0
    n_valid = n // tn
    wspec = pl.BlockSpec((1, k, tn), lambda i, j: (layer, 0, jnp.minimum(j, n_valid - 1)))
    return pl.pallas_call(
        functools.partial(_ffn_up_body, n_valid=n_valid),
        grid=(m // tm, n_out // tn),
        in_specs=[pl.BlockSpec((tm, k), lambda i, j: (i, 0), pipeline_mode=pl.Buffered(1)), wspec, wspec],
        out_specs=pl.BlockSpec((tm, tn), lambda i, j: (i, j)),
        out_shape=jax.ShapeDtypeStruct((m, n_out), BF16),
        compiler_params=_cparams(("parallel", "arbitrary")),
        name="ffn_up",
    )(x, wg, wu)


def _merge_body(g_ref, b_ref, w_ref, o_ref, acc_ref):
    n = pl.program_id(2)
    term = jax.nn.sigmoid(g_ref[...]) * jnp.dot(b_ref[0], w_ref[0, 0].astype(BF16), preferred_element_type=F32)

    @pl.when(n == 0)
    def _():
        acc_ref[...] = term

    @pl.when(n > 0)
    def _():
        acc_ref[...] += term

    @pl.when(n == pl.num_programs(2) - 1)
    def _():
        o_ref[...] = acc_ref[...].astype(o_ref.dtype)


def _merge(proj, branches, wb, layer, tm, tn):
    nb, m, k = branches.shape
    n = wb.shape[3]
    return pl.pallas_call(
        _merge_body,
        grid=(n // tn, m // tm, nb),
        in_specs=[pl.BlockSpec((pl.Element(tm), pl.Element(tn)),
                               lambda j, i, b: (i * tm, (C_GATE // LANE + b * (n // LANE) + j * (tn // LANE)) * LANE)),
                  pl.BlockSpec((1, tm, k), lambda j, i, b: (b, i, 0)),
                  pl.BlockSpec((1, 1, k, tn), lambda j, i, b: (layer, b, 0, j))],
        out_specs=pl.BlockSpec((tm, tn), lambda j, i, b: (i, j)),
        out_shape=jax.ShapeDtypeStruct((m, n), BF16),
        scratch_shapes=[pltpu.VMEM((tm, tn), F32)],
        compiler_params=_cparams(("parallel", "parallel", "arbitrary")),
        name="merge",
    )(proj, branches, wb)


G_MLSTM = 2
G_DELTA = 4
G_RWKV = 4


def _group_size(n_seq, n_chunks, want):
    return math.gcd(n_seq, want) if n_chunks == 1 else 1


def _seg(L, n_chunks, G, col, width=MIX):
    start = col // LANE * LANE
    off = col - start
    win = -(-(off + width) // LANE) * LANE
    spec = pl.BlockSpec((pl.Element(G), pl.Element(L), pl.Element(win)),
                        lambda b, c: ((b * n_chunks + c) * G, 0, start))
    return spec, off


def _per_seq(G, tail):
    zeros = (0,) * len(tail)
    return pl.BlockSpec((G,) + tail, lambda b, c: (b,) + zeros)


def _per_chunk(G, n_chunks, tail):
    zeros = (0,) * len(tail)
    return pl.BlockSpec((G,) + tail, lambda b, c: (b * n_chunks + c,) + zeros)


def _state_in(state, tail, G):
    zeros = (0,) * len(tail)
    if isinstance(state, tuple):
        arr, layer = state
        return arr, pl.BlockSpec((None, G) + tail, lambda b, c: (layer, b) + zeros)
    return state, _per_seq(G, tail)


def _state_out(tail, n_seq, slab, G):
    zeros = (0,) * len(tail)
    if slab is None:
        return jax.ShapeDtypeStruct((n_seq,) + tail, F32), _per_seq(G, tail)
    layer, depth = slab
    return (jax.ShapeDtypeStruct((depth, n_seq) + tail, F32),
            pl.BlockSpec((None, G) + tail, lambda b, c: (layer, b) + zeros))


def _alias_prev(body, n_inputs, out_index, prev):
    if prev is None:
        return body, [], [], {}

    def aliased_body(*refs):
        return body(*refs[:n_inputs], *refs[n_inputs + 1:])

    return aliased_body, [prev], [pl.BlockSpec(memory_space=pl.ANY)], {n_inputs: out_index}


def _tri_masks(L):
    row = lax.broadcasted_iota(jnp.int32, (L, L), 0)
    col = lax.broadcasted_iota(jnp.int32, (L, L), 1)
    return col <= row, col < row


def _dot(a, b):
    return jnp.dot(a.astype(BF16), b.astype(BF16), preferred_element_type=F32)


def _dot_nt(a, b):
    return lax.dot_general(a.astype(BF16), b.astype(BF16), (((1,), (1,)), ((), ())), preferred_element_type=F32)


def _dot_tn(a, b):
    return lax.dot_general(a.astype(BF16), b.astype(BF16), (((0,), (0,)), ((), ())), preferred_element_type=F32)


def _dot_exact(a, b):
    return jnp.dot(a, b, precision=lax.Precision.HIGHEST, preferred_element_type=F32)


def _unit_lower_inverse(ns, L):
    eye = (lax.broadcasted_iota(jnp.int32, (L, L), 0) == lax.broadcasted_iota(jnp.int32, (L, L), 1)).astype(F32)
    ms = [-n for n in ns]
    ts = [eye + m for m in ms]
    span = 2
    while span < L:
        ms = [_dot(m, m) for m in ms]
        ts = [t + _dot(t, m) for t, m in zip(ts, ms)]
        span *= 2
    return ts


def _mlstm_body(q_ref, k_ref, v_ref, o_ref, gc_ref, gr_ref, bc_ref, br_ref, norm_ref, c0_ref, n0_ref, m0_ref,
                out_ref, cout_ref, nout_ref, mout_ref, c_s, n_s, m_s, *, L, G):
    c = pl.program_id(1)

    @pl.when(c == 0)
    def _():
        c_s[...] = c0_ref[...]
        n_s[...] = n0_ref[...]
        m_s[...] = m0_ref[...]

    incl, _ = _tri_masks(L)
    tril = incl.astype(F32)
    ig_c, ig_r, b_c, b_r = [], [], [], []
    for g in range(G):
        gc = gc_ref[g]
        gr = gr_ref[g]
        ig_c.append(gc[:, 0:N_HEADS] + bc_ref[:, 0:N_HEADS])
        ig_r.append(gr[0:N_HEADS, :] + br_ref[0:N_HEADS, :])
        lf_c = jax.nn.log_sigmoid(gc[:, N_HEADS:2 * N_HEADS] + bc_ref[:, N_HEADS:2 * N_HEADS])
        lf_r = jax.nn.log_sigmoid(gr[N_HEADS:2 * N_HEADS, :] + br_ref[N_HEADS:2 * N_HEADS, :])
        b_c.append(_dot_exact(tril, lf_c))
        b_r.append(_dot_exact(lf_r, tril.T))
    units = [(g, h) for g in range(G) for h in range(N_HEADS)]
    sl = lambda h: slice(h * HEAD_DIM, (h + 1) * HEAD_DIM)
    qs = [q_ref[g, :, sl(h)] * (HEAD_DIM ** -0.5) for g, h in units]
    ks = [k_ref[g, :, sl(h)] for g, h in units]
    vs = [v_ref[g, :, sl(h)] for g, h in units]
    c_old = [c_s[g, h] for g, h in units]
    n_old = [n_s[g, h:h + 1, :] for g, h in units]
    qk = [_dot_nt(q, k) for q, k in zip(qs, ks)]
    qc = [_dot(q, co) for q, co in zip(qs, c_old)]
    bcs = [b_c[g][:, h:h + 1] for g, h in units]
    m_prev = [m_s[g][:, h:h + 1] for g, h in units]
    log_d = [jnp.where(incl, bcs[u] - b_r[g][h:h + 1, :] + ig_r[g][h:h + 1, :], -jnp.inf)
             for u, (g, h) in enumerate(units)]
    state_w = [b + m for b, m in zip(bcs, m_prev)]
    m_t = [jnp.maximum(s, jnp.max(ld, axis=-1, keepdims=True)) for s, ld in zip(state_w, log_d)]
    scores = [a * jnp.exp(ld - mt) for a, ld, mt in zip(qk, log_d, m_t)]
    sw = [jnp.exp(s - mt) for s, mt in zip(state_w, m_t)]
    sv = [_dot(s, v) for s, v in zip(scores, vs)]
    m_last = [mt[L - 1:L, :] for mt in m_t]
    b_last = [b[L - 1:L, :] for b in bcs]
    kw = [ks[u] * jnp.exp(b_last[u] - bcs[u] + ig_c[g][:, h:h + 1] - m_last[u]) for u, (g, h) in enumerate(units)]
    kv = [_dot_tn(a, v) for a, v in zip(kw, vs)]
    for u, (g, h) in enumerate(units):
        decay = jnp.exp(b_last[u] + m_prev[u] - m_last[u])
        c_s[g, h] = decay * c_old[u] + kv[u]
        n_s[g, h:h + 1, :] = decay * n_old[u] + jnp.sum(kw[u], axis=0, keepdims=True)
        num = sv[u] + sw[u] * qc[u]
        den = jnp.sum(scores[u], axis=-1, keepdims=True) + sw[u] * jnp.sum(qs[u] * n_old[u], axis=-1, keepdims=True)
        hid = num / jnp.maximum(jnp.abs(den), jnp.exp(-m_t[u]))
        hn = hid * lax.rsqrt(jnp.mean(hid * hid, axis=-1, keepdims=True) + NORM_EPS) * norm_ref[:, sl(h)]
        out_ref[g, :, sl(h)] = (jax.nn.sigmoid(o_ref[g, :, sl(h)]) * hn).astype(out_ref.dtype)
    for g in range(G):
        m_s[g] = jnp.concatenate(m_last[g * N_HEADS:(g + 1) * N_HEADS], axis=1)

    @pl.when(c == pl.num_programs(1) - 1)
    def _():
        cout_ref[...] = c_s[...]
        nout_ref[...] = n_s[...]
        mout_ref[...] = m_s[...]


def _mlstm(proj3, small_c, small_r, blk0, n_seq, n_chunks, L, b_i, b_f, ml_norm, c0, n0, m0, slab=None, prev=None):
    assert C_ML % LANE == 0 and blk0 == 0
    G = _group_size(n_seq, n_chunks, G_MLSTM)
    bias_c = jnp.zeros((1, LANE), F32).at[0, 0:N_HEADS].set(b_i).at[0, N_HEADS:2 * N_HEADS].set(b_f)
    bias_r = jnp.zeros((32, 1), F32).at[0:N_HEADS, 0].set(b_i).at[N_HEADS:2 * N_HEADS, 0].set(b_f)

    def col(j):
        return _seg(L, n_chunks, G, C_ML + j * MIX)[0]

    full2 = lambda shape: pl.BlockSpec(shape, lambda b, c: (0, 0))
    c0, c0_spec = _state_in(c0, (N_HEADS, HEAD_DIM, HEAD_DIM), G)
    c_shape, c_spec = _state_out((N_HEADS, HEAD_DIM, HEAD_DIM), n_seq, slab, G)
    inputs = [proj3, proj3, proj3, proj3, small_c, small_r, bias_c, bias_r, ml_norm.reshape(1, MIX),
              c0, n0, m0.reshape(n_seq, 1, N_HEADS)]
    body, extra_in, extra_specs, aliases = _alias_prev(functools.partial(_mlstm_body, L=L, G=G), len(inputs), 1, prev)
    outs = pl.pallas_call(
        body,
        grid=(n_seq // G, n_chunks),
        in_specs=[col(0), col(1), col(2), col(3),
                  _per_chunk(G, n_chunks, (L, 32)), _per_chunk(G, n_chunks, (32, L)),
                  full2((1, LANE)), full2((32, 1)), full2((1, MIX)),
                  c0_spec, _per_seq(G, (N_HEADS, HEAD_DIM)), _per_seq(G, (1, N_HEADS))] + extra_specs,
        out_specs=[_per_chunk(G, n_chunks, (L, MIX)),
                   c_spec, _per_seq(G, (N_HEADS, HEAD_DIM)), _per_seq(G, (1, N_HEADS))],
        out_shape=[jax.ShapeDtypeStruct((n_seq * n_chunks, L, MIX), BF16),
                   c_shape,
                   jax.ShapeDtypeStruct((n_seq, N_HEADS, HEAD_DIM), F32),
                   jax.ShapeDtypeStruct((n_seq, 1, N_HEADS), F32)],
        scratch_shapes=[pltpu.VMEM((G, N_HEADS, HEAD_DIM, HEAD_DIM), F32),
                        pltpu.VMEM((G, N_HEADS, HEAD_DIM), F32),
                        pltpu.VMEM((G, 1, N_HEADS), F32)],
        input_output_aliases=aliases,
        compiler_params=_cparams(("parallel", "arbitrary")),
        name="mlstm",
    )(*inputs, *extra_in)
    out, c_new, n_new, m_new = outs
    return out, c_new, n_new, m_new.reshape(n_seq, N_HEADS)


HIST = 8


def _delta_body(q_ref, k_ref, v_ref, z_ref, gc_ref, gr_ref, bc_ref, br_ref, cw_ref, hist_ref, norm_ref, s0_ref,
                out_ref, sout_ref, cout_ref, s_s, ext_s, *, L, G, off):
    c = pl.program_id(1)

    @pl.when(c == 0)
    def _():
        s_s[...] = s0_ref[...]
        ext_s[:, 0:HIST, :] = hist_ref[...]

    incl, strict = _tri_masks(L)
    tril = incl.astype(F32)
    a_log_c, dt_c = bc_ref[:, 0:N_HEADS], bc_ref[:, N_HEADS:2 * N_HEADS]
    a_log_r, dt_r = br_ref[0:N_HEADS, :], br_ref[N_HEADS:2 * N_HEADS, :]
    base = HIST - (CONV_W - 1)
    qkv, z, beta_c, gcum_c, gcum_r = [], [], [], [], []
    for g in range(G):
        ext_s[g, HIST:HIST + L, 0:MIX] = q_ref[g, :, off:off + MIX]
        ext_s[g, HIST:HIST + L, MIX:2 * MIX] = k_ref[g, :, off:off + MIX]
        ext_s[g, HIST:HIST + L, 2 * MIX:3 * MIX] = v_ref[g, :, off:off + MIX]
        z.append(z_ref[g, :, off:off + MIX])
        y = ext_s[g, base:base + L, :] * cw_ref[0:1, :]
        for j in range(1, CONV_W):
            y = y + ext_s[g, base + j:base + j + L, :] * cw_ref[j:j + 1, :]
        ext_s[g, 0:HIST, :] = ext_s[g, L:L + HIST, :]
        qkv.append(jax.nn.silu(y))
        gc = gc_ref[g]
        gr = gr_ref[g]
        g_c = -jnp.exp(a_log_c) * jax.nn.softplus(gc[:, 2 * N_HEADS:3 * N_HEADS] + dt_c)
        g_r = -jnp.exp(a_log_r) * jax.nn.softplus(gr[2 * N_HEADS:3 * N_HEADS, :] + dt_r)
        beta_c.append(jax.nn.sigmoid(gc[:, 3 * N_HEADS:4 * N_HEADS]))
        gcum_c.append(_dot_exact(tril, g_c))
        gcum_r.append(_dot_exact(g_r, tril.T))
    units = [(g, h) for g in range(G) for h in range(N_HEADS)]
    nu = range(len(units))
    sl = lambda h: slice(h * HEAD_DIM, (h + 1) * HEAD_DIM)
    qs = [qkv[g][:, h * HEAD_DIM:(h + 1) * HEAD_DIM] for g, h in units]
    ks = [qkv[g][:, MIX + h * HEAD_DIM:MIX + (h + 1) * HEAD_DIM] for g, h in units]
    vs = [qkv[g][:, 2 * MIX + h * HEAD_DIM:2 * MIX + (h + 1) * HEAD_DIM] for g, h in units]
    qs = [x * lax.rsqrt(jnp.sum(x * x, axis=-1, keepdims=True) + NORM_EPS) * (HEAD_DIM ** -0.5) for x in qs]
    ks = [x * lax.rsqrt(jnp.sum(x * x, axis=-1, keepdims=True) + NORM_EPS) for x in ks]
    gch = [gcum_c[g][:, h:h + 1] for g, h in units]
    beta = [beta_c[g][:, h:h + 1] for g, h in units]
    eg = [jnp.exp(x) for x in gch]
    dec_incl = [jnp.exp(jnp.where(incl, gch[u] - gcum_r[g][h:h + 1, :], -jnp.inf)) for u, (g, h) in enumerate(units)]
    s_old = [s_s[g, h] for g, h in units]
    if L % 8 == 0:
        prod = [_dot_nt(jnp.concatenate([ks[u], qs[u]], axis=0), ks[u]) for u in nu]
        kk = [x[0:L] for x in prod]
        qk = [x[L:2 * L] for x in prod]
    else:
        kk = [_dot_nt(ks[u], ks[u]) for u in nu]
        qk = [_dot_nt(qs[u], ks[u]) for u in nu]
    a_mat = [jnp.where(strict, beta[u] * kk[u] * dec_incl[u], 0.0) for u in nu]
    t_inv = _unit_lower_inverse(a_mat, L)
    uw = [_dot(t_inv[u], jnp.concatenate([vs[u] * beta[u], ks[u] * (beta[u] * eg[u])], axis=1)) for u in nu]
    uu = [x[:, 0:HEAD_DIM] for x in uw]
    w = [x[:, HEAD_DIM:2 * HEAD_DIM] for x in uw]
    if L % 8 == 0:
        ws = [_dot(jnp.concatenate([w[u], qs[u] * eg[u]], axis=0), s_old[u]) for u in nu]
        w_s = [x[0:L] for x in ws]
        q_s = [x[L:2 * L] for x in ws]
    else:
        w_s = [_dot(w[u], s_old[u]) for u in nu]
        q_s = [_dot(qs[u] * eg[u], s_old[u]) for u in nu]
    delta = [uu[u] - w_s[u] for u in nu]
    qkd = [_dot(qk[u] * dec_incl[u], delta[u]) for u in nu]
    g_last = [x[L - 1:L, :] for x in gch]
    kd = [_dot_tn(ks[u] * jnp.exp(g_last[u] - gch[u]), delta[u]) for u in nu]
    for u, (g, h) in enumerate(units):
        s_s[g, h] = jnp.exp(g_last[u]) * s_old[u] + kd[u]
        o = q_s[u] + qkd[u]
        on = o * lax.rsqrt(jnp.mean(o * o, axis=-1, keepdims=True) + NORM_EPS) * norm_ref[...]
        out_ref[g, :, sl(h)] = (on * jax.nn.silu(z[g][:, sl(h)])).astype(out_ref.dtype)

    @pl.when(c == pl.num_programs(1) - 1)
    def _():
        sout_ref[...] = s_s[...]
        cout_ref[...] = ext_s[:, HIST + L - (CONV_W - 1):HIST + L, :]


def _hist_rows(state_rows, n_seq, width):
    k = state_rows.shape[1]
    return jnp.concatenate([jnp.zeros((n_seq, HIST - k, width), F32), state_rows], axis=1)


def _delta(proj3, small_c, small_r, blk0, n_seq, n_chunks, L, conv_w, a_log, dt_bias, dn_norm, conv0, s0,
           slab=None, prev=None):
    bias_c = jnp.zeros((1, LANE), F32).at[0, 0:N_HEADS].set(a_log).at[0, N_HEADS:2 * N_HEADS].set(dt_bias)
    bias_r = jnp.zeros((32, 1), F32).at[0:N_HEADS, 0].set(a_log).at[N_HEADS:2 * N_HEADS, 0].set(dt_bias)
    assert blk0 == 0
    G = _group_size(n_seq, n_chunks, G_DELTA)
    hist = _hist_rows(conv0, n_seq, 3 * MIX)
    segs = [_seg(L, n_chunks, G, C_DN + j * MIX) for j in range(4)]
    off = segs[0][1]
    assert all(s[1] == off for s in segs)

    full2 = lambda shape: pl.BlockSpec(shape, lambda b, c: (0, 0))
    s0, s0_spec = _state_in(s0, (N_HEADS, HEAD_DIM, HEAD_DIM), G)
    s_shape, s_spec = _state_out((N_HEADS, HEAD_DIM, HEAD_DIM), n_seq, slab, G)
    inputs = [proj3, proj3, proj3, proj3, small_c, small_r, bias_c, bias_r, conv_w, hist,
              dn_norm.reshape(1, HEAD_DIM), s0]
    body, extra_in, extra_specs, aliases = _alias_prev(functools.partial(_delta_body, L=L, G=G, off=off),
                                                       len(inputs), 1, prev)
    out, s_new, conv_new = pl.pallas_call(
        body,
        grid=(n_seq // G, n_chunks),
        in_specs=[segs[0][0], segs[1][0], segs[2][0], segs[3][0],
                  _per_chunk(G, n_chunks, (L, 32)), _per_chunk(G, n_chunks, (32, L)),
                  full2((1, LANE)), full2((32, 1)), full2((CONV_W, 3 * MIX)),
                  _per_seq(G, (HIST, 3 * MIX)),
                  full2((1, HEAD_DIM)),
                  s0_spec] + extra_specs,
        out_specs=[_per_chunk(G, n_chunks, (L, MIX)),
                   s_spec,
                   _per_seq(G, (CONV_W - 1, 3 * MIX))],
        out_shape=[jax.ShapeDtypeStruct((n_seq * n_chunks, L, MIX), BF16),
                   s_shape,
                   jax.ShapeDtypeStruct((n_seq, CONV_W - 1, 3 * MIX), F32)],
        scratch_shapes=[pltpu.VMEM((G, N_HEADS, HEAD_DIM, HEAD_DIM), F32),
                        pltpu.VMEM((G, HIST + max(L, HIST), 3 * MIX), F32)],
        input_output_aliases=aliases,
        compiler_params=_cparams(("parallel", "arbitrary")),
        name="deltanet",
    )(*inputs, *extra_in)
    return out, s_new, conv_new


RWX_BLOCK = 768
RW_GATE_PAD = 512


def _rwkv_body(r_ref, k_ref, v_ref, x_ref, hist_ref, mu_ref, w0_ref, w2_ref, a0_ref, a2_ref, g2_ref,
               kk_ref, ka_ref, rk_ref, lnw_ref, lnb_ref, s0_ref,
               out_ref, sout_ref, shout_ref, s_s, ext_s, *, L, G, off):
    c = pl.program_id(1)
    width = 3 * MIX + RWX_BLOCK

    @pl.when(c == 0)
    def _():
        s_s[...] = s0_ref[...]
        ext_s[:, 0:HIST, :] = hist_ref[...]

    incl, strict = _tri_masks(L)
    r, k2, v, a, gate, kkk, e_cum, e_prev, e_neg = [], [], [], [], [], [], [], [], []
    for g in range(G):
        ext_s[g, HIST:HIST + L, 0:MIX] = r_ref[g, :, off:off + MIX]
        ext_s[g, HIST:HIST + L, MIX:2 * MIX] = k_ref[g, :, off:off + MIX]
        ext_s[g, HIST:HIST + L, 2 * MIX:3 * MIX] = v_ref[g, :, off:off + MIX]
        ext_s[g, HIST:HIST + L, 3 * MIX:3 * MIX + RW_CODES] = x_ref[g, :, off:off + RW_CODES]
        ext_s[g, HIST:HIST + L, 3 * MIX + RW_CODES:width] = jnp.zeros((L, width - 3 * MIX - RW_CODES), F32)
        cur = ext_s[g, HIST:HIST + L, :]
        prev = ext_s[g, HIST - 1:HIST - 1 + L, :]
        ext_s[g, 0:HIST, :] = ext_s[g, L:L + HIST, :]
        z = cur + (prev - cur) * mu_ref[...]
        k = z[:, MIX:2 * MIX]
        zw = z[:, 3 * MIX + RWX_W:3 * MIX + RWX_W + RW_DECAY_RANK]
        za = z[:, 3 * MIX + RWX_A:3 * MIX + RWX_A + RW_A_RANK]
        zg = z[:, 3 * MIX + RWX_G:3 * MIX + RWX_G + RW_GATE_PAD]
        w_log = -jax.nn.softplus(-(w0_ref[...] + _dot(jnp.tanh(zw), w2_ref[...]))) - 0.5
        a_g = jax.nn.sigmoid(a0_ref[...] + _dot(za, a2_ref[...]))
        log_decay = -jnp.exp(w_log)
        cum = _dot_exact(incl.astype(F32), log_decay)
        r.append(z[:, 0:MIX])
        v.append(z[:, 2 * MIX:3 * MIX])
        a.append(a_g)
        gate.append(_dot(jax.nn.sigmoid(zg), g2_ref[...]))
        kkk.append(k * kk_ref[...])
        k2.append(k * (1.0 + (a_g - 1.0) * ka_ref[...]))
        e_cum.append(jnp.exp(cum))
        e_prev.append(jnp.exp(cum - log_decay))
        e_neg.append(jnp.exp(-cum))
    units = [(g, h) for g in range(G) for h in range(RW_HEADS)]
    nu = range(len(units))
    sl = lambda h: slice(h * RW_HD, (h + 1) * RW_HD)
    kkn = [kkk[g][:, sl(h)] for g, h in units]
    kkn = [x * lax.rsqrt(jnp.sum(x * x, axis=-1, keepdims=True) + NORM_EPS) for x in kkn]
    rs = [r[g][:, sl(h)] for g, h in units]
    ks = [k2[g][:, sl(h)] for g, h in units]
    vs = [v[g][:, sl(h)] for g, h in units]
    k_til = [kkn[u] * e_prev[g][:, sl(h)] for u, (g, h) in enumerate(units)]
    r_til = [rs[u] * e_cum[g][:, sl(h)] for u, (g, h) in enumerate(units)]
    k_hat = [ks[u] * e_neg[g][:, sl(h)] for u, (g, h) in enumerate(units)]
    b_hat = [kkn[u] * a[g][:, sl(h)] * e_neg[g][:, sl(h)] for u, (g, h) in enumerate(units)]
    s_old = [s_s[g, h] for g, h in units]
    if L % 8 == 0:
        kr = [jnp.concatenate([k_til[u], r_til[u]], axis=0) for u in nu]
        mask2 = jnp.concatenate([strict, incl], axis=0)
        p_k = [jnp.where(mask2, _dot_nt(kr[u], k_hat[u]), 0.0) for u in nu]
        p_b = [jnp.where(mask2, _dot_nt(kr[u], b_hat[u]), 0.0) for u in nu]
        a_b = [x[0:L] for x in p_b]
        b_b = [x[L:2 * L] for x in p_b]
        kv = [_dot(p_k[u], vs[u]) for u in nu]
        ksd = [_dot_nt(kr[u], s_old[u]) for u in nu]
        inner = [ksd[u][0:L] + kv[u][0:L] for u in nu]
        y_part = [ksd[u][L:2 * L] + kv[u][L:2 * L] for u in nu]
    else:
        a_k = [jnp.where(strict, _dot_nt(k_til[u], k_hat[u]), 0.0) for u in nu]
        a_b = [jnp.where(strict, _dot_nt(k_til[u], b_hat[u]), 0.0) for u in nu]
        b_k = [jnp.where(incl, _dot_nt(r_til[u], k_hat[u]), 0.0) for u in nu]
        b_b = [jnp.where(incl, _dot_nt(r_til[u], b_hat[u]), 0.0) for u in nu]
        inner = [_dot_nt(k_til[u], s_old[u]) + _dot(a_k[u], vs[u]) for u in nu]
        y_part = [_dot_nt(r_til[u], s_old[u]) + _dot(b_k[u], vs[u]) for u in nu]
    t_inv = _unit_lower_inverse(a_b, L)
    uu = [_dot(t_inv[u], inner[u]) for u in nu]
    bu = [_dot(b_b[u], uu[u]) for u in nu]
    el = [e_cum[g][L - 1:L, sl(h)] for g, h in units]
    if L % 8 == 0:
        upd = [_dot_tn(jnp.concatenate([vs[u], -uu[u]], axis=0),
                       jnp.concatenate([k_hat[u] * el[u], b_hat[u] * el[u]], axis=0)) for u in nu]
    else:
        upd = [_dot_tn(vs[u], k_hat[u] * el[u]) - _dot_tn(uu[u], b_hat[u] * el[u]) for u in nu]
    for u, (g, h) in enumerate(units):
        s_s[g, h] = s_old[u] * el[u] + upd[u]
        y = y_part[u] - bu[u]
        mean = jnp.mean(y, axis=-1, keepdims=True)
        var = jnp.mean(jnp.square(y - mean), axis=-1, keepdims=True)
        y = (y - mean) * lax.rsqrt(var + RW_LN_EPS) * lnw_ref[:, sl(h)] + lnb_ref[:, sl(h)]
        y = y + jnp.sum(rs[u] * ks[u] * rk_ref[:, sl(h)], axis=-1, keepdims=True) * vs[u]
        out_ref[g, :, sl(h)] = (y * gate[g][:, sl(h)]).astype(out_ref.dtype)

    @pl.when(c == pl.num_programs(1) - 1)
    def _():
        sout_ref[...] = s_s[...]
        shout_ref[...] = ext_s[:, HIST + L - 1:HIST + L, 0:_RW_COLS]


def _rw_cols(vec):
    pad = jnp.zeros((RWX_BLOCK - (_RW_COLS - 3 * MIX),), F32)
    return jnp.concatenate([vec, pad]).reshape(1, 3 * MIX + RWX_BLOCK)


def _rwkv(proj3, blk0, n_seq, n_chunks, L, p, shift0, s0, slab=None, prev=None):
    width = 3 * MIX + RWX_BLOCK
    hist = jnp.concatenate([jnp.zeros((n_seq, HIST - 1, width), F32),
                            jnp.pad(shift0, ((0, 0), (0, width - _RW_COLS)))[:, None, :]], axis=1)
    g2 = jnp.pad(p['rw_g2'], ((0, RW_GATE_PAD - RW_GATE_RANK), (0, 0))).astype(BF16)
    row = lambda vec: vec.reshape(1, MIX)
    assert blk0 == 0
    G = _group_size(n_seq, n_chunks, G_RWKV)
    segs = [_seg(L, n_chunks, G, C_RW + j * MIX) for j in range(3)] + [_seg(L, n_chunks, G, C_RWX, RW_CODES)]
    off = segs[0][1]
    assert all(s[1] == off for s in segs)

    full2 = lambda shape: pl.BlockSpec(shape, lambda b, c: (0, 0))
    s0, s0_spec = _state_in(s0, (RW_HEADS, RW_HD, RW_HD), G)
    s_shape, s_spec = _state_out((RW_HEADS, RW_HD, RW_HD), n_seq, slab, G)
    inputs = [proj3, proj3, proj3, proj3, hist, _rw_cols(p['rw_mu']),
              row(p['rw_w0']), p['rw_w2'].astype(BF16), row(p['rw_a0']), p['rw_a2'].astype(BF16), g2,
              row(p['rw_k_k']), row(p['rw_k_a']), row(p['rw_r_k'].reshape(-1)), row(p['rw_ln_w']), row(p['rw_ln_b']),
              s0]
    body, extra_in, extra_specs, aliases = _alias_prev(functools.partial(_rwkv_body, L=L, G=G, off=off),
                                                       len(inputs), 1, prev)
    out, s_new, shift_new = pl.pallas_call(
        body,
        grid=(n_seq // G, n_chunks),
        in_specs=[segs[0][0], segs[1][0], segs[2][0], segs[3][0],
                  _per_seq(G, (HIST, width)),
                  full2((1, width)),
                  full2((1, MIX)), full2((RW_DECAY_RANK, MIX)),
                  full2((1, MIX)), full2((RW_A_RANK, MIX)),
                  full2((RW_GATE_PAD, MIX)),
                  full2((1, MIX)), full2((1, MIX)), full2((1, MIX)), full2((1, MIX)), full2((1, MIX)),
                  s0_spec] + extra_specs,
        out_specs=[_per_chunk(G, n_chunks, (L, MIX)),
                   s_spec,
                   _per_seq(G, (1, _RW_COLS))],
        out_shape=[jax.ShapeDtypeStruct((n_seq * n_chunks, L, MIX), BF16),
                   s_shape,
                   jax.ShapeDtypeStruct((n_seq, 1, _RW_COLS), F32)],
        scratch_shapes=[pltpu.VMEM((G, RW_HEADS, RW_HD, RW_HD), F32),
                        pltpu.VMEM((G, HIST + max(L, HIST), width), F32)],
        input_output_aliases=aliases,
        compiler_params=_cparams(("parallel", "arbitrary")),
        name="rwkv7",
    )(*inputs, *extra_in)
    return out, s_new, shift_new.reshape(n_seq, _RW_COLS)


def _shift_rows(x, d, fill):
    rows = lax.broadcasted_iota(jnp.int32, x.shape, 0)
    return jnp.where(rows >= d, pltpu.roll(x, d, axis=0), fill)


def _lru_body(x_ref, g_ref, hist_ref, cw_ref, cb_ref, wa_ref, ba_ref, wx_ref, bx_ref, lam_ref, h0_ref,
              out_ref, hout_ref, cout_ref, h_s, ext_s, *, L, off):
    c = pl.program_id(1)

    @pl.when(c == 0)
    def _():
        h_s[...] = h0_ref[0]
        ext_s[0:HIST, :] = hist_ref[0]

    ext_s[HIST:HIST + L, :] = x_ref[0, :, off:off + MIX]
    base = HIST - (CONV_W - 1)
    xc = ext_s[base:base + L, :] * cw_ref[0:1, :]
    for j in range(1, CONV_W):
        xc = xc + ext_s[base + j:base + j + L, :] * cw_ref[j:j + 1, :]
    ext_s[0:HIST, :] = ext_s[L:L + HIST, :]
    xc = xc + cb_ref[...]
    r = jax.nn.sigmoid(_dot(xc, wa_ref[...]) + ba_ref[...])
    i = jax.nn.sigmoid(_dot(xc, wx_ref[...]) + bx_ref[...])
    log_a = -LRU_C * r * jax.nn.softplus(-lam_ref[...])
    a = jnp.exp(log_a)
    u = jnp.sqrt(-jnp.tanh(log_a) * (a * a + 1.0)) * (i * xc)
    h_prev = h_s[...]
    if L % 8 == 0:
        acc_a, acc_h = a, u
        d = 1
        while d < L:
            acc_h = acc_a * _shift_rows(acc_h, d, 0.0) + acc_h
            acc_a = acc_a * _shift_rows(acc_a, d, 1.0)
            d *= 2
        hs = acc_a * h_prev + acc_h
        h_s[...] = hs[L - 1:L, :]
    else:
        rows = []
        for t in range(L):
            h_prev = a[t:t + 1, :] * h_prev + u[t:t + 1, :]
            rows.append(h_prev)
        hs = jnp.concatenate(rows, axis=0)
        h_s[...] = h_prev
    out_ref[0] = (hs * jax.nn.gelu(g_ref[0, :, off:off + MIX])).astype(out_ref.dtype)

    @pl.when(c == pl.num_programs(1) - 1)
    def _():
        hout_ref[0] = h_s[...]
        cout_ref[0] = ext_s[HIST + L - (CONV_W - 1):HIST + L, :]


def _block_diag(w):
    nb, bs, _ = w.shape
    eye = jnp.eye(nb, dtype=w.dtype)
    return (eye[:, None, :, None] * w[:, :, None, :]).reshape(nb * bs, nb * bs)


def _lru(proj3, blk0, n_seq, n_chunks, L, p, conv0, h0):
    hist = _hist_rows(conv0, n_seq, MIX)
    row = lambda vec: vec.reshape(1, MIX)
    full2 = lambda shape: pl.BlockSpec(shape, lambda b, c: (0, 0))
    assert blk0 == 0
    segs = [_seg(L, n_chunks, 1, C_LRU + j * MIX) for j in range(2)]
    off = segs[0][1]
    assert all(s[1] == off for s in segs)
    out, h_new, conv_new = pl.pallas_call(
        functools.partial(_lru_body, L=L, off=off),
        grid=(n_seq, n_chunks),
        in_specs=[segs[0][0], segs[1][0],
                  pl.BlockSpec((1, HIST, MIX), lambda b, c: (b, 0, 0)),
                  full2((CONV_W, MIX)), full2((1, MIX)),
                  full2((MIX, MIX)), full2((1, MIX)), full2((MIX, MIX)), full2((1, MIX)), full2((1, MIX)),
                  pl.BlockSpec((1, 1, MIX), lambda b, c: (b, 0, 0))],
        out_specs=[pl.BlockSpec((1, L, MIX), lambda b, c: (b * n_chunks + c, 0, 0)),
                   pl.BlockSpec((1, 1, MIX), lambda b, c: (b, 0, 0)),
                   pl.BlockSpec((1, CONV_W - 1, MIX), lambda b, c: (b, 0, 0))],
        out_shape=[jax.ShapeDtypeStruct((n_seq * n_chunks, L, MIX), BF16),
                   jax.ShapeDtypeStruct((n_seq, 1, MIX), F32),
                   jax.ShapeDtypeStruct((n_seq, CONV_W - 1, MIX), F32)],
        scratch_shapes=[pltpu.VMEM((1, MIX), F32),
                        pltpu.VMEM((HIST + max(L, HIST), MIX), F32)],
        compiler_params=_cparams(("parallel", "arbitrary")),
        name="rglru",
    )(proj3, proj3, hist, p['lru_conv_w'], row(p['lru_conv_b']),
      _block_diag(p['lru_wa']).astype(BF16), row(p['lru_ba']),
      _block_diag(p['lru_wx']).astype(BF16), row(p['lru_bx']), row(p['lru_lambda']),
      h0.reshape(n_seq, 1, MIX))
    return out, h_new.reshape(n_seq, MIX), conv_new


L_MLSTM = 128
L_DELTA = 64
L_RWKV = 64
L_LRU = 256


def _row_tile(m, cap):
    return max(t for t in range(16, cap + 1, 16) if m % t == 0)


def _chunk_len(t, want):
    return math.gcd(t, want)


def _small_views(small, L):
    m = small.shape[0]
    small_c = small.reshape(m // L, L, 32)
    small_r = jnp.swapaxes(small_c, 1, 2)
    return small_c, small_r


def _mixers(proj, n_seq, t_len, p, st, slab, prev):
    m = proj.shape[0]
    ml_c, ml_n, ml_m, dn_s, dn_conv, lru_h, lru_conv, rw_s, rw_shift = st
    prev_c, prev_dn, prev_rw = prev if prev is not None else (None, None, None)
    small = jnp.concatenate([proj[:, C_ML + 4 * MIX:C_ML + 4 * MIX + 2 * N_HEADS],
                             proj[:, C_DN + 4 * MIX:C_DN + 4 * MIX + 2 * N_HEADS]], axis=1)

    def view(want):
        L = _chunk_len(t_len, want)
        assert m % L == 0
        return proj.reshape(m // L, L, N_PROJ), L, t_len // L

    p3, L, nch = view(L_MLSTM)
    sc, sr = _small_views(small, L)
    out_ml, ml_c, ml_n, ml_m = _mlstm(p3, sc, sr, 0, n_seq, nch, L, p['ml_b_i'], p['ml_b_f'], p['ml_norm'],
                                      ml_c, ml_n, ml_m, slab, prev_c)
    p3, L, nch = view(L_DELTA)
    sc, sr = _small_views(small, L)
    out_dn, dn_s, dn_conv = _delta(p3, sc, sr, 0, n_seq, nch, L, p['dn_conv'], p['dn_A_log'], p['dn_dt_bias'],
                                   p['dn_norm'], dn_conv, dn_s, slab, prev_dn)
    p3, L, nch = view(L_LRU)
    out_lru, lru_h, lru_conv = _lru(p3, 0, n_seq, nch, L, p, lru_conv, lru_h)
    p3, L, nch = view(L_RWKV)
    out_rw, rw_s, rw_shift = _rwkv(p3, 0, n_seq, nch, L, p, rw_shift, rw_s, slab, prev_rw)
    outs = tuple(o.reshape(n_seq * t_len, MIX) for o in (out_ml, out_dn, out_lru, out_rw))
    return outs, (ml_c, ml_n, ml_m, dn_s, dn_conv, lru_h, lru_conv, rw_s, rw_shift)


_STATE_SHAPES = ((N_HEADS, HEAD_DIM, HEAD_DIM), (N_HEADS, HEAD_DIM), (N_HEADS,), (N_HEADS, HEAD_DIM, HEAD_DIM),
                 (CONV_W - 1, 3 * MIX), (MIX,), (CONV_W - 1, MIX), (RW_HEADS, RW_HD, RW_HD), (_RW_COLS,))
_BIG_STATES = (0, 3, 7)


def kernel(x_prompt, x_sample, state_mlstm_C, state_mlstm_n, state_mlstm_m, state_delta_S, state_delta_conv, state_rglru_h, state_rglru_conv, state_rwkv_S, state_rwkv_shift, norm_mix, w_in, ml_b_i, ml_b_f, ml_norm, dn_conv, dn_A_log, dn_dt_bias, dn_norm, lru_conv_w, lru_conv_b, lru_wa, lru_ba, lru_wx, lru_bx, lru_lambda, rw_mu, rw_w0, rw_w2, rw_a0, rw_a2, rw_g2, rw_k_k, rw_k_a, rw_r_k, rw_ln_w, rw_ln_b, w_branch, w_out, norm_ffn, w_ffn_gate, w_ffn_up, w_ffn_down, norm_final):
    params = {
        'ml_b_i': ml_b_i, 'ml_b_f': ml_b_f, 'ml_norm': ml_norm,
        'dn_conv': dn_conv, 'dn_A_log': dn_A_log, 'dn_dt_bias': dn_dt_bias, 'dn_norm': dn_norm,
        'lru_conv_w': lru_conv_w, 'lru_conv_b': lru_conv_b, 'lru_wa': lru_wa, 'lru_ba': lru_ba,
        'lru_wx': lru_wx, 'lru_bx': lru_bx, 'lru_lambda': lru_lambda,
        'rw_mu': rw_mu, 'rw_w0': rw_w0, 'rw_w2': rw_w2, 'rw_a0': rw_a0, 'rw_a2': rw_a2, 'rw_g2': rw_g2,
        'rw_k_k': rw_k_k, 'rw_k_a': rw_k_a, 'rw_r_k': rw_r_k, 'rw_ln_w': rw_ln_w, 'rw_ln_b': rw_ln_b,
    }
    depth = w_in.shape[0]
    bp, tp, d = x_prompt.shape
    bs, ts, _ = x_sample.shape
    n_p, n_s = bp * tp, bs * ts
    m = n_p + n_s
    x = jnp.concatenate([x_prompt.reshape(n_p, d), x_sample.reshape(n_s, d)], axis=0)
    tm = _row_tile(m, 1088)
    tm_panel = _row_tile(m, 2176)
    tm_norm = _row_tile(m, 544)
    sample_states = (state_mlstm_C, state_mlstm_n, state_mlstm_m, state_delta_S, state_delta_conv,
                     state_rglru_h, state_rglru_conv, state_rwkv_S, state_rwkv_shift)
    big_p = big_s = None
    new_p = [[] for _ in _STATE_SHAPES]
    new_s = [[] for _ in _STATE_SHAPES]
    wd_b = jnp.pad(w_ffn_down.astype(BF16), ((0, 0), (0, D_FF_PAD - D_FF), (0, 0)))
    w_out_b = w_out.astype(BF16)
    for l in range(depth):
        p = {name: w[l] for name, w in params.items()}
        h = _rmsnorm(x, norm_mix[l], BF16, tm_norm)
        proj = _matmul(h, w_in, l, tm_panel, 512)
        st_p = tuple(jnp.zeros((bp,) + shp, F32) for shp in _STATE_SHAPES)
        st_s = tuple((s, l) if s.ndim == 5 else s[l] for s in sample_states)
        outs_p, st_p = _mixers(proj, bp, tp, p, st_p, (l, depth), big_p)
        outs_s, st_s = _mixers(proj[n_p:], bs, ts, p, st_s, (l, depth), big_s)
        big_p = tuple(st_p[i] for i in _BIG_STATES)
        big_s = tuple(st_s[i] for i in _BIG_STATES)
        for lst, s in zip(new_p, st_p):
            lst.append(s)
        for lst, s in zip(new_s, st_s):
            lst.append(s)
        branches = jnp.stack([jnp.concatenate([a, b], axis=0) for a, b in zip(outs_p, outs_s)])
        mix = _merge(proj, branches, w_branch, l, tm, 1024)
        x = _matmul_residual(mix, w_out_b, l, x, tm, 1024, 2048)
        h2 = _rmsnorm(x, norm_ffn[l], BF16, tm_norm)
        act = _ffn_up(h2, w_ffn_gate, w_ffn_up, l, tm_panel, 256, D_FF_PAD)
        x = _matmul_residual(act, wd_b, l, x, tm, 1024, D_FF_PAD // 4)
    y = _rmsnorm(x, norm_final, F32, tm_norm)
    y_prompt = y[:n_p].reshape(bp, tp, d)
    y_sample = y[n_p:].reshape(bs, ts, d)
    def stacked(per_layer):
        return tuple(lst[-1] if i in _BIG_STATES else jnp.stack(lst) for i, lst in enumerate(per_layer))

    return (y_prompt, y_sample) + stacked(new_p) + stacked(new_s)
```

```python
import functools
import math

import jax
import jax.numpy as jnp
from jax import lax
from jax.experimental import pallas as pl
from jax.experimental.pallas import tpu as pltpu

F32 = jnp.float32
BF16 = jnp.bfloat16

D_MODEL = 4096
MIX = D_MODEL // 4
HEAD_DIM = 128
N_HEADS = MIX // HEAD_DIM
RW_HD = 64
RW_HEADS = MIX // RW_HD
RW_DECAY_RANK = 128
RW_A_RANK = 128
RW_GATE_RANK = 480
CONV_W = 4
LRU_C = 8.0
D_FF = -(-(8 * D_MODEL) // (3 * 256)) * 256
NORM_EPS = 1e-6
RW_LN_EPS = 64e-5

_ML_COLS = 4 * MIX + 2 * N_HEADS
_DN_COLS = 4 * MIX + 2 * N_HEADS
_LRU_COLS = 2 * MIX
_RW_COLS = 3 * MIX + RW_DECAY_RANK + RW_A_RANK + RW_GATE_RANK
C_ML = 0
C_DN = _ML_COLS
C_LRU = C_DN + _DN_COLS
C_RW = C_LRU + _LRU_COLS
C_RWX = C_RW + 3 * MIX
C_GATE = C_RW + _RW_COLS
N_PROJ = C_GATE + 4 * D_MODEL
RW_CODES = _RW_COLS - 3 * MIX
RWX_W = 0
RWX_A = RW_DECAY_RANK
RWX_G = RW_DECAY_RANK + RW_A_RANK
D_FF_PAD = -(-D_FF // 1024) * 1024

VMEM_LIMIT_BYTES = 56 * 1024 * 1024
LANE = 128


def _cparams(sem):
    return pltpu.CompilerParams(dimension_semantics=sem, vmem_limit_bytes=VMEM_LIMIT_BYTES)


def _rmsnorm_body(x_ref, g_ref, o_ref):
    x = x_ref[...]
    y = x * lax.rsqrt(jnp.mean(x * x, axis=-1, keepdims=True) + NORM_EPS)
    o_ref[...] = (y * g_ref[...]).astype(o_ref.dtype)


def _rmsnorm(x, g, out_dtype, tm):
    m, d = x.shape
    return pl.pallas_call(
        _rmsnorm_body,
        grid=(m // tm,),
        in_specs=[pl.BlockSpec((tm, d), lambda i: (i, 0)), pl.BlockSpec((1, d), lambda i: (0, 0))],
        out_specs=pl.BlockSpec((tm, d), lambda i: (i, 0)),
        out_shape=jax.ShapeDtypeStruct((m, d), out_dtype),
        compiler_params=_cparams(("parallel",)),
        name="rmsnorm",
    )(x, g.reshape(1, d))


def _mm_body(x_ref, w_ref, o_ref):
    o_ref[...] = jnp.dot(x_ref[...], w_ref[0].astype(BF16), preferred_element_type=F32).astype(o_ref.dtype)


def _matmul(x, w, layer, tm, tn, out_dtype=F32):
    m, k = x.shape
    n = w.shape[2]
    return pl.pallas_call(
        _mm_body,
        grid=(m // tm, pl.cdiv(n, tn)),
        in_specs=[pl.BlockSpec((tm, k), lambda i, j: (i, 0), pipeline_mode=pl.Buffered(1)),
                  pl.BlockSpec((1, k, tn), lambda i, j: (layer, 0, j))],
        out_specs=pl.BlockSpec((tm, tn), lambda i, j: (i, j)),
        out_shape=jax.ShapeDtypeStruct((m, n), out_dtype),
        compiler_params=_cparams(("parallel", "arbitrary")),
        name="matmul",
    )(x, w)


def _mm_res_body(x_ref, w_ref, r_ref, o_ref, acc_ref):
    kk = pl.program_id(2)

    @pl.when(kk == 0)
    def _():
        acc_ref[...] = r_ref[...]

    acc_ref[...] += jnp.dot(x_ref[...], w_ref[0], preferred_element_type=F32)

    @pl.when(kk == pl.num_programs(2) - 1)
    def _():
        o_ref[...] = acc_ref[...]


def _matmul_residual(x, w, layer, res, tm, tn, tk):
    m, k = x.shape
    n = w.shape[2]
    return pl.pallas_call(
        _mm_res_body,
        grid=(n // tn, m // tm, k // tk),
        in_specs=[pl.BlockSpec((tm, tk), lambda j, i, kk: (i, kk)),
                  pl.BlockSpec((1, tk, tn), lambda j, i, kk: (layer, kk, j)),
                  pl.BlockSpec((tm, tn), lambda j, i, kk: (i, j))],
        out_specs=pl.BlockSpec((tm, tn), lambda j, i, kk: (i, j)),
        out_shape=jax.ShapeDtypeStruct((m, n), F32),
        scratch_shapes=[pltpu.VMEM((tm, tn), F32)],
        compiler_params=_cparams(("parallel", "parallel", "arbitrary")),
        name="matmul_residual",
    )(x, w, res)


def _ffn_up_body(x_ref, wg_ref, wu_ref, o_ref, *, n_valid):
    x = x_ref[...]
    g = jnp.dot(x, wg_ref[0].astype(BF16), preferred_element_type=F32)
    u = jnp.dot(x, wu_ref[0].astype(BF16), preferred_element_type=F32)
    act = jax.nn.silu(g) * u
    o_ref[...] = jnp.where(pl.program_id(1) < n_valid, act, 0.0).astype(o_ref.dtype)


def _ffn_up(x, wg, wu, layer, tm, tn, n_out):
    m, k = x.shape
    n = wg.shape[2]
    assert n % tn == 0 and n_out % tn == 0
    n_valid = n // tn
    wspec = pl.BlockSpec((1, k, tn), lambda i, j: (layer, 0, jnp.minimum(j, n_valid - 1)))
    return pl.pallas_call(
        functools.partial(_ffn_up_body, n_valid=n_valid),
        grid=(m // tm, n_out // tn),
        in_specs=[pl.BlockSpec((tm, k), lambda i, j: (i, 0), pipeline_mode=pl.Buffered(1)), wspec, wspec],
        out_specs=pl.BlockSpec((tm, tn), lambda i, j: (i, j)),
        out_shape=jax.ShapeDtypeStruct((m, n_out), BF16),
        compiler_params=_cparams(("parallel", "arbitrary")),
        name="ffn_up",
    )(x, wg, wu)


def _merge_body(g_ref, b_ref, w_ref, o_ref, acc_ref):
    n = pl.program_id(2)
    term = jax.nn.sigmoid(g_ref[...]) * jnp.dot(b_ref[0], w_ref[0, 0].astype(BF16), preferred_element_type=F32)

    @pl.when(n == 0)
    def _():
        acc_ref[...] = term

    @pl.when(n > 0)
    def _():
        acc_ref[...] += term

    @pl.when(n == pl.num_programs(2) - 1)
    def _():
        o_ref[...] = acc_ref[...].astype(o_ref.dtype)


def _merge(proj, branches, wb, layer, tm, tn):
    nb, m, k = branches.shape
    n = wb.shape[3]
    return pl.pallas_call(
        _merge_body,
        grid=(n // tn, m // tm, nb),
        in_specs=[pl.BlockSpec((pl.Element(tm), pl.Element(tn)),
                               lambda j, i, b: (i * tm, (C_GATE // LANE + b * (n // LANE) + j * (tn // LANE)) * LANE)),
                  pl.BlockSpec((1, tm, k), lambda j, i, b: (b, i, 0)),
                  pl.BlockSpec((1, 1, k, tn), lambda j, i, b: (layer, b, 0, j))],
        out_specs=pl.BlockSpec((tm, tn), lambda j, i, b: (i, j)),
        out_shape=jax.ShapeDtypeStruct((m, n), BF16),
        scratch_shapes=[pltpu.VMEM((tm, tn), F32)],
        compiler_params=_cparams(("parallel", "parallel", "arbitrary")),
        name="merge",
    )(proj, branches, wb)


G_MLSTM = 2
G_DELTA = 4
G_RWKV = 4


def _group_size(n_seq, n_chunks, want):
    return math.gcd(n_seq, want) if n_chunks == 1 else 1


def _seg(L, n_chunks, G, col, width=MIX):
    start = col // LANE * LANE
    off = col - start
    win = -(-(off + width) // LANE) * LANE
    spec = pl.BlockSpec((pl.Element(G), pl.Element(L), pl.Element(win)),
                        lambda b, c: ((b * n_chunks + c) * G, 0, start))
    return spec, off


def _per_seq(G, tail):
    zeros = (0,) * len(tail)
    return pl.BlockSpec((G,) + tail, lambda b, c: (b,) + zeros)


def _per_chunk(G, n_chunks, tail):
    zeros = (0,) * len(tail)
    return pl.BlockSpec((G,) + tail, lambda b, c: (b * n_chunks + c,) + zeros)


def _state_in(state, tail, G):
    zeros = (0,) * len(tail)
    if isinstance(state, tuple):
        arr, layer = state
        return arr, pl.BlockSpec((None, G) + tail, lambda b, c: (layer, b) + zeros)
    return state, _per_seq(G, tail)


def _state_out(tail, n_seq, slab, G):
    zeros = (0,) * len(tail)
    if slab is None:
        return jax.ShapeDtypeStruct((n_seq,) + tail, F32), _per_seq(G, tail)
    layer, depth = slab
    return (jax.ShapeDtypeStruct((depth, n_seq) + tail, F32),
            pl.BlockSpec((None, G) + tail, lambda b, c: (layer, b) + zeros))


def _alias_prev(body, n_inputs, out_index, prev):
    if prev is None:
        return body, [], [], {}

    def aliased_body(*refs):
        return body(*refs[:n_inputs], *refs[n_inputs + 1:])

    return aliased_body, [prev], [pl.BlockSpec(memory_space=pl.ANY)], {n_inputs: out_index}


def _tri_masks(L):
    row = lax.broadcasted_iota(jnp.int32, (L, L), 0)
    col = lax.broadcasted_iota(jnp.int32, (L, L), 1)
    return col <= row, col < row


def _dot(a, b):
    return jnp.dot(a.astype(BF16), b.astype(BF16), preferred_element_type=F32)


def _dot_nt(a, b):
    return lax.dot_general(a.astype(BF16), b.astype(BF16), (((1,), (1,)), ((), ())), preferred_element_type=F32)


def _dot_tn(a, b):
    return lax.dot_general(a.astype(BF16), b.astype(BF16), (((0,), (0,)), ((), ())), preferred_element_type=F32)


def _dot_exact(a, b):
    return jnp.dot(a, b, precision=lax.Precision.HIGHEST, preferred_element_type=F32)


def _unit_lower_inverse(ns, L):
    size = ns[0].shape[0]
    eye = (lax.broadcasted_iota(jnp.int32, (size, size), 0)
           == lax.broadcasted_iota(jnp.int32, (size, size), 1)).astype(F32)
    ms = [-n for n in ns]
    ts = [eye + m for m in ms]
    span = 2
    while span < L:
        ms = [_dot(m, m) for m in ms]
        ts = [t + _dot(t, m) for t, m in zip(ts, ms)]
        span *= 2
    return ts


def _mlstm_body(q_ref, k_ref, v_ref, o_ref, gc_ref, gr_ref, bc_ref, br_ref, norm_ref, c0_ref, n0_ref, m0_ref,
                out_ref, cout_ref, nout_ref, mout_ref, c_s, n_s, m_s, *, L, G):
    c = pl.program_id(1)

    @pl.when(c == 0)
    def _():
        c_s[...] = c0_ref[...]
        n_s[...] = n0_ref[...]
        m_s[...] = m0_ref[...]

    incl, _ = _tri_masks(L)
    tril = incl.astype(F32)
    ig_c, ig_r, b_c, b_r = [], [], [], []
    for g in range(G):
        gc = gc_ref[g]
        gr = gr_ref[g]
        ig_c.append(gc[:, 0:N_HEADS] + bc_ref[:, 0:N_HEADS])
        ig_r.append(gr[0:N_HEADS, :] + br_ref[0:N_HEADS, :])
        lf_c = jax.nn.log_sigmoid(gc[:, N_HEADS:2 * N_HEADS] + bc_ref[:, N_HEADS:2 * N_HEADS])
        lf_r = jax.nn.log_sigmoid(gr[N_HEADS:2 * N_HEADS, :] + br_ref[N_HEADS:2 * N_HEADS, :])
        b_c.append(_dot_exact(tril, lf_c))
        b_r.append(_dot_exact(lf_r, tril.T))
    units = [(g, h) for g in range(G) for h in range(N_HEADS)]
    sl = lambda h: slice(h * HEAD_DIM, (h + 1) * HEAD_DIM)
    qs = [q_ref[g, :, sl(h)] * (HEAD_DIM ** -0.5) for g, h in units]
    ks = [k_ref[g, :, sl(h)] for g, h in units]
    vs = [v_ref[g, :, sl(h)] for g, h in units]
    c_old = [c_s[g, h] for g, h in units]
    n_old = [n_s[g, h:h + 1, :] for g, h in units]
    qk = [_dot_nt(q, k) for q, k in zip(qs, ks)]
    qc = [_dot(q, co) for q, co in zip(qs, c_old)]
    bcs = [b_c[g][:, h:h + 1] for g, h in units]
    m_prev = [m_s[g][:, h:h + 1] for g, h in units]
    log_d = [jnp.where(incl, bcs[u] - b_r[g][h:h + 1, :] + ig_r[g][h:h + 1, :], -jnp.inf)
             for u, (g, h) in enumerate(units)]
    state_w = [b + m for b, m in zip(bcs, m_prev)]
    m_t = [jnp.maximum(s, jnp.max(ld, axis=-1, keepdims=True)) for s, ld in zip(state_w, log_d)]
    scores = [a * jnp.exp(ld - mt) for a, ld, mt in zip(qk, log_d, m_t)]
    sw = [jnp.exp(s - mt) for s, mt in zip(state_w, m_t)]
    sv = [_dot(s, v) for s, v in zip(scores, vs)]
    m_last = [mt[L - 1:L, :] for mt in m_t]
    b_last = [b[L - 1:L, :] for b in bcs]
    kw = [ks[u] * jnp.exp(b_last[u] - bcs[u] + ig_c[g][:, h:h + 1] - m_last[u]) for u, (g, h) in enumerate(units)]
    kv = [_dot_tn(a, v) for a, v in zip(kw, vs)]
    for u, (g, h) in enumerate(units):
        decay = jnp.exp(b_last[u] + m_prev[u] - m_last[u])
        c_s[g, h] = decay * c_old[u] + kv[u]
        n_s[g, h:h + 1, :] = decay * n_old[u] + jnp.sum(kw[u], axis=0, keepdims=True)
        num = sv[u] + sw[u] * qc[u]
        den = jnp.sum(scores[u], axis=-1, keepdims=True) + sw[u] * jnp.sum(qs[u] * n_old[u], axis=-1, keepdims=True)
        hid = num / jnp.maximum(jnp.abs(den), jnp.exp(-m_t[u]))
        hn = hid * lax.rsqrt(jnp.mean(hid * hid, axis=-1, keepdims=True) + NORM_EPS) * norm_ref[:, sl(h)]
        out_ref[g, :, sl(h)] = (jax.nn.sigmoid(o_ref[g, :, sl(h)]) * hn).astype(out_ref.dtype)
    for g in range(G):
        m_s[g] = jnp.concatenate(m_last[g * N_HEADS:(g + 1) * N_HEADS], axis=1)

    @pl.when(c == pl.num_programs(1) - 1)
    def _():
        cout_ref[...] = c_s[...]
        nout_ref[...] = n_s[...]
        mout_ref[...] = m_s[...]


def _mlstm(proj3, small_c, small_r, blk0, n_seq, n_chunks, L, b_i, b_f, ml_norm, c0, n0, m0, slab=None, prev=None):
    assert C_ML % LANE == 0 and blk0 == 0
    G = _group_size(n_seq, n_chunks, G_MLSTM)
    bias_c = jnp.zeros((1, LANE), F32).at[0, 0:N_HEADS].set(b_i).at[0, N_HEADS:2 * N_HEADS].set(b_f)
    bias_r = jnp.zeros((32, 1), F32).at[0:N_HEADS, 0].set(b_i).at[N_HEADS:2 * N_HEADS, 0].set(b_f)

    def col(j):
        return _seg(L, n_chunks, G, C_ML + j * MIX)[0]

    full2 = lambda shape: pl.BlockSpec(shape, lambda b, c: (0, 0))
    c0, c0_spec = _state_in(c0, (N_HEADS, HEAD_DIM, HEAD_DIM), G)
    c_shape, c_spec = _state_out((N_HEADS, HEAD_DIM, HEAD_DIM), n_seq, slab, G)
    inputs = [proj3, proj3, proj3, proj3, small_c, small_r, bias_c, bias_r, ml_norm.reshape(1, MIX),
              c0, n0, m0.reshape(n_seq, 1, N_HEADS)]
    body, extra_in, extra_specs, aliases = _alias_prev(functools.partial(_mlstm_body, L=L, G=G), len(inputs), 1, prev)
    outs = pl.pallas_call(
        body,
        grid=(n_seq // G, n_chunks),
        in_specs=[col(0), col(1), col(2), col(3),
                  _per_chunk(G, n_chunks, (L, 32)), _per_chunk(G, n_chunks, (32, L)),
                  full2((1, LANE)), full2((32, 1)), full2((1, MIX)),
                  c0_spec, _per_seq(G, (N_HEADS, HEAD_DIM)), _per_seq(G, (1, N_HEADS))] + extra_specs,
        out_specs=[_per_chunk(G, n_chunks, (L, MIX)),
                   c_spec, _per_seq(G, (N_HEADS, HEAD_DIM)), _per_seq(G, (1, N_HEADS))],
        out_shape=[jax.ShapeDtypeStruct((n_seq * n_chunks, L, MIX), BF16),
                   c_shape,
                   jax.ShapeDtypeStruct((n_seq, N_HEADS, HEAD_DIM), F32),
                   jax.ShapeDtypeStruct((n_seq, 1, N_HEADS), F32)],
        scratch_shapes=[pltpu.VMEM((G, N_HEADS, HEAD_DIM, HEAD_DIM), F32),
                        pltpu.VMEM((G, N_HEADS, HEAD_DIM), F32),
                        pltpu.VMEM((G, 1, N_HEADS), F32)],
        input_output_aliases=aliases,
        compiler_params=_cparams(("parallel", "arbitrary")),
        name="mlstm",
    )(*inputs, *extra_in)
    out, c_new, n_new, m_new = outs
    return out, c_new, n_new, m_new.reshape(n_seq, N_HEADS)


HIST = 8


def _delta_body(q_ref, k_ref, v_ref, z_ref, gc_ref, gr_ref, bc_ref, br_ref, cw_ref, hist_ref, norm_ref, s0_ref,
                out_ref, sout_ref, cout_ref, s_s, ext_s, *, L, G, off):
    c = pl.program_id(1)

    @pl.when(c == 0)
    def _():
        s_s[...] = s0_ref[...]
        ext_s[:, 0:HIST, :] = hist_ref[...]

    incl, strict = _tri_masks(L)
    tril = incl.astype(F32)
    a_log_c, dt_c = bc_ref[:, 0:N_HEADS], bc_ref[:, N_HEADS:2 * N_HEADS]
    a_log_r, dt_r = br_ref[0:N_HEADS, :], br_ref[N_HEADS:2 * N_HEADS, :]
    base = HIST - (CONV_W - 1)
    qkv, z, beta_c, gcum_c, gcum_r = [], [], [], [], []
    for g in range(G):
        ext_s[g, HIST:HIST + L, 0:MIX] = q_ref[g, :, off:off + MIX]
        ext_s[g, HIST:HIST + L, MIX:2 * MIX] = k_ref[g, :, off:off + MIX]
        ext_s[g, HIST:HIST + L, 2 * MIX:3 * MIX] = v_ref[g, :, off:off + MIX]
        z.append(z_ref[g, :, off:off + MIX])
        y = ext_s[g, base:base + L, :] * cw_ref[0:1, :]
        for j in range(1, CONV_W):
            y = y + ext_s[g, base + j:base + j + L, :] * cw_ref[j:j + 1, :]
        ext_s[g, 0:HIST, :] = ext_s[g, L:L + HIST, :]
        qkv.append(jax.nn.silu(y))
        gc = gc_ref[g]
        gr = gr_ref[g]
        g_c = -jnp.exp(a_log_c) * jax.nn.softplus(gc[:, 2 * N_HEADS:3 * N_HEADS] + dt_c)
        g_r = -jnp.exp(a_log_r) * jax.nn.softplus(gr[2 * N_HEADS:3 * N_HEADS, :] + dt_r)
        beta_c.append(jax.nn.sigmoid(gc[:, 3 * N_HEADS:4 * N_HEADS]))
        gcum_c.append(_dot_exact(tril, g_c))
        gcum_r.append(_dot_exact(g_r, tril.T))
    units = [(g, h) for g in range(G) for h in range(N_HEADS)]
    nu = range(len(units))
    sl = lambda h: slice(h * HEAD_DIM, (h + 1) * HEAD_DIM)
    qs = [qkv[g][:, h * HEAD_DIM:(h + 1) * HEAD_DIM] for g, h in units]
    ks = [qkv[g][:, MIX + h * HEAD_DIM:MIX + (h + 1) * HEAD_DIM] for g, h in units]
    vs = [qkv[g][:, 2 * MIX + h * HEAD_DIM:2 * MIX + (h + 1) * HEAD_DIM] for g, h in units]
    qs = [x * lax.rsqrt(jnp.sum(x * x, axis=-1, keepdims=True) + NORM_EPS) * (HEAD_DIM ** -0.5) for x in qs]
    ks = [x * lax.rsqrt(jnp.sum(x * x, axis=-1, keepdims=True) + NORM_EPS) for x in ks]
    gch = [gcum_c[g][:, h:h + 1] for g, h in units]
    beta = [beta_c[g][:, h:h + 1] for g, h in units]
    eg = [jnp.exp(x) for x in gch]
    dec_incl = [jnp.exp(jnp.where(incl, gch[u] - gcum_r[g][h:h + 1, :], -jnp.inf)) for u, (g, h) in enumerate(units)]
    s_old = [s_s[g, h] for g, h in units]
    if L % 8 == 0:
        prod = [_dot_nt(jnp.concatenate([ks[u], qs[u]], axis=0), ks[u]) for u in nu]
        kk = [x[0:L] for x in prod]
        qk = [x[L:2 * L] for x in prod]
    else:
        kk = [_dot_nt(ks[u], ks[u]) for u in nu]
        qk = [_dot_nt(qs[u], ks[u]) for u in nu]
    a_mat = [jnp.where(strict, beta[u] * kk[u] * dec_incl[u], 0.0) for u in nu]
    t_inv = _unit_lower_inverse(a_mat, L)
    uw = [_dot(t_inv[u], jnp.concatenate([vs[u] * beta[u], ks[u] * (beta[u] * eg[u])], axis=1)) for u in nu]
    uu = [x[:, 0:HEAD_DIM] for x in uw]
    w = [x[:, HEAD_DIM:2 * HEAD_DIM] for x in uw]
    if L % 8 == 0:
        ws = [_dot(jnp.concatenate([w[u], qs[u] * eg[u]], axis=0), s_old[u]) for u in nu]
        w_s = [x[0:L] for x in ws]
        q_s = [x[L:2 * L] for x in ws]
    else:
        w_s = [_dot(w[u], s_old[u]) for u in nu]
        q_s = [_dot(qs[u] * eg[u], s_old[u]) for u in nu]
    delta = [uu[u] - w_s[u] for u in nu]
    qkd = [_dot(qk[u] * dec_incl[u], delta[u]) for u in nu]
    g_last = [x[L - 1:L, :] for x in gch]
    kd = [_dot_tn(ks[u] * jnp.exp(g_last[u] - gch[u]), delta[u]) for u in nu]
    for u, (g, h) in enumerate(units):
        s_s[g, h] = jnp.exp(g_last[u]) * s_old[u] + kd[u]
        o = q_s[u] + qkd[u]
        on = o * lax.rsqrt(jnp.mean(o * o, axis=-1, keepdims=True) + NORM_EPS) * norm_ref[...]
        out_ref[g, :, sl(h)] = (on * jax.nn.silu(z[g][:, sl(h)])).astype(out_ref.dtype)

    @pl.when(c == pl.num_programs(1) - 1)
    def _():
        sout_ref[...] = s_s[...]
        cout_ref[...] = ext_s[:, HIST + L - (CONV_W - 1):HIST + L, :]


def _hist_rows(state_rows, n_seq, width):
    k = state_rows.shape[1]
    return jnp.concatenate([jnp.zeros((n_seq, HIST - k, width), F32), state_rows], axis=1)


def _delta(proj3, small_c, small_r, blk0, n_seq, n_chunks, L, conv_w, a_log, dt_bias, dn_norm, conv0, s0,
           slab=None, prev=None):
    bias_c = jnp.zeros((1, LANE), F32).at[0, 0:N_HEADS].set(a_log).at[0, N_HEADS:2 * N_HEADS].set(dt_bias)
    bias_r = jnp.zeros((32, 1), F32).at[0:N_HEADS, 0].set(a_log).at[N_HEADS:2 * N_HEADS, 0].set(dt_bias)
    assert blk0 == 0
    G = _group_size(n_seq, n_chunks, G_DELTA)
    hist = _hist_rows(conv0, n_seq, 3 * MIX)
    segs = [_seg(L, n_chunks, G, C_DN + j * MIX) for j in range(4)]
    off = segs[0][1]
    assert all(s[1] == off for s in segs)

    full2 = lambda shape: pl.BlockSpec(shape, lambda b, c: (0, 0))
    s0, s0_spec = _state_in(s0, (N_HEADS, HEAD_DIM, HEAD_DIM), G)
    s_shape, s_spec = _state_out((N_HEADS, HEAD_DIM, HEAD_DIM), n_seq, slab, G)
    inputs = [proj3, proj3, proj3, proj3, small_c, small_r, bias_c, bias_r, conv_w, hist,
              dn_norm.reshape(1, HEAD_DIM), s0]
    body, extra_in, extra_specs, aliases = _alias_prev(functools.partial(_delta_body, L=L, G=G, off=off),
                                                       len(inputs), 1, prev)
    out, s_new, conv_new = pl.pallas_call(
        body,
        grid=(n_seq // G, n_chunks),
        in_specs=[segs[0][0], segs[1][0], segs[2][0], segs[3][0],
                  _per_chunk(G, n_chunks, (L, 32)), _per_chunk(G, n_chunks, (32, L)),
                  full2((1, LANE)), full2((32, 1)), full2((CONV_W, 3 * MIX)),
                  _per_seq(G, (HIST, 3 * MIX)),
                  full2((1, HEAD_DIM)),
                  s0_spec] + extra_specs,
        out_specs=[_per_chunk(G, n_chunks, (L, MIX)),
                   s_spec,
                   _per_seq(G, (CONV_W - 1, 3 * MIX))],
        out_shape=[jax.ShapeDtypeStruct((n_seq * n_chunks, L, MIX), BF16),
                   s_shape,
                   jax.ShapeDtypeStruct((n_seq, CONV_W - 1, 3 * MIX), F32)],
        scratch_shapes=[pltpu.VMEM((G, N_HEADS, HEAD_DIM, HEAD_DIM), F32),
                        pltpu.VMEM((G, HIST + max(L, HIST), 3 * MIX), F32)],
        input_output_aliases=aliases,
        compiler_params=_cparams(("parallel", "arbitrary")),
        name="deltanet",
    )(*inputs, *extra_in)
    return out, s_new, conv_new


RWX_BLOCK = 768
RW_GATE_PAD = 512


def _rw_paired(L):
    return L % 8 == 0


def _rwkv_heads(pre, rk_ref, lnw_ref, lnb_ref, out_ref, s_s, *, L, G):
    r, k2, v, a, gate, kkk, e_cum, e_prev, e_neg = pre
    incl, strict = _tri_masks(L)
    units = [(g, h) for g in range(G) for h in range(RW_HEADS)]
    nu = range(len(units))
    sl = lambda h: slice(h * RW_HD, (h + 1) * RW_HD)
    kkn = [kkk[g][:, sl(h)] for g, h in units]
    kkn = [x * lax.rsqrt(jnp.sum(x * x, axis=-1, keepdims=True) + NORM_EPS) for x in kkn]
    rs = [r[g][:, sl(h)] for g, h in units]
    ks = [k2[g][:, sl(h)] for g, h in units]
    vs = [v[g][:, sl(h)] for g, h in units]
    k_til = [kkn[u] * e_prev[g][:, sl(h)] for u, (g, h) in enumerate(units)]
    r_til = [rs[u] * e_cum[g][:, sl(h)] for u, (g, h) in enumerate(units)]
    k_hat = [ks[u] * e_neg[g][:, sl(h)] for u, (g, h) in enumerate(units)]
    b_hat = [kkn[u] * a[g][:, sl(h)] * e_neg[g][:, sl(h)] for u, (g, h) in enumerate(units)]
    s_old = [s_s[g, h] for g, h in units]
    a_k = [jnp.where(strict, _dot_nt(k_til[u], k_hat[u]), 0.0) for u in nu]
    a_b = [jnp.where(strict, _dot_nt(k_til[u], b_hat[u]), 0.0) for u in nu]
    b_k = [jnp.where(incl, _dot_nt(r_til[u], k_hat[u]), 0.0) for u in nu]
    b_b = [jnp.where(incl, _dot_nt(r_til[u], b_hat[u]), 0.0) for u in nu]
    inner = [_dot_nt(k_til[u], s_old[u]) + _dot(a_k[u], vs[u]) for u in nu]
    y_part = [_dot_nt(r_til[u], s_old[u]) + _dot(b_k[u], vs[u]) for u in nu]
    t_inv = _unit_lower_inverse(a_b, L)
    uu = [_dot(t_inv[u], inner[u]) for u in nu]
    bu = [_dot(b_b[u], uu[u]) for u in nu]
    el = [e_cum[g][L - 1:L, sl(h)] for g, h in units]
    upd = [_dot_tn(vs[u], k_hat[u] * el[u]) - _dot_tn(uu[u], b_hat[u] * el[u]) for u in nu]
    for u, (g, h) in enumerate(units):
        s_s[g, h] = s_old[u] * el[u] + upd[u]
        y = y_part[u] - bu[u]
        mean = jnp.mean(y, axis=-1, keepdims=True)
        var = jnp.mean(jnp.square(y - mean), axis=-1, keepdims=True)
        y = (y - mean) * lax.rsqrt(var + RW_LN_EPS) * lnw_ref[:, sl(h)] + lnb_ref[:, sl(h)]
        y = y + jnp.sum(rs[u] * ks[u] * rk_ref[:, sl(h)], axis=-1, keepdims=True) * vs[u]
        out_ref[g, :, sl(h)] = (y * gate[g][:, sl(h)]).astype(out_ref.dtype)


def _rwkv_pairs(pre, rk_ref, lnw_ref, lnb_ref, out_ref, s_s, *, L, G):
    r, k2, v, a, gate, kkk, e_cum, e_prev, e_neg = pre
    pw = 2 * RW_HD
    lo = lax.broadcasted_iota(jnp.int32, (1, pw), 1) < RW_HD
    row2 = lax.broadcasted_iota(jnp.int32, (2 * L, 2 * L), 0)
    col2 = lax.broadcasted_iota(jnp.int32, (2 * L, 2 * L), 1)
    incl_rows = jnp.where(row2 >= L, 1, 0)
    t_idx = row2 - L * incl_rows
    s_idx = jnp.where(col2 >= L, col2 - L, col2)
    mask2 = s_idx < t_idx + incl_rows
    col_lo = lax.broadcasted_iota(jnp.int32, (1, 2 * L), 1) < L
    blk_r = jnp.where(lax.broadcasted_iota(jnp.int32, (pw, pw), 0) >= RW_HD, 1, 0)
    blk_c = jnp.where(lax.broadcasted_iota(jnp.int32, (pw, pw), 1) >= RW_HD, 1, 0)
    bd_state = blk_r == blk_c

    def split_rows(x, sel):
        return jnp.concatenate([jnp.where(sel, x, 0.0), jnp.where(sel, 0.0, x)], axis=0)

    def seg_sum(x):
        s_lo = jnp.sum(jnp.where(lo, x, 0.0), axis=-1, keepdims=True)
        s_hi = jnp.sum(jnp.where(lo, 0.0, x), axis=-1, keepdims=True)
        return jnp.where(lo, s_lo, s_hi)

    units = [(g, p) for g in range(G) for p in range(RW_HEADS // 2)]
    nu = range(len(units))
    sl = lambda p: slice(p * pw, (p + 1) * pw)
    kkn = [kkk[g][:, sl(p)] for g, p in units]
    kkn = [x * lax.rsqrt(seg_sum(x * x) + NORM_EPS) for x in kkn]
    rs = [r[g][:, sl(p)] for g, p in units]
    ks = [k2[g][:, sl(p)] for g, p in units]
    vs = [v[g][:, sl(p)] for g, p in units]
    k_til = [kkn[u] * e_prev[g][:, sl(p)] for u, (g, p) in enumerate(units)]
    r_til = [rs[u] * e_cum[g][:, sl(p)] for u, (g, p) in enumerate(units)]
    k_hat = [ks[u] * e_neg[g][:, sl(p)] for u, (g, p) in enumerate(units)]
    b_hat = [kkn[u] * a[g][:, sl(p)] * e_neg[g][:, sl(p)] for u, (g, p) in enumerate(units)]
    s_old = [s_s[g, p] for g, p in units]
    kr = [jnp.concatenate([k_til[u], r_til[u]], axis=0) for u in nu]
    p_k = [jnp.where(mask2, _dot_nt(kr[u], split_rows(k_hat[u], lo)), 0.0) for u in nu]
    p_b = [jnp.where(mask2, _dot_nt(kr[u], split_rows(b_hat[u], lo)), 0.0) for u in nu]
    t_bd = _unit_lower_inverse([split_rows(x[0:L], col_lo) for x in p_b], L)
    t_pair = [x[0:L] + x[L:2 * L] for x in t_bd]
    kv = [_dot(p_k[u], split_rows(vs[u], lo)) for u in nu]
    ksd = [_dot_nt(kr[u], s_old[u]) for u in nu]
    inner = [ksd[u][0:L] + kv[u][0:L] for u in nu]
    y_part = [ksd[u][L:2 * L] + kv[u][L:2 * L] for u in nu]
    uu = [_dot(t_pair[u], split_rows(inner[u], lo)) for u in nu]
    bu = [_dot(p_b[u][L:2 * L], split_rows(uu[u], lo)) for u in nu]
    el = [e_cum[g][L - 1:L, sl(p)] for g, p in units]
    upd = [_dot_tn(jnp.concatenate([vs[u], -uu[u]], axis=0),
                   jnp.concatenate([k_hat[u] * el[u], b_hat[u] * el[u]], axis=0)) for u in nu]
    for u, (g, p) in enumerate(units):
        s_s[g, p] = s_old[u] * el[u] + jnp.where(bd_state, upd[u], 0.0)
        y = y_part[u] - bu[u]
        mean = seg_sum(y) * (1.0 / RW_HD)
        d = y - mean
        var = seg_sum(d * d) * (1.0 / RW_HD)
        y = d * lax.rsqrt(var + RW_LN_EPS) * lnw_ref[:, sl(p)] + lnb_ref[:, sl(p)]
        y = y + seg_sum(rs[u] * ks[u] * rk_ref[:, sl(p)]) * vs[u]
        out_ref[g, :, sl(p)] = (y * gate[g][:, sl(p)]).astype(out_ref.dtype)


def _rwkv_body(r_ref, k_ref, v_ref, x_ref, hist_ref, mu_ref, w0_ref, w2_ref, a0_ref, a2_ref, g2_ref,
               kk_ref, ka_ref, rk_ref, lnw_ref, lnb_ref, s0_ref,
               out_ref, sout_ref, shout_ref, s_s, ext_s, *, L, G, off):
    c = pl.program_id(1)
    width = 3 * MIX + RWX_BLOCK

    @pl.when(c == 0)
    def _():
        if _rw_paired(L):
            s_s[...] = jnp.zeros(s_s.shape, F32)
            for h in range(RW_HEADS):
                o = (h % 2) * RW_HD
                s_s[:, h // 2, o:o + RW_HD, o:o + RW_HD] = s0_ref[:, h]
        else:
            s_s[...] = s0_ref[...]
        ext_s[:, 0:HIST, :] = hist_ref[...]

    incl, strict = _tri_masks(L)
    r, k2, v, a, gate, kkk, e_cum, e_prev, e_neg = [], [], [], [], [], [], [], [], []
    for g in range(G):
        ext_s[g, HIST:HIST + L, 0:MIX] = r_ref[g, :, off:off + MIX]
        ext_s[g, HIST:HIST + L, MIX:2 * MIX] = k_ref[g, :, off:off + MIX]
        ext_s[g, HIST:HIST + L, 2 * MIX:3 * MIX] = v_ref[g, :, off:off + MIX]
        ext_s[g, HIST:HIST + L, 3 * MIX:3 * MIX + RW_CODES] = x_ref[g, :, off:off + RW_CODES]
        ext_s[g, HIST:HIST + L, 3 * MIX + RW_CODES:width] = jnp.zeros((L, width - 3 * MIX - RW_CODES), F32)
        cur = ext_s[g, HIST:HIST + L, :]
        prev = ext_s[g, HIST - 1:HIST - 1 + L, :]
        ext_s[g, 0:HIST, :] = ext_s[g, L:L + HIST, :]
        z = cur + (prev - cur) * mu_ref[...]
        k = z[:, MIX:2 * MIX]
        zw = z[:, 3 * MIX + RWX_W:3 * MIX + RWX_W + RW_DECAY_RANK]
        za = z[:, 3 * MIX + RWX_A:3 * MIX + RWX_A + RW_A_RANK]
        zg = z[:, 3 * MIX + RWX_G:3 * MIX + RWX_G + RW_GATE_PAD]
        w_log = -jax.nn.softplus(-(w0_ref[...] + _dot(jnp.tanh(zw), w2_ref[...]))) - 0.5
        a_g = jax.nn.sigmoid(a0_ref[...] + _dot(za, a2_ref[...]))
        log_decay = -jnp.exp(w_log)
        cum = _dot_exact(incl.astype(F32), log_decay)
        r.append(z[:, 0:MIX])
        v.append(z[:, 2 * MIX:3 * MIX])
        a.append(a_g)
        gate.append(_dot(jax.nn.sigmoid(zg), g2_ref[...]))
        kkk.append(k * kk_ref[...])
        k2.append(k * (1.0 + (a_g - 1.0) * ka_ref[...]))
        e_cum.append(jnp.exp(cum))
        e_prev.append(jnp.exp(cum - log_decay))
        e_neg.append(jnp.exp(-cum))
    pre = (r, k2, v, a, gate, kkk, e_cum, e_prev, e_neg)
    if _rw_paired(L):
        _rwkv_pairs(pre, rk_ref, lnw_ref, lnb_ref, out_ref, s_s, L=L, G=G)
    else:
        _rwkv_heads(pre, rk_ref, lnw_ref, lnb_ref, out_ref, s_s, L=L, G=G)

    @pl.when(c == pl.num_programs(1) - 1)
    def _():
        if _rw_paired(L):
            for h in range(RW_HEADS):
                o = (h % 2) * RW_HD
                sout_ref[:, h] = s_s[:, h // 2, o:o + RW_HD, o:o + RW_HD]
        else:
            sout_ref[...] = s_s[...]
        shout_ref[...] = ext_s[:, HIST + L - 1:HIST + L, 0:_RW_COLS]


def _rw_cols(vec):
    pad = jnp.zeros((RWX_BLOCK - (_RW_COLS - 3 * MIX),), F32)
    return jnp.concatenate([vec, pad]).reshape(1, 3 * MIX + RWX_BLOCK)


def _rwkv(proj3, blk0, n_seq, n_chunks, L, p, shift0, s0, slab=None, prev=None):
    width = 3 * MIX + RWX_BLOCK
    hist = jnp.concatenate([jnp.zeros((n_seq, HIST - 1, width), F32),
                            jnp.pad(shift0, ((0, 0), (0, width - _RW_COLS)))[:, None, :]], axis=1)
    g2 = jnp.pad(p['rw_g2'], ((0, RW_GATE_PAD - RW_GATE_RANK), (0, 0))).astype(BF16)
    row = lambda vec: vec.reshape(1, MIX)
    assert blk0 == 0
    G = _group_size(n_seq, n_chunks, G_RWKV)
    segs = [_seg(L, n_chunks, G, C_RW + j * MIX) for j in range(3)] + [_seg(L, n_chunks, G, C_RWX, RW_CODES)]
    off = segs[0][1]
    assert all(s[1] == off for s in segs)

    full2 = lambda shape: pl.BlockSpec(shape, lambda b, c: (0, 0))
    s0, s0_spec = _state_in(s0, (RW_HEADS, RW_HD, RW_HD), G)
    s_shape, s_spec = _state_out((RW_HEADS, RW_HD, RW_HD), n_seq, slab, G)
    inputs = [proj3, proj3, proj3, proj3, hist, _rw_cols(p['rw_mu']),
              row(p['rw_w0']), p['rw_w2'].astype(BF16), row(p['rw_a0']), p['rw_a2'].astype(BF16), g2,
              row(p['rw_k_k']), row(p['rw_k_a']), row(p['rw_r_k'].reshape(-1)), row(p['rw_ln_w']), row(p['rw_ln_b']),
              s0]
    body, extra_in, extra_specs, aliases = _alias_prev(functools.partial(_rwkv_body, L=L, G=G, off=off),
                                                       len(inputs), 1, prev)
    out, s_new, shift_new = pl.pallas_call(
        body,
        grid=(n_seq // G, n_chunks),
        in_specs=[segs[0][0], segs[1][0], segs[2][0], segs[3][0],
                  _per_seq(G, (HIST, width)),
                  full2((1, width)),
                  full2((1, MIX)), full2((RW_DECAY_RANK, MIX)),
                  full2((1, MIX)), full2((RW_A_RANK, MIX)),
                  full2((RW_GATE_PAD, MIX)),
                  full2((1, MIX)), full2((1, MIX)), full2((1, MIX)), full2((1, MIX)), full2((1, MIX)),
                  s0_spec] + extra_specs,
        out_specs=[_per_chunk(G, n_chunks, (L, MIX)),
                   s_spec,
                   _per_seq(G, (1, _RW_COLS))],
        out_shape=[jax.ShapeDtypeStruct((n_seq * n_chunks, L, MIX), BF16),
                   s_shape,
                   jax.ShapeDtypeStruct((n_seq, 1, _RW_COLS), F32)],
        scratch_shapes=[pltpu.VMEM((G, RW_HEADS // 2, 2 * RW_HD, 2 * RW_HD) if _rw_paired(L)
                                   else (G, RW_HEADS, RW_HD, RW_HD), F32),
                        pltpu.VMEM((G, HIST + max(L, HIST), width), F32)],
        input_output_aliases=aliases,
        compiler_params=_cparams(("parallel", "arbitrary")),
        name="rwkv7",
    )(*inputs, *extra_in)
    return out, s_new, shift_new.reshape(n_seq, _RW_COLS)


def _shift_rows(x, d, fill):
    rows = lax.broadcasted_iota(jnp.int32, x.shape, 0)
    return jnp.where(rows >= d, pltpu.roll(x, d, axis=0), fill)


def _lru_body(x_ref, g_ref, hist_ref, cw_ref, cb_ref, wa_ref, ba_ref, wx_ref, bx_ref, lam_ref, h0_ref,
              out_ref, hout_ref, cout_ref, h_s, ext_s, *, L, off):
    c = pl.program_id(1)

    @pl.when(c == 0)
    def _():
        h_s[...] = h0_ref[0]
        ext_s[0:HIST, :] = hist_ref[0]

    ext_s[HIST:HIST + L, :] = x_ref[0, :, off:off + MIX]
    base = HIST - (CONV_W - 1)
    xc = ext_s[base:base + L, :] * cw_ref[0:1, :]
    for j in range(1, CONV_W):
        xc = xc + ext_s[base + j:base + j + L, :] * cw_ref[j:j + 1, :]
    ext_s[0:HIST, :] = ext_s[L:L + HIST, :]
    xc = xc + cb_ref[...]
    r = jax.nn.sigmoid(_dot(xc, wa_ref[...]) + ba_ref[...])
    i = jax.nn.sigmoid(_dot(xc, wx_ref[...]) + bx_ref[...])
    log_a = -LRU_C * r * jax.nn.softplus(-lam_ref[...])
    a = jnp.exp(log_a)
    u = jnp.sqrt(-jnp.tanh(log_a) * (a * a + 1.0)) * (i * xc)
    h_prev = h_s[...]
    if L % 8 == 0:
        acc_a, acc_h = a, u
        d = 1
        while d < L:
            acc_h = acc_a * _shift_rows(acc_h, d, 0.0) + acc_h
            acc_a = acc_a * _shift_rows(acc_a, d, 1.0)
            d *= 2
        hs = acc_a * h_prev + acc_h
        h_s[...] = hs[L - 1:L, :]
    else:
        rows = []
        for t in range(L):
            h_prev = a[t:t + 1, :] * h_prev + u[t:t + 1, :]
            rows.append(h_prev)
        hs = jnp.concatenate(rows, axis=0)
        h_s[...] = h_prev
    out_ref[0] = (hs * jax.nn.gelu(g_ref[0, :, off:off + MIX])).astype(out_ref.dtype)

    @pl.when(c == pl.num_programs(1) - 1)
    def _():
        hout_ref[0] = h_s[...]
        cout_ref[0] = ext_s[HIST + L - (CONV_W - 1):HIST + L, :]


def _block_diag(w):
    nb, bs, _ = w.shape
    eye = jnp.eye(nb, dtype=w.dtype)
    return (eye[:, None, :, None] * w[:, :, None, :]).reshape(nb * bs, nb * bs)


def _lru(proj3, blk0, n_seq, n_chunks, L, p, conv0, h0):
    hist = _hist_rows(conv0, n_seq, MIX)
    row = lambda vec: vec.reshape(1, MIX)
    full2 = lambda shape: pl.BlockSpec(shape, lambda b, c: (0, 0))
    assert blk0 == 0
    segs = [_seg(L, n_chunks, 1, C_LRU + j * MIX) for j in range(2)]
    off = segs[0][1]
    assert all(s[1] == off for s in segs)
    out, h_new, conv_new = pl.pallas_call(
        functools.partial(_lru_body, L=L, off=off),
        grid=(n_seq, n_chunks),
        in_specs=[segs[0][0], segs[1][0],
                  pl.BlockSpec((1, HIST, MIX), lambda b, c: (b, 0, 0)),
                  full2((CONV_W, MIX)), full2((1, MIX)),
                  full2((MIX, MIX)), full2((1, MIX)), full2((MIX, MIX)), full2((1, MIX)), full2((1, MIX)),
                  pl.BlockSpec((1, 1, MIX), lambda b, c: (b, 0, 0))],
        out_specs=[pl.BlockSpec((1, L, MIX), lambda b, c: (b * n_chunks + c, 0, 0)),
                   pl.BlockSpec((1, 1, MIX), lambda b, c: (b, 0, 0)),
                   pl.BlockSpec((1, CONV_W - 1, MIX), lambda b, c: (b, 0, 0))],
        out_shape=[jax.ShapeDtypeStruct((n_seq * n_chunks, L, MIX), BF16),
                   jax.ShapeDtypeStruct((n_seq, 1, MIX), F32),
                   jax.ShapeDtypeStruct((n_seq, CONV_W - 1, MIX), F32)],
        scratch_shapes=[pltpu.VMEM((1, MIX), F32),
                        pltpu.VMEM((HIST + max(L, HIST), MIX), F32)],
        compiler_params=_cparams(("parallel", "arbitrary")),
        name="rglru",
    )(proj3, proj3, hist, p['lru_conv_w'], row(p['lru_conv_b']),
      _block_diag(p['lru_wa']).astype(BF16), row(p['lru_ba']),
      _block_diag(p['lru_wx']).astype(BF16), row(p['lru_bx']), row(p['lru_lambda']),
      h0.reshape(n_seq, 1, MIX))
    return out, h_new.reshape(n_seq, MIX), conv_new


L_MLSTM = 256
L_DELTA = 64
L_RWKV = 64
L_LRU = 256


def _row_tile(m, cap):
    return max(t for t in range(16, cap + 1, 16) if m % t == 0)


def _chunk_len(t, want):
    return math.gcd(t, want)


def _small_views(small, L):
    m = small.shape[0]
    small_c = small.reshape(m // L, L, 32)
    small_r = jnp.swapaxes(small_c, 1, 2)
    return small_c, small_r


def _mixers(proj, n_seq, t_len, p, st, slab, prev):
    m = proj.shape[0]
    ml_c, ml_n, ml_m, dn_s, dn_conv, lru_h, lru_conv, rw_s, rw_shift = st
    prev_c, prev_dn, prev_rw = prev if prev is not None else (None, None, None)
    small = jnp.concatenate([proj[:, C_ML + 4 * MIX:C_ML + 4 * MIX + 2 * N_HEADS],
                             proj[:, C_DN + 4 * MIX:C_DN + 4 * MIX + 2 * N_HEADS]], axis=1)

    def view(want):
        L = _chunk_len(t_len, want)
        assert m % L == 0
        return proj.reshape(m // L, L, N_PROJ), L, t_len // L

    p3, L, nch = view(L_MLSTM)
    sc, sr = _small_views(small, L)
    out_ml, ml_c, ml_n, ml_m = _mlstm(p3, sc, sr, 0, n_seq, nch, L, p['ml_b_i'], p['ml_b_f'], p['ml_norm'],
                                      ml_c, ml_n, ml_m, slab, prev_c)
    p3, L, nch = view(L_DELTA)
    sc, sr = _small_views(small, L)
    out_dn, dn_s, dn_conv = _delta(p3, sc, sr, 0, n_seq, nch, L, p['dn_conv'], p['dn_A_log'], p['dn_dt_bias'],
                                   p['dn_norm'], dn_conv, dn_s, slab, prev_dn)
    p3, L, nch = view(L_LRU)
    out_lru, lru_h, lru_conv = _lru(p3, 0, n_seq, nch, L, p, lru_conv, lru_h)
    p3, L, nch = view(L_RWKV)
    out_rw, rw_s, rw_shift = _rwkv(p3, 0, n_seq, nch, L, p, rw_shift, rw_s, slab, prev_rw)
    outs = tuple(o.reshape(n_seq * t_len, MIX) for o in (out_ml, out_dn, out_lru, out_rw))
    return outs, (ml_c, ml_n, ml_m, dn_s, dn_conv, lru_h, lru_conv, rw_s, rw_shift)


_STATE_SHAPES = ((N_HEADS, HEAD_DIM, HEAD_DIM), (N_HEADS, HEAD_DIM), (N_HEADS,), (N_HEADS, HEAD_DIM, HEAD_DIM),
                 (CONV_W - 1, 3 * MIX), (MIX,), (CONV_W - 1, MIX), (RW_HEADS, RW_HD, RW_HD), (_RW_COLS,))
_BIG_STATES = (0, 3, 7)


def kernel(x_prompt, x_sample, state_mlstm_C, state_mlstm_n, state_mlstm_m, state_delta_S, state_delta_conv, state_rglru_h, state_rglru_conv, state_rwkv_S, state_rwkv_shift, norm_mix, w_in, ml_b_i, ml_b_f, ml_norm, dn_conv, dn_A_log, dn_dt_bias, dn_norm, lru_conv_w, lru_conv_b, lru_wa, lru_ba, lru_wx, lru_bx, lru_lambda, rw_mu, rw_w0, rw_w2, rw_a0, rw_a2, rw_g2, rw_k_k, rw_k_a, rw_r_k, rw_ln_w, rw_ln_b, w_branch, w_out, norm_ffn, w_ffn_gate, w_ffn_up, w_ffn_down, norm_final):
    params = {
        'ml_b_i': ml_b_i, 'ml_b_f': ml_b_f, 'ml_norm': ml_norm,
        'dn_conv': dn_conv, 'dn_A_log': dn_A_log, 'dn_dt_bias': dn_dt_bias, 'dn_norm': dn_norm,
        'lru_conv_w': lru_conv_w, 'lru_conv_b': lru_conv_b, 'lru_wa': lru_wa, 'lru_ba': lru_ba,
        'lru_wx': lru_wx, 'lru_bx': lru_bx, 'lru_lambda': lru_lambda,
        'rw_mu': rw_mu, 'rw_w0': rw_w0, 'rw_w2': rw_w2, 'rw_a0': rw_a0, 'rw_a2': rw_a2, 'rw_g2': rw_g2,
        'rw_k_k': rw_k_k, 'rw_k_a': rw_k_a, 'rw_r_k': rw_r_k, 'rw_ln_w': rw_ln_w, 'rw_ln_b': rw_ln_b,
    }
    depth = w_in.shape[0]
    bp, tp, d = x_prompt.shape
    bs, ts, _ = x_sample.shape
    n_p, n_s = bp * tp, bs * ts
    m = n_p + n_s
    x = jnp.concatenate([x_prompt.reshape(n_p, d), x_sample.reshape(n_s, d)], axis=0)
    tm = _row_tile(m, 1088)
    tm_panel = _row_tile(m, 2176)
    tm_norm = _row_tile(m, 544)
    sample_states = (state_mlstm_C, state_mlstm_n, state_mlstm_m, state_delta_S, state_delta_conv,
                     state_rglru_h, state_rglru_conv, state_rwkv_S, state_rwkv_shift)
    big_p = big_s = None
    new_p = [[] for _ in _STATE_SHAPES]
    new_s = [[] for _ in _STATE_SHAPES]
    wd_b = jnp.pad(w_ffn_down.astype(BF16), ((0, 0), (0, D_FF_PAD - D_FF), (0, 0)))
    w_out_b = w_out.astype(BF16)
    for l in range(depth):
        p = {name: w[l] for name, w in params.items()}
        h = _rmsnorm(x, norm_mix[l], BF16, tm_norm)
        proj = _matmul(h, w_in, l, tm_panel, 512)
        st_p = tuple(jnp.zeros((bp,) + shp, F32) for shp in _STATE_SHAPES)
        st_s = tuple((s, l) if s.ndim == 5 else s[l] for s in sample_states)
        outs_p, st_p = _mixers(proj, bp, tp, p, st_p, (l, depth), big_p)
        outs_s, st_s = _mixers(proj[n_p:], bs, ts, p, st_s, (l, depth), big_s)
        big_p = tuple(st_p[i] for i in _BIG_STATES)
        big_s = tuple(st_s[i] for i in _BIG_STATES)
        for lst, s in zip(new_p, st_p):
            lst.append(s)
        for lst, s in zip(new_s, st_s):
            lst.append(s)
        branches = jnp.stack([jnp.concatenate([a, b], axis=0) for a, b in zip(outs_p, outs_s)])
        mix = _merge(proj, branches, w_branch, l, tm, 1024)
        x = _matmul_residual(mix, w_out_b, l, x, tm, 1024, 2048)
        h2 = _rmsnorm(x, norm_ffn[l], BF16, tm_norm)
        act = _ffn_up(h2, w_ffn_gate, w_ffn_up, l, tm_panel, 256, D_FF_PAD)
        x = _matmul_residual(act, wd_b, l, x, tm, 1024, D_FF_PAD // 4)
    y = _rmsnorm(x, norm_final, F32, tm_norm)
    y_prompt = y[:n_p].reshape(bp, tp, d)
    y_sample = y[n_p:].reshape(bs, ts, d)
    def stacked(per_layer):
        return tuple(lst[-1] if i in _BIG_STATES else jnp.stack(lst) for i, lst in enumerate(per_layer))

    return (y_prompt, y_sample) + stacked(new_p) + stacked(new_s)
```

```python
import functools
import math

import jax
import jax.numpy as jnp
from jax import lax
from jax.experimental import pallas as pl
from jax.experimental.pallas import tpu as pltpu

F32 = jnp.float32
BF16 = jnp.bfloat16

D_MODEL = 4096
MIX = D_MODEL // 4
HEAD_DIM = 128
N_HEADS = MIX // HEAD_DIM
RW_HD = 64
RW_HEADS = MIX // RW_HD
RW_DECAY_RANK = 128
RW_A_RANK = 128
RW_GATE_RANK = 480
CONV_W = 4
LRU_C = 8.0
D_FF = -(-(8 * D_MODEL) // (3 * 256)) * 256
NORM_EPS = 1e-6
RW_LN_EPS = 64e-5

_ML_COLS = 4 * MIX + 2 * N_HEADS
_DN_COLS = 4 * MIX + 2 * N_HEADS
_LRU_COLS = 2 * MIX
_RW_COLS = 3 * MIX + RW_DECAY_RANK + RW_A_RANK + RW_GATE_RANK
C_ML = 0
C_DN = _ML_COLS
C_LRU = C_DN + _DN_COLS
C_RW = C_LRU + _LRU_COLS
C_RWX = C_RW + 3 * MIX
C_GATE = C_RW + _RW_COLS
N_PROJ = C_GATE + 4 * D_MODEL
RW_CODES = _RW_COLS - 3 * MIX
RWX_W = 0
RWX_A = RW_DECAY_RANK
RWX_G = RW_DECAY_RANK + RW_A_RANK
D_FF_PAD = -(-D_FF // 1024) * 1024

VMEM_LIMIT_BYTES = 56 * 1024 * 1024
LANE = 128


def _cparams(sem):
    return pltpu.CompilerParams(dimension_semantics=sem, vmem_limit_bytes=VMEM_LIMIT_BYTES)


def _rmsnorm_body(x_ref, g_ref, o_ref):
    x = x_ref[...]
    y = x * lax.rsqrt(jnp.mean(x * x, axis=-1, keepdims=True) + NORM_EPS)
    o_ref[...] = (y * g_ref[...]).astype(o_ref.dtype)


def _rmsnorm(x, g, out_dtype, tm):
    m, d = x.shape
    return pl.pallas_call(
        _rmsnorm_body,
        grid=(m // tm,),
        in_specs=[pl.BlockSpec((tm, d), lambda i: (i, 0)), pl.BlockSpec((1, d), lambda i: (0, 0))],
        out_specs=pl.BlockSpec((tm, d), lambda i: (i, 0)),
        out_shape=jax.ShapeDtypeStruct((m, d), out_dtype),
        compiler_params=_cparams(("parallel",)),
        name="rmsnorm",
    )(x, g.reshape(1, d))


def _mm_body(x_ref, w_ref, o_ref):
    o_ref[...] = jnp.dot(x_ref[...], w_ref[0].astype(BF16), preferred_element_type=F32).astype(o_ref.dtype)


def _matmul(x, w, layer, tm, tn, out_dtype=F32):
    m, k = x.shape
    n = w.shape[2]
    return pl.pallas_call(
        _mm_body,
        grid=(m // tm, pl.cdiv(n, tn)),
        in_specs=[pl.BlockSpec((tm, k), lambda i, j: (i, 0), pipeline_mode=pl.Buffered(1)),
                  pl.BlockSpec((1, k, tn), lambda i, j: (layer, 0, j))],
        out_specs=pl.BlockSpec((tm, tn), lambda i, j: (i, j)),
        out_shape=jax.ShapeDtypeStruct((m, n), out_dtype),
        compiler_params=_cparams(("parallel", "arbitrary")),
        name="matmul",
    )(x, w)


def _mm_res_body(x_ref, w_ref, r_ref, o_ref, acc_ref):
    kk = pl.program_id(2)

    @pl.when(kk == 0)
    def _():
        acc_ref[...] = r_ref[...]

    acc_ref[...] += jnp.dot(x_ref[...], w_ref[0], preferred_element_type=F32)

    @pl.when(kk == pl.num_programs(2) - 1)
    def _():
        o_ref[...] = acc_ref[...]


def _matmul_residual(x, w, layer, res, tm, tn, tk):
    m, k = x.shape
    n = w.shape[2]
    return pl.pallas_call(
        _mm_res_body,
        grid=(n // tn, m // tm, k // tk),
        in_specs=[pl.BlockSpec((tm, tk), lambda j, i, kk: (i, kk)),
                  pl.BlockSpec((1, tk, tn), lambda j, i, kk: (layer, kk, j)),
                  pl.BlockSpec((tm, tn), lambda j, i, kk: (i, j))],
        out_specs=pl.BlockSpec((tm, tn), lambda j, i, kk: (i, j)),
        out_shape=jax.ShapeDtypeStruct((m, n), F32),
        scratch_shapes=[pltpu.VMEM((tm, tn), F32)],
        compiler_params=_cparams(("parallel", "parallel", "arbitrary")),
        name="matmul_residual",
    )(x, w, res)


def _ffn_up_body(x_ref, wg_ref, wu_ref, o_ref, *, n_valid):
    x = x_ref[...]
    g = jnp.dot(x, wg_ref[0].astype(BF16), preferred_element_type=F32)
    u = jnp.dot(x, wu_ref[0].astype(BF16), preferred_element_type=F32)
    act = jax.nn.silu(g) * u
    o_ref[...] = jnp.where(pl.program_id(1) < n_valid, act, 0.0).astype(o_ref.dtype)


def _ffn_up(x, wg, wu, layer, tm, tn, n_out):
    m, k = x.shape
    n = wg.shape[2]
    assert n % tn == 0 and n_out % tn == 0
    n_valid = n // tn
    wspec = pl.BlockSpec((1, k, tn), lambda i, j: (layer, 0, jnp.minimum(j, n_valid - 1)))
    return pl.pallas_call(
        functools.partial(_ffn_up_body, n_valid=n_valid),
        grid=(m // tm, n_out // tn),
        in_specs=[pl.BlockSpec((tm, k), lambda i, j: (i, 0), pipeline_mode=pl.Buffered(1)), wspec, wspec],
        out_specs=pl.BlockSpec((tm, tn), lambda i, j: (i, j)),
        out_shape=jax.ShapeDtypeStruct((m, n_out), BF16),
        compiler_params=_cparams(("parallel", "arbitrary")),
        name="ffn_up",
    )(x, wg, wu)


def _merge_body(g_ref, b_ref, w_ref, o_ref, acc_ref):
    n = pl.program_id(2)
    term = jax.nn.sigmoid(g_ref[...]) * jnp.dot(b_ref[0], w_ref[0, 0], preferred_element_type=F32)

    @pl.when(n == 0)
    def _():
        acc_ref[...] = term

    @pl.when(n > 0)
    def _():
        acc_ref[...] += term

    @pl.when(n == pl.num_programs(2) - 1)
    def _():
        o_ref[...] = acc_ref[...].astype(o_ref.dtype)


def _merge(proj, branches, wb, layer, tm, tn):
    nb, m, k = branches.shape
    n = wb.shape[3]
    return pl.pallas_call(
        _merge_body,
        grid=(n // tn, m // tm, nb),
        in_specs=[pl.BlockSpec((pl.Element(tm), pl.Element(tn)),
                               lambda j, i, b: (i * tm, (C_GATE // LANE + b * (n // LANE) + j * (tn // LANE)) * LANE)),
                  pl.BlockSpec((1, tm, k), lambda j, i, b: (b, i, 0)),
                  pl.BlockSpec((1, 1, k, tn), lambda j, i, b: (layer, b, 0, j))],
        out_specs=pl.BlockSpec((tm, tn), lambda j, i, b: (i, j)),
        out_shape=jax.ShapeDtypeStruct((m, n), BF16),
        scratch_shapes=[pltpu.VMEM((tm, tn), F32)],
        compiler_params=_cparams(("parallel", "parallel", "arbitrary")),
        name="merge",
    )(proj, branches, wb)


G_MLSTM = 2
G_DELTA = 4
G_RWKV = 8
G_LRU = 8


def _group_size(n_seq, n_chunks, want):
    return math.gcd(n_seq, want) if n_chunks == 1 else 1


def _seg(L, n_chunks, G, col, width=MIX):
    start = col // LANE * LANE
    off = col - start
    win = -(-(off + width) // LANE) * LANE
    spec = pl.BlockSpec((pl.Element(G), pl.Element(L), pl.Element(win)),
                        lambda b, c: ((b * n_chunks + c) * G, 0, start))
    return spec, off


def _per_seq(G, tail):
    zeros = (0,) * len(tail)
    return pl.BlockSpec((G,) + tail, lambda b, c: (b,) + zeros)


def _per_chunk(G, n_chunks, tail):
    zeros = (0,) * len(tail)
    return pl.BlockSpec((G,) + tail, lambda b, c: (b * n_chunks + c,) + zeros)


def _state_in(state, tail, G):
    zeros = (0,) * len(tail)
    if isinstance(state, tuple):
        arr, layer = state
        return arr, pl.BlockSpec((None, G) + tail, lambda b, c: (layer, b) + zeros)
    return state, _per_seq(G, tail)


def _state_out(tail, n_seq, slab, G):
    zeros = (0,) * len(tail)
    if slab is None:
        return jax.ShapeDtypeStruct((n_seq,) + tail, F32), _per_seq(G, tail)
    layer, depth = slab
    return (jax.ShapeDtypeStruct((depth, n_seq) + tail, F32),
            pl.BlockSpec((None, G) + tail, lambda b, c: (layer, b) + zeros))


def _alias_prev(body, n_inputs, out_index, prev):
    if prev is None:
        return body, [], [], {}

    def aliased_body(*refs):
        return body(*refs[:n_inputs], *refs[n_inputs + 1:])

    return aliased_body, [prev], [pl.BlockSpec(memory_space=pl.ANY)], {n_inputs: out_index}


def _tri_masks(L):
    row = lax.broadcasted_iota(jnp.int32, (L, L), 0)
    col = lax.broadcasted_iota(jnp.int32, (L, L), 1)
    return col <= row, col < row


def _dot(a, b):
    return jnp.dot(a.astype(BF16), b.astype(BF16), preferred_element_type=F32)


def _dot_nt(a, b):
    return lax.dot_general(a.astype(BF16), b.astype(BF16), (((1,), (1,)), ((), ())), preferred_element_type=F32)


def _dot_tn(a, b):
    return lax.dot_general(a.astype(BF16), b.astype(BF16), (((0,), (0,)), ((), ())), preferred_element_type=F32)


def _dot_exact(a, b):
    return jnp.dot(a, b, precision=lax.Precision.HIGHEST, preferred_element_type=F32)


def _unit_lower_inverse(ns, L):
    size = ns[0].shape[0]
    eye = (lax.broadcasted_iota(jnp.int32, (size, size), 0)
           == lax.broadcasted_iota(jnp.int32, (size, size), 1)).astype(F32)
    ms = [-n for n in ns]
    ts = [eye + m for m in ms]
    span = 2
    while span < L:
        ms = [_dot(m, m) for m in ms]
        ts = [t + _dot(t, m) for t, m in zip(ts, ms)]
        span *= 2
    return ts


def _mlstm_body(q_ref, k_ref, v_ref, o_ref, gc_ref, gr_ref, bc_ref, br_ref, norm_ref, c0_ref, n0_ref, m0_ref,
                out_ref, cout_ref, nout_ref, mout_ref, c_s, n_s, m_s, *, L, G):
    c = pl.program_id(1)

    @pl.when(c == 0)
    def _():
        c_s[...] = c0_ref[...]
        n_s[...] = n0_ref[...]
        m_s[...] = m0_ref[...]

    incl, _ = _tri_masks(L)
    tril = incl.astype(F32)
    ig_c, ig_r, b_c, b_r = [], [], [], []
    for g in range(G):
        gc = gc_ref[g]
        gr = gr_ref[g]
        ig_c.append(gc[:, 0:N_HEADS] + bc_ref[:, 0:N_HEADS])
        ig_r.append(gr[0:N_HEADS, :] + br_ref[0:N_HEADS, :])
        lf_c = jax.nn.log_sigmoid(gc[:, N_HEADS:2 * N_HEADS] + bc_ref[:, N_HEADS:2 * N_HEADS])
        lf_r = jax.nn.log_sigmoid(gr[N_HEADS:2 * N_HEADS, :] + br_ref[N_HEADS:2 * N_HEADS, :])
        b_c.append(_dot_exact(tril, lf_c))
        b_r.append(_dot_exact(lf_r, tril.T))
    units = [(g, h) for g in range(G) for h in range(N_HEADS)]
    sl = lambda h: slice(h * HEAD_DIM, (h + 1) * HEAD_DIM)
    qs = [q_ref[g, :, sl(h)] * (HEAD_DIM ** -0.5) for g, h in units]
    ks = [k_ref[g, :, sl(h)] for g, h in units]
    vs = [v_ref[g, :, sl(h)] for g, h in units]
    c_old = [c_s[g, h] for g, h in units]
    n_old = [n_s[g, h:h + 1, :] for g, h in units]
    qk = [_dot_nt(q, k) for q, k in zip(qs, ks)]
    qc = [_dot(q, co) for q, co in zip(qs, c_old)]
    bcs = [b_c[g][:, h:h + 1] for g, h in units]
    m_prev = [m_s[g][:, h:h + 1] for g, h in units]
    log_d = [jnp.where(incl, bcs[u] - b_r[g][h:h + 1, :] + ig_r[g][h:h + 1, :], -jnp.inf)
             for u, (g, h) in enumerate(units)]
    state_w = [b + m for b, m in zip(bcs, m_prev)]
    m_t = [jnp.maximum(s, jnp.max(ld, axis=-1, keepdims=True)) for s, ld in zip(state_w, log_d)]
    scores = [a * jnp.exp(ld - mt) for a, ld, mt in zip(qk, log_d, m_t)]
    sw = [jnp.exp(s - mt) for s, mt in zip(state_w, m_t)]
    sv = [_dot(s, v) for s, v in zip(scores, vs)]
    m_last = [mt[L - 1:L, :] for mt in m_t]
    b_last = [b[L - 1:L, :] for b in bcs]
    kw = [ks[u] * jnp.exp(b_last[u] - bcs[u] + ig_c[g][:, h:h + 1] - m_last[u]) for u, (g, h) in enumerate(units)]
    kv = [_dot_tn(a, v) for a, v in zip(kw, vs)]
    for u, (g, h) in enumerate(units):
        decay = jnp.exp(b_last[u] + m_prev[u] - m_last[u])
        c_s[g, h] = decay * c_old[u] + kv[u]
        n_s[g, h:h + 1, :] = decay * n_old[u] + jnp.sum(kw[u], axis=0, keepdims=True)
        num = sv[u] + sw[u] * qc[u]
        den = jnp.sum(scores[u], axis=-1, keepdims=True) + sw[u] * jnp.sum(qs[u] * n_old[u], axis=-1, keepdims=True)
        hid = num / jnp.maximum(jnp.abs(den), jnp.exp(-m_t[u]))
        hn = hid * lax.rsqrt(jnp.mean(hid * hid, axis=-1, keepdims=True) + NORM_EPS) * norm_ref[:, sl(h)]
        out_ref[g, :, sl(h)] = (jax.nn.sigmoid(o_ref[g, :, sl(h)]) * hn).astype(out_ref.dtype)
    for g in range(G):
        m_s[g] = jnp.concatenate(m_last[g * N_HEADS:(g + 1) * N_HEADS], axis=1)

    @pl.when(c == pl.num_programs(1) - 1)
    def _():
        cout_ref[...] = c_s[...]
        nout_ref[...] = n_s[...]
        mout_ref[...] = m_s[...]


def _mlstm(proj3, small_c, small_r, blk0, n_seq, n_chunks, L, b_i, b_f, ml_norm, c0, n0, m0, slab=None, prev=None):
    assert C_ML % LANE == 0 and blk0 == 0
    G = _group_size(n_seq, n_chunks, G_MLSTM)
    bias_c = jnp.zeros((1, LANE), F32).at[0, 0:N_HEADS].set(b_i).at[0, N_HEADS:2 * N_HEADS].set(b_f)
    bias_r = jnp.zeros((32, 1), F32).at[0:N_HEADS, 0].set(b_i).at[N_HEADS:2 * N_HEADS, 0].set(b_f)

    def col(j):
        return _seg(L, n_chunks, G, C_ML + j * MIX)[0]

    full2 = lambda shape: pl.BlockSpec(shape, lambda b, c: (0, 0))
    c0, c0_spec = _state_in(c0, (N_HEADS, HEAD_DIM, HEAD_DIM), G)
    c_shape, c_spec = _state_out((N_HEADS, HEAD_DIM, HEAD_DIM), n_seq, slab, G)
    inputs = [proj3, proj3, proj3, proj3, small_c, small_r, bias_c, bias_r, ml_norm.reshape(1, MIX),
              c0, n0, m0.reshape(n_seq, 1, N_HEADS)]
    body, extra_in, extra_specs, aliases = _alias_prev(functools.partial(_mlstm_body, L=L, G=G), len(inputs), 1, prev)
    outs = pl.pallas_call(
        body,
        grid=(n_seq // G, n_chunks),
        in_specs=[col(0), col(1), col(2), col(3),
                  _per_chunk(G, n_chunks, (L, 32)), _per_chunk(G, n_chunks, (32, L)),
                  full2((1, LANE)), full2((32, 1)), full2((1, MIX)),
                  c0_spec, _per_seq(G, (N_HEADS, HEAD_DIM)), _per_seq(G, (1, N_HEADS))] + extra_specs,
        out_specs=[_per_chunk(G, n_chunks, (L, MIX)),
                   c_spec, _per_seq(G, (N_HEADS, HEAD_DIM)), _per_seq(G, (1, N_HEADS))],
        out_shape=[jax.ShapeDtypeStruct((n_seq * n_chunks, L, MIX), BF16),
                   c_shape,
                   jax.ShapeDtypeStruct((n_seq, N_HEADS, HEAD_DIM), F32),
                   jax.ShapeDtypeStruct((n_seq, 1, N_HEADS), F32)],
        scratch_shapes=[pltpu.VMEM((G, N_HEADS, HEAD_DIM, HEAD_DIM), F32),
                        pltpu.VMEM((G, N_HEADS, HEAD_DIM), F32),
                        pltpu.VMEM((G, 1, N_HEADS), F32)],
        input_output_aliases=aliases,
        compiler_params=_cparams(("parallel", "arbitrary")),
        name="mlstm",
    )(*inputs, *extra_in)
    out, c_new, n_new, m_new = outs
    return out, c_new, n_new, m_new.reshape(n_seq, N_HEADS)


HIST = 8


def _delta_body(q_ref, k_ref, v_ref, z_ref, gc_ref, gr_ref, bc_ref, br_ref, cw_ref, hist_ref, norm_ref, s0_ref,
                out_ref, sout_ref, cout_ref, s_s, ext_s, *, L, G, off):
    c = pl.program_id(1)

    @pl.when(c == 0)
    def _():
        s_s[...] = s0_ref[...]
        ext_s[:, 0:HIST, :] = hist_ref[...]

    incl, strict = _tri_masks(L)
    tril = incl.astype(F32)
    a_log_c, dt_c = bc_ref[:, 0:N_HEADS], bc_ref[:, N_HEADS:2 * N_HEADS]
    a_log_r, dt_r = br_ref[0:N_HEADS, :], br_ref[N_HEADS:2 * N_HEADS, :]
    base = HIST - (CONV_W - 1)
    qkv, z, beta_c, gcum_c, gcum_r = [], [], [], [], []
    for g in range(G):
        ext_s[g, HIST:HIST + L, 0:MIX] = q_ref[g, :, off:off + MIX]
        ext_s[g, HIST:HIST + L, MIX:2 * MIX] = k_ref[g, :, off:off + MIX]
        ext_s[g, HIST:HIST + L, 2 * MIX:3 * MIX] = v_ref[g, :, off:off + MIX]
        z.append(z_ref[g, :, off:off + MIX])
        y = ext_s[g, base:base + L, :] * cw_ref[0:1, :]
        for j in range(1, CONV_W):
            y = y + ext_s[g, base + j:base + j + L, :] * cw_ref[j:j + 1, :]
        ext_s[g, 0:HIST, :] = ext_s[g, L:L + HIST, :]
        qkv.append(jax.nn.silu(y))
        gc = gc_ref[g]
        gr = gr_ref[g]
        g_c = -jnp.exp(a_log_c) * jax.nn.softplus(gc[:, 2 * N_HEADS:3 * N_HEADS] + dt_c)
        g_r = -jnp.exp(a_log_r) * jax.nn.softplus(gr[2 * N_HEADS:3 * N_HEADS, :] + dt_r)
        beta_c.append(jax.nn.sigmoid(gc[:, 3 * N_HEADS:4 * N_HEADS]))
        gcum_c.append(_dot_exact(tril, g_c))
        gcum_r.append(_dot_exact(g_r, tril.T))
    units = [(g, h) for g in range(G) for h in range(N_HEADS)]
    nu = range(len(units))
    sl = lambda h: slice(h * HEAD_DIM, (h + 1) * HEAD_DIM)
    qs = [qkv[g][:, h * HEAD_DIM:(h + 1) * HEAD_DIM] for g, h in units]
    ks = [qkv[g][:, MIX + h * HEAD_DIM:MIX + (h + 1) * HEAD_DIM] for g, h in units]
    vs = [qkv[g][:, 2 * MIX + h * HEAD_DIM:2 * MIX + (h + 1) * HEAD_DIM] for g, h in units]
    qs = [x * lax.rsqrt(jnp.sum(x * x, axis=-1, keepdims=True) + NORM_EPS) * (HEAD_DIM ** -0.5) for x in qs]
    ks = [x * lax.rsqrt(jnp.sum(x * x, axis=-1, keepdims=True) + NORM_EPS) for x in ks]
    gch = [gcum_c[g][:, h:h + 1] for g, h in units]
    beta = [beta_c[g][:, h:h + 1] for g, h in units]
    eg = [jnp.exp(x) for x in gch]
    dec_incl = [jnp.exp(jnp.where(incl, gch[u] - gcum_r[g][h:h + 1, :], -jnp.inf)) for u, (g, h) in enumerate(units)]
    s_old = [s_s[g, h] for g, h in units]
    prod = [_dot_nt(jnp.concatenate([ks[u], qs[u]], axis=0), ks[u]) for u in nu]
    kk = [x[0:L] for x in prod]
    qk = [x[L:2 * L] for x in prod]
    a_mat = [jnp.where(strict, beta[u] * kk[u] * dec_incl[u], 0.0) for u in nu]
    t_inv = _unit_lower_inverse(a_mat, L)
    uw = [_dot(t_inv[u], jnp.concatenate([vs[u] * beta[u], ks[u] * (beta[u] * eg[u])], axis=1)) for u in nu]
    uu = [x[:, 0:HEAD_DIM] for x in uw]
    w = [x[:, HEAD_DIM:2 * HEAD_DIM] for x in uw]
    ws = [_dot(jnp.concatenate([w[u], qs[u] * eg[u]], axis=0), s_old[u]) for u in nu]
    w_s = [x[0:L] for x in ws]
    q_s = [x[L:2 * L] for x in ws]
    delta = [uu[u] - w_s[u] for u in nu]
    qkd = [_dot(qk[u] * dec_incl[u], delta[u]) for u in nu]
    g_last = [x[L - 1:L, :] for x in gch]
    kd = [_dot_tn(ks[u] * jnp.exp(g_last[u] - gch[u]), delta[u]) for u in nu]
    for u, (g, h) in enumerate(units):
        s_s[g, h] = jnp.exp(g_last[u]) * s_old[u] + kd[u]
        o = q_s[u] + qkd[u]
        on = o * lax.rsqrt(jnp.mean(o * o, axis=-1, keepdims=True) + NORM_EPS) * norm_ref[...]
        out_ref[g, :, sl(h)] = (on * jax.nn.silu(z[g][:, sl(h)])).astype(out_ref.dtype)

    @pl.when(c == pl.num_programs(1) - 1)
    def _():
        sout_ref[...] = s_s[...]
        cout_ref[...] = ext_s[:, HIST + L - (CONV_W - 1):HIST + L, :]


def _hist_rows(state_rows, n_seq, width):
    k = state_rows.shape[1]
    return jnp.concatenate([jnp.zeros((n_seq, HIST - k, width), F32), state_rows], axis=1)


def _delta(proj3, small_c, small_r, blk0, n_seq, n_chunks, L, conv_w, a_log, dt_bias, dn_norm, conv0, s0,
           slab=None, prev=None):
    bias_c = jnp.zeros((1, LANE), F32).at[0, 0:N_HEADS].set(a_log).at[0, N_HEADS:2 * N_HEADS].set(dt_bias)
    bias_r = jnp.zeros((32, 1), F32).at[0:N_HEADS, 0].set(a_log).at[N_HEADS:2 * N_HEADS, 0].set(dt_bias)
    assert blk0 == 0
    G = _group_size(n_seq, n_chunks, G_DELTA)
    hist = _hist_rows(conv0, n_seq, 3 * MIX)
    segs = [_seg(L, n_chunks, G, C_DN + j * MIX) for j in range(4)]
    off = segs[0][1]
    assert all(s[1] == off for s in segs)

    full2 = lambda shape: pl.BlockSpec(shape, lambda b, c: (0, 0))
    s0, s0_spec = _state_in(s0, (N_HEADS, HEAD_DIM, HEAD_DIM), G)
    s_shape, s_spec = _state_out((N_HEADS, HEAD_DIM, HEAD_DIM), n_seq, slab, G)
    inputs = [proj3, proj3, proj3, proj3, small_c, small_r, bias_c, bias_r, conv_w, hist,
              dn_norm.reshape(1, HEAD_DIM), s0]
    body, extra_in, extra_specs, aliases = _alias_prev(functools.partial(_delta_body, L=L, G=G, off=off),
                                                       len(inputs), 1, prev)
    out, s_new, conv_new = pl.pallas_call(
        body,
        grid=(n_seq // G, n_chunks),
        in_specs=[segs[0][0], segs[1][0], segs[2][0], segs[3][0],
                  _per_chunk(G, n_chunks, (L, 32)), _per_chunk(G, n_chunks, (32, L)),
                  full2((1, LANE)), full2((32, 1)), full2((CONV_W, 3 * MIX)),
                  _per_seq(G, (HIST, 3 * MIX)),
                  full2((1, HEAD_DIM)),
                  s0_spec] + extra_specs,
        out_specs=[_per_chunk(G, n_chunks, (L, MIX)),
                   s_spec,
                   _per_seq(G, (CONV_W - 1, 3 * MIX))],
        out_shape=[jax.ShapeDtypeStruct((n_seq * n_chunks, L, MIX), BF16),
                   s_shape,
                   jax.ShapeDtypeStruct((n_seq, CONV_W - 1, 3 * MIX), F32)],
        scratch_shapes=[pltpu.VMEM((G, N_HEADS, HEAD_DIM, HEAD_DIM), F32),
                        pltpu.VMEM((G, HIST + max(L, HIST), 3 * MIX), F32)],
        input_output_aliases=aliases,
        compiler_params=_cparams(("parallel", "arbitrary")),
        name="deltanet",
    )(*inputs, *extra_in)
    return out, s_new, conv_new


RWX_BLOCK = 768
RW_GATE_PAD = 512


def _rwkv_pairs(pre, rk_ref, lnw_ref, lnb_ref, out_ref, s_s, *, L, G):
    r, k2, v, a, gate, kkk, e_cum, e_prev, e_neg = pre
    pw = 2 * RW_HD
    lo = lax.broadcasted_iota(jnp.int32, (1, pw), 1) < RW_HD
    row2 = lax.broadcasted_iota(jnp.int32, (2 * L, 2 * L), 0)
    col2 = lax.broadcasted_iota(jnp.int32, (2 * L, 2 * L), 1)
    incl_rows = jnp.where(row2 >= L, 1, 0)
    t_idx = row2 - L * incl_rows
    s_idx = jnp.where(col2 >= L, col2 - L, col2)
    mask2 = s_idx < t_idx + incl_rows
    col_lo = lax.broadcasted_iota(jnp.int32, (1, 2 * L), 1) < L
    blk_r = jnp.where(lax.broadcasted_iota(jnp.int32, (pw, pw), 0) >= RW_HD, 1, 0)
    blk_c = jnp.where(lax.broadcasted_iota(jnp.int32, (pw, pw), 1) >= RW_HD, 1, 0)
    bd_state = blk_r == blk_c

    def split_rows(x, sel):
        return jnp.concatenate([jnp.where(sel, x, 0.0), jnp.where(sel, 0.0, x)], axis=0)

    def seg_sum(x):
        s_lo = jnp.sum(jnp.where(lo, x, 0.0), axis=-1, keepdims=True)
        s_hi = jnp.sum(jnp.where(lo, 0.0, x), axis=-1, keepdims=True)
        return jnp.where(lo, s_lo, s_hi)

    units = [(g, p) for g in range(G) for p in range(RW_HEADS // 2)]
    nu = range(len(units))
    sl = lambda p: slice(p * pw, (p + 1) * pw)
    kkn = [kkk[g][:, sl(p)] for g, p in units]
    kkn = [x * lax.rsqrt(seg_sum(x * x) + NORM_EPS) for x in kkn]
    rs = [r[g][:, sl(p)] for g, p in units]
    ks = [k2[g][:, sl(p)] for g, p in units]
    vs = [v[g][:, sl(p)] for g, p in units]
    k_til = [kkn[u] * e_prev[g][:, sl(p)] for u, (g, p) in enumerate(units)]
    r_til = [rs[u] * e_cum[g][:, sl(p)] for u, (g, p) in enumerate(units)]
    k_hat = [ks[u] * e_neg[g][:, sl(p)] for u, (g, p) in enumerate(units)]
    b_hat = [kkn[u] * a[g][:, sl(p)] * e_neg[g][:, sl(p)] for u, (g, p) in enumerate(units)]
    s_old = [s_s[g, p] for g, p in units]
    kr = [jnp.concatenate([k_til[u], r_til[u]], axis=0) for u in nu]
    p_k = [jnp.where(mask2, _dot_nt(kr[u], split_rows(k_hat[u], lo)), 0.0) for u in nu]
    p_b = [jnp.where(mask2, _dot_nt(kr[u], split_rows(b_hat[u], lo)), 0.0) for u in nu]
    t_bd = _unit_lower_inverse([split_rows(x[0:L], col_lo) for x in p_b], L)
    t_pair = [x[0:L] + x[L:2 * L] for x in t_bd]
    kv = [_dot(p_k[u], split_rows(vs[u], lo)) for u in nu]
    ksd = [_dot_nt(kr[u], s_old[u]) for u in nu]
    inner = [ksd[u][0:L] + kv[u][0:L] for u in nu]
    y_part = [ksd[u][L:2 * L] + kv[u][L:2 * L] for u in nu]
    uu = [_dot(t_pair[u], split_rows(inner[u], lo)) for u in nu]
    bu = [_dot(p_b[u][L:2 * L], split_rows(uu[u], lo)) for u in nu]
    el = [e_cum[g][L - 1:L, sl(p)] for g, p in units]
    upd = [_dot_tn(jnp.concatenate([vs[u], -uu[u]], axis=0),
                   jnp.concatenate([k_hat[u] * el[u], b_hat[u] * el[u]], axis=0)) for u in nu]
    for u, (g, p) in enumerate(units):
        s_s[g, p] = s_old[u] * el[u] + jnp.where(bd_state, upd[u], 0.0)
        y = y_part[u] - bu[u]
        mean = seg_sum(y) * (1.0 / RW_HD)
        d = y - mean
        var = seg_sum(d * d) * (1.0 / RW_HD)
        y = d * lax.rsqrt(var + RW_LN_EPS) * lnw_ref[:, sl(p)] + lnb_ref[:, sl(p)]
        y = y + seg_sum(rs[u] * ks[u] * rk_ref[:, sl(p)]) * vs[u]
        out_ref[g, :, sl(p)] = (y * gate[g][:, sl(p)]).astype(out_ref.dtype)


def _rwkv_body(r_ref, k_ref, v_ref, x_ref, hist_ref, mu_ref, w0_ref, w2_ref, a0_ref, a2_ref, g2_ref,
               kk_ref, ka_ref, rk_ref, lnw_ref, lnb_ref, s0_ref,
               out_ref, sout_ref, shout_ref, s_s, ext_s, *, L, G, off):
    c = pl.program_id(1)
    width = 3 * MIX + RWX_BLOCK

    @pl.when(c == 0)
    def _():
        s_s[...] = jnp.zeros(s_s.shape, F32)
        for h in range(RW_HEADS):
            o = (h % 2) * RW_HD
            s_s[:, h // 2, o:o + RW_HD, o:o + RW_HD] = s0_ref[:, h]
        ext_s[:, 0:HIST, :] = hist_ref[...]

    incl, _ = _tri_masks(L)
    r, k2, v, a, gate, kkk, e_cum, e_prev, e_neg = [], [], [], [], [], [], [], [], []
    for g in range(G):
        ext_s[g, HIST:HIST + L, 0:MIX] = r_ref[g, :, off:off + MIX]
        ext_s[g, HIST:HIST + L, MIX:2 * MIX] = k_ref[g, :, off:off + MIX]
        ext_s[g, HIST:HIST + L, 2 * MIX:3 * MIX] = v_ref[g, :, off:off + MIX]
        ext_s[g, HIST:HIST + L, 3 * MIX:3 * MIX + RW_CODES] = x_ref[g, :, off:off + RW_CODES]
        ext_s[g, HIST:HIST + L, 3 * MIX + RW_CODES:width] = jnp.zeros((L, width - 3 * MIX - RW_CODES), F32)
        cur = ext_s[g, HIST:HIST + L, :]
        prev = ext_s[g, HIST - 1:HIST - 1 + L, :]
        ext_s[g, 0:HIST, :] = ext_s[g, L:L + HIST, :]
        z = cur + (prev - cur) * mu_ref[...]
        k = z[:, MIX:2 * MIX]
        zw = z[:, 3 * MIX + RWX_W:3 * MIX + RWX_W + RW_DECAY_RANK]
        za = z[:, 3 * MIX + RWX_A:3 * MIX + RWX_A + RW_A_RANK]
        zg = z[:, 3 * MIX + RWX_G:3 * MIX + RWX_G + RW_GATE_PAD]
        w_log = -jax.nn.softplus(-(w0_ref[...] + _dot(jnp.tanh(zw), w2_ref[...]))) - 0.5
        a_g = jax.nn.sigmoid(a0_ref[...] + _dot(za, a2_ref[...]))
        log_decay = -jnp.exp(w_log)
        cum = _dot_exact(incl.astype(F32), log_decay)
        r.append(z[:, 0:MIX])
        v.append(z[:, 2 * MIX:3 * MIX])
        a.append(a_g)
        gate.append(_dot(jax.nn.sigmoid(zg), g2_ref[...]))
        kkk.append(k * kk_ref[...])
        k2.append(k * (1.0 + (a_g - 1.0) * ka_ref[...]))
        e_cum.append(jnp.exp(cum))
        e_prev.append(jnp.exp(cum - log_decay))
        e_neg.append(jnp.exp(-cum))
    pre = (r, k2, v, a, gate, kkk, e_cum, e_prev, e_neg)
    _rwkv_pairs(pre, rk_ref, lnw_ref, lnb_ref, out_ref, s_s, L=L, G=G)

    @pl.when(c == pl.num_programs(1) - 1)
    def _():
        for h in range(RW_HEADS):
            o = (h % 2) * RW_HD
            sout_ref[:, h] = s_s[:, h // 2, o:o + RW_HD, o:o + RW_HD]
        shout_ref[...] = ext_s[:, HIST + L - 1:HIST + L, 0:_RW_COLS]


def _rw_cols(vec):
    pad = jnp.zeros((RWX_BLOCK - (_RW_COLS - 3 * MIX),), F32)
    return jnp.concatenate([vec, pad]).reshape(1, 3 * MIX + RWX_BLOCK)


def _rwkv(proj3, blk0, n_seq, n_chunks, L, p, shift0, s0, slab=None, prev=None):
    width = 3 * MIX + RWX_BLOCK
    hist = jnp.concatenate([jnp.zeros((n_seq, HIST - 1, width), F32),
                            jnp.pad(shift0, ((0, 0), (0, width - _RW_COLS)))[:, None, :]], axis=1)
    g2 = jnp.pad(p['rw_g2'], ((0, RW_GATE_PAD - RW_GATE_RANK), (0, 0))).astype(BF16)
    row = lambda vec: vec.reshape(1, MIX)
    assert blk0 == 0
    G = _group_size(n_seq, n_chunks, G_RWKV)
    segs = [_seg(L, n_chunks, G, C_RW + j * MIX) for j in range(3)] + [_seg(L, n_chunks, G, C_RWX, RW_CODES)]
    off = segs[0][1]
    assert all(s[1] == off for s in segs)

    full2 = lambda shape: pl.BlockSpec(shape, lambda b, c: (0, 0))
    s0, s0_spec = _state_in(s0, (RW_HEADS, RW_HD, RW_HD), G)
    s_shape, s_spec = _state_out((RW_HEADS, RW_HD, RW_HD), n_seq, slab, G)
    inputs = [proj3, proj3, proj3, proj3, hist, _rw_cols(p['rw_mu']),
              row(p['rw_w0']), p['rw_w2'].astype(BF16), row(p['rw_a0']), p['rw_a2'].astype(BF16), g2,
              row(p['rw_k_k']), row(p['rw_k_a']), row(p['rw_r_k'].reshape(-1)), row(p['rw_ln_w']), row(p['rw_ln_b']),
              s0]
    body, extra_in, extra_specs, aliases = _alias_prev(functools.partial(_rwkv_body, L=L, G=G, off=off),
                                                       len(inputs), 1, prev)
    out, s_new, shift_new = pl.pallas_call(
        body,
        grid=(n_seq // G, n_chunks),
        in_specs=[segs[0][0], segs[1][0], segs[2][0], segs[3][0],
                  _per_seq(G, (HIST, width)),
                  full2((1, width)),
                  full2((1, MIX)), full2((RW_DECAY_RANK, MIX)),
                  full2((1, MIX)), full2((RW_A_RANK, MIX)),
                  full2((RW_GATE_PAD, MIX)),
                  full2((1, MIX)), full2((1, MIX)), full2((1, MIX)), full2((1, MIX)), full2((1, MIX)),
                  s0_spec] + extra_specs,
        out_specs=[_per_chunk(G, n_chunks, (L, MIX)),
                   s_spec,
                   _per_seq(G, (1, _RW_COLS))],
        out_shape=[jax.ShapeDtypeStruct((n_seq * n_chunks, L, MIX), BF16),
                   s_shape,
                   jax.ShapeDtypeStruct((n_seq, 1, _RW_COLS), F32)],
        scratch_shapes=[pltpu.VMEM((G, RW_HEADS // 2, 2 * RW_HD, 2 * RW_HD), F32),
                        pltpu.VMEM((G, HIST + max(L, HIST), width), F32)],
        input_output_aliases=aliases,
        compiler_params=_cparams(("parallel", "arbitrary")),
        name="rwkv7",
    )(*inputs, *extra_in)
    return out, s_new, shift_new.reshape(n_seq, _RW_COLS)


def _shift_rows(x, d, fill):
    rows = lax.broadcasted_iota(jnp.int32, x.shape, 0)
    return jnp.where(rows >= d, pltpu.roll(x, d, axis=0), fill)


def _lru_body(x_ref, g_ref, hist_ref, cw_ref, cb_ref, wa_ref, ba_ref, wx_ref, bx_ref, lam_ref, h0_ref,
              out_ref, hout_ref, cout_ref, h_s, ext_s, xc_s, *, L, G, off):
    c = pl.program_id(1)

    @pl.when(c == 0)
    def _():
        h_s[...] = h0_ref[...]
        ext_s[:, 0:HIST, :] = hist_ref[...]

    base = HIST - (CONV_W - 1)
    for g in range(G):
        ext_s[g, HIST:HIST + L, :] = x_ref[g, :, off:off + MIX]
        conv = ext_s[g, base:base + L, :] * cw_ref[0:1, :]
        for j in range(1, CONV_W):
            conv = conv + ext_s[g, base + j:base + j + L, :] * cw_ref[j:j + 1, :]
        ext_s[g, 0:HIST, :] = ext_s[g, L:L + HIST, :]
        xc_s[g * L:(g + 1) * L, :] = conv + cb_ref[...]
    xc = xc_s[...]
    r = jax.nn.sigmoid(_dot(xc, wa_ref[...]) + ba_ref[...])
    i = jax.nn.sigmoid(_dot(xc, wx_ref[...]) + bx_ref[...])
    log_a = -LRU_C * r * jax.nn.softplus(-lam_ref[...])
    a = jnp.exp(log_a)
    u_all = jnp.sqrt(-jnp.tanh(log_a) * (a * a + 1.0)) * (i * xc)
    for g in range(G):
        a_g = a[g * L:(g + 1) * L, :]
        u_g = u_all[g * L:(g + 1) * L, :]
        h_prev = h_s[g]
        if L % 8 == 0:
            acc_a, acc_h = a_g, u_g
            d = 1
            while d < L:
                acc_h = acc_a * _shift_rows(acc_h, d, 0.0) + acc_h
                acc_a = acc_a * _shift_rows(acc_a, d, 1.0)
                d *= 2
            hs = acc_a * h_prev + acc_h
            h_s[g] = hs[L - 1:L, :]
        else:
            rows = []
            for t in range(L):
                h_prev = a_g[t:t + 1, :] * h_prev + u_g[t:t + 1, :]
                rows.append(h_prev)
            hs = jnp.concatenate(rows, axis=0)
            h_s[g] = h_prev
        out_ref[g] = (hs * jax.nn.gelu(g_ref[g, :, off:off + MIX])).astype(out_ref.dtype)

    @pl.when(c == pl.num_programs(1) - 1)
    def _():
        hout_ref[...] = h_s[...]
        cout_ref[...] = ext_s[:, HIST + L - (CONV_W - 1):HIST + L, :]


def _block_diag(w):
    nb, bs, _ = w.shape
    eye = jnp.eye(nb, dtype=w.dtype)
    return (eye[:, None, :, None] * w[:, :, None, :]).reshape(nb * bs, nb * bs)


def _lru(proj3, blk0, n_seq, n_chunks, L, p, conv0, h0):
    hist = _hist_rows(conv0, n_seq, MIX)
    row = lambda vec: vec.reshape(1, MIX)
    full2 = lambda shape: pl.BlockSpec(shape, lambda b, c: (0, 0))
    assert blk0 == 0
    G = _group_size(n_seq, n_chunks, G_LRU)
    segs = [_seg(L, n_chunks, G, C_LRU + j * MIX) for j in range(2)]
    off = segs[0][1]
    assert all(s[1] == off for s in segs)
    out, h_new, conv_new = pl.pallas_call(
        functools.partial(_lru_body, L=L, G=G, off=off),
        grid=(n_seq // G, n_chunks),
        in_specs=[segs[0][0], segs[1][0],
                  _per_seq(G, (HIST, MIX)),
                  full2((CONV_W, MIX)), full2((1, MIX)),
                  full2((MIX, MIX)), full2((1, MIX)), full2((MIX, MIX)), full2((1, MIX)), full2((1, MIX)),
                  _per_seq(G, (1, MIX))],
        out_specs=[_per_chunk(G, n_chunks, (L, MIX)),
                   _per_seq(G, (1, MIX)),
                   _per_seq(G, (CONV_W - 1, MIX))],
        out_shape=[jax.ShapeDtypeStruct((n_seq * n_chunks, L, MIX), BF16),
                   jax.ShapeDtypeStruct((n_seq, 1, MIX), F32),
                   jax.ShapeDtypeStruct((n_seq, CONV_W - 1, MIX), F32)],
        scratch_shapes=[pltpu.VMEM((G, 1, MIX), F32),
                        pltpu.VMEM((G, HIST + max(L, HIST), MIX), F32),
                        pltpu.VMEM((G * L, MIX), F32)],
        compiler_params=_cparams(("parallel", "arbitrary")),
        name="rglru",
    )(proj3, proj3, hist, p['lru_conv_w'], row(p['lru_conv_b']),
      _block_diag(p['lru_wa']).astype(BF16), row(p['lru_ba']),
      _block_diag(p['lru_wx']).astype(BF16), row(p['lru_bx']), row(p['lru_lambda']),
      h0.reshape(n_seq, 1, MIX))
    return out, h_new.reshape(n_seq, MIX), conv_new


L_MLSTM = 256
L_DELTA = 64
L_RWKV = 64
L_LRU = 256


def _row_tile(m, cap):
    return max(t for t in range(16, cap + 1, 16) if m % t == 0)


def _chunk_len(t, want):
    return math.gcd(t, want)


def _small_views(small, L):
    m = small.shape[0]
    small_c = small.reshape(m // L, L, 32)
    small_r = jnp.swapaxes(small_c, 1, 2)
    return small_c, small_r


def _mixers(proj, n_seq, t_len, p, st, slab, prev):
    m = proj.shape[0]
    ml_c, ml_n, ml_m, dn_s, dn_conv, lru_h, lru_conv, rw_s, rw_shift = st
    prev_c, prev_dn, prev_rw = prev if prev is not None else (None, None, None)
    small = jnp.concatenate([proj[:, C_ML + 4 * MIX:C_ML + 4 * MIX + 2 * N_HEADS],
                             proj[:, C_DN + 4 * MIX:C_DN + 4 * MIX + 2 * N_HEADS]], axis=1)

    def view(want):
        L = _chunk_len(t_len, want)
        assert m % L == 0
        return proj.reshape(m // L, L, N_PROJ), L, t_len // L

    p3, L, nch = view(L_MLSTM)
    sc, sr = _small_views(small, L)
    out_ml, ml_c, ml_n, ml_m = _mlstm(p3, sc, sr, 0, n_seq, nch, L, p['ml_b_i'], p['ml_b_f'], p['ml_norm'],
                                      ml_c, ml_n, ml_m, slab, prev_c)
    p3, L, nch = view(L_DELTA)
    sc, sr = _small_views(small, L)
    out_dn, dn_s, dn_conv = _delta(p3, sc, sr, 0, n_seq, nch, L, p['dn_conv'], p['dn_A_log'], p['dn_dt_bias'],
                                   p['dn_norm'], dn_conv, dn_s, slab, prev_dn)
    p3, L, nch = view(L_LRU)
    out_lru, lru_h, lru_conv = _lru(p3, 0, n_seq, nch, L, p, lru_conv, lru_h)
    p3, L, nch = view(L_RWKV)
    out_rw, rw_s, rw_shift = _rwkv(p3, 0, n_seq, nch, L, p, rw_shift, rw_s, slab, prev_rw)
    outs = tuple(o.reshape(n_seq * t_len, MIX) for o in (out_ml, out_dn, out_lru, out_rw))
    return outs, (ml_c, ml_n, ml_m, dn_s, dn_conv, lru_h, lru_conv, rw_s, rw_shift)


_STATE_SHAPES = ((N_HEADS, HEAD_DIM, HEAD_DIM), (N_HEADS, HEAD_DIM), (N_HEADS,), (N_HEADS, HEAD_DIM, HEAD_DIM),
                 (CONV_W - 1, 3 * MIX), (MIX,), (CONV_W - 1, MIX), (RW_HEADS, RW_HD, RW_HD), (_RW_COLS,))
_BIG_STATES = (0, 3, 7)


def kernel(x_prompt, x_sample, state_mlstm_C, state_mlstm_n, state_mlstm_m, state_delta_S, state_delta_conv, state_rglru_h, state_rglru_conv, state_rwkv_S, state_rwkv_shift, norm_mix, w_in, ml_b_i, ml_b_f, ml_norm, dn_conv, dn_A_log, dn_dt_bias, dn_norm, lru_conv_w, lru_conv_b, lru_wa, lru_ba, lru_wx, lru_bx, lru_lambda, rw_mu, rw_w0, rw_w2, rw_a0, rw_a2, rw_g2, rw_k_k, rw_k_a, rw_r_k, rw_ln_w, rw_ln_b, w_branch, w_out, norm_ffn, w_ffn_gate, w_ffn_up, w_ffn_down, norm_final):
    params = {
        'ml_b_i': ml_b_i, 'ml_b_f': ml_b_f, 'ml_norm': ml_norm,
        'dn_conv': dn_conv, 'dn_A_log': dn_A_log, 'dn_dt_bias': dn_dt_bias, 'dn_norm': dn_norm,
        'lru_conv_w': lru_conv_w, 'lru_conv_b': lru_conv_b, 'lru_wa': lru_wa, 'lru_ba': lru_ba,
        'lru_wx': lru_wx, 'lru_bx': lru_bx, 'lru_lambda': lru_lambda,
        'rw_mu': rw_mu, 'rw_w0': rw_w0, 'rw_w2': rw_w2, 'rw_a0': rw_a0, 'rw_a2': rw_a2, 'rw_g2': rw_g2,
        'rw_k_k': rw_k_k, 'rw_k_a': rw_k_a, 'rw_r_k': rw_r_k, 'rw_ln_w': rw_ln_w, 'rw_ln_b': rw_ln_b,
    }
    depth = w_in.shape[0]
    bp, tp, d = x_prompt.shape
    bs, ts, _ = x_sample.shape
    n_p, n_s = bp * tp, bs * ts
    m = n_p + n_s
    x = jnp.concatenate([x_prompt.reshape(n_p, d), x_sample.reshape(n_s, d)], axis=0)
    tm = _row_tile(m, 1088)
    tm_panel = _row_tile(m, 2176)
    tm_norm = _row_tile(m, 544)
    sample_states = (state_mlstm_C, state_mlstm_n, state_mlstm_m, state_delta_S, state_delta_conv,
                     state_rglru_h, state_rglru_conv, state_rwkv_S, state_rwkv_shift)
    big_p = big_s = None
    new_p = [[] for _ in _STATE_SHAPES]
    new_s = [[] for _ in _STATE_SHAPES]
    wd_b = jnp.pad(w_ffn_down.astype(BF16), ((0, 0), (0, D_FF_PAD - D_FF), (0, 0)))
    w_out_b = w_out.astype(BF16)
    w_branch_b = w_branch.astype(BF16)
    for l in range(depth):
        p = {name: w[l] for name, w in params.items()}
        h = _rmsnorm(x, norm_mix[l], BF16, tm_norm)
        proj = _matmul(h, w_in, l, tm_panel, 512)
        st_p = tuple(jnp.zeros((bp,) + shp, F32) for shp in _STATE_SHAPES)
        st_s = tuple((s, l) if s.ndim == 5 else s[l] for s in sample_states)
        outs_p, st_p = _mixers(proj, bp, tp, p, st_p, (l, depth), big_p)
        outs_s, st_s = _mixers(proj[n_p:], bs, ts, p, st_s, (l, depth), big_s)
        big_p = tuple(st_p[i] for i in _BIG_STATES)
        big_s = tuple(st_s[i] for i in _BIG_STATES)
        for lst, s in zip(new_p, st_p):
            lst.append(s)
        for lst, s in zip(new_s, st_s):
            lst.append(s)
        branches = jnp.stack([jnp.concatenate([a, b], axis=0) for a, b in zip(outs_p, outs_s)])
        mix = _merge(proj, branches, w_branch_b, l, tm, 1024)
        x = _matmul_residual(mix, w_out_b, l, x, tm, 1024, 2048)
        h2 = _rmsnorm(x, norm_ffn[l], BF16, tm_norm)
        act = _ffn_up(h2, w_ffn_gate, w_ffn_up, l, tm_panel, 256, D_FF_PAD)
        x = _matmul_residual(act, wd_b, l, x, tm, 1024, D_FF_PAD // 4)
    y = _rmsnorm(x, norm_final, F32, tm_norm)
    y_prompt = y[:n_p].reshape(bp, tp, d)
    y_sample = y[n_p:].reshape(bs, ts, d)
    def stacked(per_layer):
        return tuple(lst[-1] if i in _BIG_STATES else jnp.stack(lst) for i, lst in enumerate(per_layer))

    return (y_prompt, y_sample) + stacked(new_p) + stacked(new_s)
```

```python
import functools
import math

import jax
import jax.numpy as jnp
from jax import lax
from jax.experimental import pallas as pl
from jax.experimental.pallas import tpu as pltpu

F32 = jnp.float32
BF16 = jnp.bfloat16

D_MODEL = 4096
MIX = D_MODEL // 4
HEAD_DIM = 128
N_HEADS = MIX // HEAD_DIM
RW_HD = 64
RW_HEADS = MIX // RW_HD
RW_DECAY_RANK = 128
RW_A_RANK = 128
RW_GATE_RANK = 480
CONV_W = 4
LRU_C = 8.0
D_FF = -(-(8 * D_MODEL) // (3 * 256)) * 256
NORM_EPS = 1e-6
RW_LN_EPS = 64e-5

_ML_COLS = 4 * MIX + 2 * N_HEADS
_DN_COLS = 4 * MIX + 2 * N_HEADS
_LRU_COLS = 2 * MIX
_RW_COLS = 3 * MIX + RW_DECAY_RANK + RW_A_RANK + RW_GATE_RANK
C_ML = 0
C_DN = _ML_COLS
C_LRU = C_DN + _DN_COLS
C_RW = C_LRU + _LRU_COLS
C_RWX = C_RW + 3 * MIX
C_GATE = C_RW + _RW_COLS
N_PROJ = C_GATE + 4 * D_MODEL
RW_CODES = _RW_COLS - 3 * MIX
RWX_W = 0
RWX_A = RW_DECAY_RANK
RWX_G = RW_DECAY_RANK + RW_A_RANK

VMEM_LIMIT_BYTES = 56 * 1024 * 1024
LANE = 128


def _cparams(sem):
    return pltpu.CompilerParams(dimension_semantics=sem, vmem_limit_bytes=VMEM_LIMIT_BYTES)


def _rmsnorm_body(x_ref, g_ref, o_ref):
    x = x_ref[...]
    y = x * lax.rsqrt(jnp.mean(x * x, axis=-1, keepdims=True) + NORM_EPS)
    o_ref[...] = (y * g_ref[...]).astype(o_ref.dtype)


def _rmsnorm(x, g, out_dtype, tm):
    m, d = x.shape
    return pl.pallas_call(
        _rmsnorm_body,
        grid=(m // tm,),
        in_specs=[pl.BlockSpec((tm, d), lambda i: (i, 0)), pl.BlockSpec((1, d), lambda i: (0, 0))],
        out_specs=pl.BlockSpec((tm, d), lambda i: (i, 0)),
        out_shape=jax.ShapeDtypeStruct((m, d), out_dtype),
        compiler_params=_cparams(("parallel",)),
        name="rmsnorm",
    )(x, g.reshape(1, d))


def _mm_body(x_ref, w_ref, o_ref):
    o_ref[...] = jnp.dot(x_ref[...], w_ref[0].astype(BF16), preferred_element_type=F32).astype(o_ref.dtype)


def _matmul(x, w, layer, tm, tn, out_dtype=F32):
    m, k = x.shape
    n = w.shape[2]
    return pl.pallas_call(
        _mm_body,
        grid=(m // tm, pl.cdiv(n, tn)),
        in_specs=[pl.BlockSpec((tm, k), lambda i, j: (i, 0), pipeline_mode=pl.Buffered(1)),
                  pl.BlockSpec((1, k, tn), lambda i, j: (layer, 0, j))],
        out_specs=pl.BlockSpec((tm, tn), lambda i, j: (i, j)),
        out_shape=jax.ShapeDtypeStruct((m, n), out_dtype),
        compiler_params=_cparams(("parallel", "arbitrary")),
        name="matmul",
    )(x, w)


def _mm_res_body(x_ref, w_ref, r_ref, o_ref, acc_ref):
    kk = pl.program_id(2)

    @pl.when(kk == 0)
    def _():
        acc_ref[...] = r_ref[...]

    acc_ref[...] += jnp.dot(x_ref[...], w_ref[0], preferred_element_type=F32)

    @pl.when(kk == pl.num_programs(2) - 1)
    def _():
        o_ref[...] = acc_ref[...]


def _matmul_residual(x, w, layer, res, tm, tn, tk):
    m, k = x.shape
    n = w.shape[2]
    return pl.pallas_call(
        _mm_res_body,
        grid=(n // tn, m // tm, k // tk),
        in_specs=[pl.BlockSpec((tm, tk), lambda j, i, kk: (i, kk)),
                  pl.BlockSpec((1, tk, tn), lambda j, i, kk: (layer, kk, j)),
                  pl.BlockSpec((tm, tn), lambda j, i, kk: (i, j))],
        out_specs=pl.BlockSpec((tm, tn), lambda j, i, kk: (i, j)),
        out_shape=jax.ShapeDtypeStruct((m, n), F32),
        scratch_shapes=[pltpu.VMEM((tm, tn), F32)],
        compiler_params=_cparams(("parallel", "parallel", "arbitrary")),
        name="matmul_residual",
    )(x, w, res)


def _ffn_up_body(x_ref, wg_ref, wu_ref, o_ref):
    x = x_ref[...]
    g = jnp.dot(x, wg_ref[0].astype(BF16), preferred_element_type=F32)
    u = jnp.dot(x, wu_ref[0].astype(BF16), preferred_element_type=F32)
    o_ref[...] = (jax.nn.silu(g) * u).astype(o_ref.dtype)


def _ffn_up(x, wg, wu, layer, tm, tn):
    m, k = x.shape
    n = wg.shape[2]
    assert n % tn == 0
    wspec = pl.BlockSpec((1, k, tn), lambda i, j: (layer, 0, j))
    return pl.pallas_call(
        _ffn_up_body,
        grid=(m // tm, n // tn),
        in_specs=[pl.BlockSpec((tm, k), lambda i, j: (i, 0), pipeline_mode=pl.Buffered(1)), wspec, wspec],
        out_specs=pl.BlockSpec((tm, tn), lambda i, j: (i, j)),
        out_shape=jax.ShapeDtypeStruct((m, n), BF16),
        compiler_params=_cparams(("parallel", "arbitrary")),
        name="ffn_up",
    )(x, wg, wu)


def _merge_body(*refs, nb):
    g_refs, b_ref, w_ref, o_ref = refs[:nb], refs[nb], refs[nb + 1], refs[nb + 2]
    acc = jax.nn.sigmoid(g_refs[0][...]) * jnp.dot(b_ref[0], w_ref[0, 0], preferred_element_type=F32)
    for n in range(1, nb):
        acc = acc + jax.nn.sigmoid(g_refs[n][...]) * jnp.dot(b_ref[n], w_ref[0, n], preferred_element_type=F32)
    o_ref[...] = acc.astype(o_ref.dtype)


def _merge(proj, branches, wb, layer, tm, tn):
    nb, m, k = branches.shape
    n = wb.shape[3]

    def gate_spec(b):
        return pl.BlockSpec((pl.Element(tm), pl.Element(tn)),
                            lambda i, j: (i * tm, (C_GATE // LANE + b * (n // LANE) + j * (tn // LANE)) * LANE))

    return pl.pallas_call(
        functools.partial(_merge_body, nb=nb),
        grid=(m // tm, n // tn),
        in_specs=[gate_spec(b) for b in range(nb)]
                 + [pl.BlockSpec((nb, tm, k), lambda i, j: (0, i, 0)),
                    pl.BlockSpec((1, nb, k, tn), lambda i, j: (layer, 0, 0, j))],
        out_specs=pl.BlockSpec((tm, tn), lambda i, j: (i, j)),
        out_shape=jax.ShapeDtypeStruct((m, n), BF16),
        compiler_params=_cparams(("parallel", "arbitrary")),
        name="merge",
    )(*([proj] * nb), branches, wb)


G_MLSTM = 2
G_DELTA = 4
G_RWKV = 8
G_LRU = 8


def _group_size(n_seq, n_chunks, want):
    return math.gcd(n_seq, want) if n_chunks == 1 else 1


def _seg(L, n_chunks, G, col, width=MIX):
    start = col // LANE * LANE
    off = col - start
    win = -(-(off + width) // LANE) * LANE
    spec = pl.BlockSpec((pl.Element(G), pl.Element(L), pl.Element(win)),
                        lambda b, c: ((b * n_chunks + c) * G, 0, start))
    return spec, off


def _per_seq(G, tail):
    zeros = (0,) * len(tail)
    return pl.BlockSpec((G,) + tail, lambda b, c: (b,) + zeros)


def _per_chunk(G, n_chunks, tail):
    zeros = (0,) * len(tail)
    return pl.BlockSpec((G,) + tail, lambda b, c: (b * n_chunks + c,) + zeros)


def _state_in(state, tail, G):
    zeros = (0,) * len(tail)
    if isinstance(state, tuple):
        arr, layer = state
        return arr, pl.BlockSpec((None, G) + tail, lambda b, c: (layer, b) + zeros)
    return state, _per_seq(G, tail)


def _state_out(tail, n_seq, slab, G):
    zeros = (0,) * len(tail)
    if slab is None:
        return jax.ShapeDtypeStruct((n_seq,) + tail, F32), _per_seq(G, tail)
    layer, depth = slab
    return (jax.ShapeDtypeStruct((depth, n_seq) + tail, F32),
            pl.BlockSpec((None, G) + tail, lambda b, c: (layer, b) + zeros))


def _alias_prev(body, n_inputs, out_index, prev):
    if prev is None:
        return body, [], [], {}

    def aliased_body(*refs):
        return body(*refs[:n_inputs], *refs[n_inputs + 1:])

    return aliased_body, [prev], [pl.BlockSpec(memory_space=pl.ANY)], {n_inputs: out_index}


def _tri_masks(L):
    row = lax.broadcasted_iota(jnp.int32, (L, L), 0)
    col = lax.broadcasted_iota(jnp.int32, (L, L), 1)
    return col <= row, col < row


def _dot(a, b):
    return jnp.dot(a.astype(BF16), b.astype(BF16), preferred_element_type=F32)


def _dot_nt(a, b):
    return lax.dot_general(a.astype(BF16), b.astype(BF16), (((1,), (1,)), ((), ())), preferred_element_type=F32)


def _dot_tn(a, b):
    return lax.dot_general(a.astype(BF16), b.astype(BF16), (((0,), (0,)), ((), ())), preferred_element_type=F32)


def _dot_exact(a, b):
    return jnp.dot(a, b, precision=lax.Precision.HIGHEST, preferred_element_type=F32)


def _unit_lower_inverse(ns, L):
    size = ns[0].shape[0]
    eye = (lax.broadcasted_iota(jnp.int32, (size, size), 0)
           == lax.broadcasted_iota(jnp.int32, (size, size), 1)).astype(F32)
    ms = [-n for n in ns]
    ts = [eye + m for m in ms]
    span = 2
    while span < L:
        ms = [_dot(m, m) for m in ms]
        ts = [t + _dot(t, m) for t, m in zip(ts, ms)]
        span *= 2
    return ts


def _mlstm_body(q_ref, k_ref, v_ref, o_ref, gc_ref, gr_ref, bc_ref, br_ref, norm_ref, c0_ref, n0_ref, m0_ref,
                out_ref, cout_ref, nout_ref, mout_ref, c_s, n_s, m_s, *, L, G):
    c = pl.program_id(1)

    @pl.when(c == 0)
    def _():
        c_s[...] = c0_ref[...]
        n_s[...] = n0_ref[...]
        m_s[...] = m0_ref[...]

    incl, _ = _tri_masks(L)
    tril = incl.astype(F32)
    ig_c, ig_r, b_c, b_r = [], [], [], []
    for g in range(G):
        gc = gc_ref[g]
        gr = gr_ref[g]
        ig_c.append(gc[:, 0:N_HEADS] + bc_ref[:, 0:N_HEADS])
        ig_r.append(gr[0:N_HEADS, :] + br_ref[0:N_HEADS, :])
        lf_c = jax.nn.log_sigmoid(gc[:, N_HEADS:2 * N_HEADS] + bc_ref[:, N_HEADS:2 * N_HEADS])
        lf_r = jax.nn.log_sigmoid(gr[N_HEADS:2 * N_HEADS, :] + br_ref[N_HEADS:2 * N_HEADS, :])
        b_c.append(_dot_exact(tril, lf_c))
        b_r.append(_dot_exact(lf_r, tril.T))
    units = [(g, h) for g in range(G) for h in range(N_HEADS)]
    sl = lambda h: slice(h * HEAD_DIM, (h + 1) * HEAD_DIM)
    qs = [q_ref[g, :, sl(h)] * (HEAD_DIM ** -0.5) for g, h in units]
    ks = [k_ref[g, :, sl(h)] for g, h in units]
    vs = [v_ref[g, :, sl(h)] for g, h in units]
    c_old = [c_s[g, h] for g, h in units]
    n_old = [n_s[g, h:h + 1, :] for g, h in units]
    qk = [_dot_nt(q, k) for q, k in zip(qs, ks)]
    qc = [_dot(q, co) for q, co in zip(qs, c_old)]
    bcs = [b_c[g][:, h:h + 1] for g, h in units]
    m_prev = [m_s[g][:, h:h + 1] for g, h in units]
    log_d = [jnp.where(incl, bcs[u] - b_r[g][h:h + 1, :] + ig_r[g][h:h + 1, :], -jnp.inf)
             for u, (g, h) in enumerate(units)]
    state_w = [b + m for b, m in zip(bcs, m_prev)]
    m_t = [jnp.maximum(s, jnp.max(ld, axis=-1, keepdims=True)) for s, ld in zip(state_w, log_d)]
    scores = [a * jnp.exp(ld - mt) for a, ld, mt in zip(qk, log_d, m_t)]
    sw = [jnp.exp(s - mt) for s, mt in zip(state_w, m_t)]
    sv = [_dot(s, v) for s, v in zip(scores, vs)]
    m_last = [mt[L - 1:L, :] for mt in m_t]
    b_last = [b[L - 1:L, :] for b in bcs]
    kw = [ks[u] * jnp.exp(b_last[u] - bcs[u] + ig_c[g][:, h:h + 1] - m_last[u]) for u, (g, h) in enumerate(units)]
    kv = [_dot_tn(a, v) for a, v in zip(kw, vs)]
    for u, (g, h) in enumerate(units):
        decay = jnp.exp(b_last[u] + m_prev[u] - m_last[u])
        c_s[g, h] = decay * c_old[u] + kv[u]
        n_s[g, h:h + 1, :] = decay * n_old[u] + jnp.sum(kw[u], axis=0, keepdims=True)
        num = sv[u] + sw[u] * qc[u]
        den = jnp.sum(scores[u], axis=-1, keepdims=True) + sw[u] * jnp.sum(qs[u] * n_old[u], axis=-1, keepdims=True)
        hid = num / jnp.maximum(jnp.abs(den), jnp.exp(-m_t[u]))
        hn = hid * lax.rsqrt(jnp.mean(hid * hid, axis=-1, keepdims=True) + NORM_EPS) * norm_ref[:, sl(h)]
        out_ref[g, :, sl(h)] = (jax.nn.sigmoid(o_ref[g, :, sl(h)]) * hn).astype(out_ref.dtype)
    for g in range(G):
        m_s[g] = jnp.concatenate(m_last[g * N_HEADS:(g + 1) * N_HEADS], axis=1)

    @pl.when(c == pl.num_programs(1) - 1)
    def _():
        cout_ref[...] = c_s[...]
        nout_ref[...] = n_s[...]
        mout_ref[...] = m_s[...]


def _mlstm(proj3, small_c, small_r, blk0, n_seq, n_chunks, L, b_i, b_f, ml_norm, c0, n0, m0, slab=None, prev=None):
    assert C_ML % LANE == 0 and blk0 == 0
    G = _group_size(n_seq, n_chunks, G_MLSTM)
    bias_c = jnp.zeros((1, LANE), F32).at[0, 0:N_HEADS].set(b_i).at[0, N_HEADS:2 * N_HEADS].set(b_f)
    bias_r = jnp.zeros((32, 1), F32).at[0:N_HEADS, 0].set(b_i).at[N_HEADS:2 * N_HEADS, 0].set(b_f)

    def col(j):
        return _seg(L, n_chunks, G, C_ML + j * MIX)[0]

    full2 = lambda shape: pl.BlockSpec(shape, lambda b, c: (0, 0))
    c0, c0_spec = _state_in(c0, (N_HEADS, HEAD_DIM, HEAD_DIM), G)
    c_shape, c_spec = _state_out((N_HEADS, HEAD_DIM, HEAD_DIM), n_seq, slab, G)
    inputs = [proj3, proj3, proj3, proj3, small_c, small_r, bias_c, bias_r, ml_norm.reshape(1, MIX),
              c0, n0, m0.reshape(n_seq, 1, N_HEADS)]
    body, extra_in, extra_specs, aliases = _alias_prev(functools.partial(_mlstm_body, L=L, G=G), len(inputs), 1, prev)
    outs = pl.pallas_call(
        body,
        grid=(n_seq // G, n_chunks),
        in_specs=[col(0), col(1), col(2), col(3),
                  _per_chunk(G, n_chunks, (L, 32)), _per_chunk(G, n_chunks, (32, L)),
                  full2((1, LANE)), full2((32, 1)), full2((1, MIX)),
                  c0_spec, _per_seq(G, (N_HEADS, HEAD_DIM)), _per_seq(G, (1, N_HEADS))] + extra_specs,
        out_specs=[_per_chunk(G, n_chunks, (L, MIX)),
                   c_spec, _per_seq(G, (N_HEADS, HEAD_DIM)), _per_seq(G, (1, N_HEADS))],
        out_shape=[jax.ShapeDtypeStruct((n_seq * n_chunks, L, MIX), BF16),
                   c_shape,
                   jax.ShapeDtypeStruct((n_seq, N_HEADS, HEAD_DIM), F32),
                   jax.ShapeDtypeStruct((n_seq, 1, N_HEADS), F32)],
        scratch_shapes=[pltpu.VMEM((G, N_HEADS, HEAD_DIM, HEAD_DIM), F32),
                        pltpu.VMEM((G, N_HEADS, HEAD_DIM), F32),
                        pltpu.VMEM((G, 1, N_HEADS), F32)],
        input_output_aliases=aliases,
        compiler_params=_cparams(("parallel", "arbitrary")),
        name="mlstm",
    )(*inputs, *extra_in)
    out, c_new, n_new, m_new = outs
    return out, c_new, n_new, m_new.reshape(n_seq, N_HEADS)


HIST = 8


def _delta_body(q_ref, k_ref, v_ref, z_ref, gc_ref, gr_ref, bc_ref, br_ref, cw_ref, hist_ref, norm_ref, s0_ref,
                out_ref, sout_ref, cout_ref, s_s, ext_s, *, L, G, off):
    c = pl.program_id(1)

    @pl.when(c == 0)
    def _():
        s_s[...] = s0_ref[...]
        ext_s[:, 0:HIST, :] = hist_ref[...]

    incl, strict = _tri_masks(L)
    tril = incl.astype(F32)
    a_log_c, dt_c = bc_ref[:, 0:N_HEADS], bc_ref[:, N_HEADS:2 * N_HEADS]
    a_log_r, dt_r = br_ref[0:N_HEADS, :], br_ref[N_HEADS:2 * N_HEADS, :]
    base = HIST - (CONV_W - 1)
    qkv, z, beta_c, gcum_c, gcum_r = [], [], [], [], []
    for g in range(G):
        ext_s[g, HIST:HIST + L, 0:MIX] = q_ref[g, :, off:off + MIX]
        ext_s[g, HIST:HIST + L, MIX:2 * MIX] = k_ref[g, :, off:off + MIX]
        ext_s[g, HIST:HIST + L, 2 * MIX:3 * MIX] = v_ref[g, :, off:off + MIX]
        z.append(z_ref[g, :, off:off + MIX])
        y = ext_s[g, base:base + L, :] * cw_ref[0:1, :]
        for j in range(1, CONV_W):
            y = y + ext_s[g, base + j:base + j + L, :] * cw_ref[j:j + 1, :]
        ext_s[g, 0:HIST, :] = ext_s[g, L:L + HIST, :]
        qkv.append(jax.nn.silu(y))
        gc = gc_ref[g]
        gr = gr_ref[g]
        g_c = -jnp.exp(a_log_c) * jax.nn.softplus(gc[:, 2 * N_HEADS:3 * N_HEADS] + dt_c)
        g_r = -jnp.exp(a_log_r) * jax.nn.softplus(gr[2 * N_HEADS:3 * N_HEADS, :] + dt_r)
        beta_c.append(jax.nn.sigmoid(gc[:, 3 * N_HEADS:4 * N_HEADS]))
        gcum_c.append(_dot_exact(tril, g_c))
        gcum_r.append(_dot_exact(g_r, tril.T))
    units = [(g, h) for g in range(G) for h in range(N_HEADS)]
    nu = range(len(units))
    sl = lambda h: slice(h * HEAD_DIM, (h + 1) * HEAD_DIM)
    qs = [qkv[g][:, h * HEAD_DIM:(h + 1) * HEAD_DIM] for g, h in units]
    ks = [qkv[g][:, MIX + h * HEAD_DIM:MIX + (h + 1) * HEAD_DIM] for g, h in units]
    vs = [qkv[g][:, 2 * MIX + h * HEAD_DIM:2 * MIX + (h + 1) * HEAD_DIM] for g, h in units]
    qs = [x * lax.rsqrt(jnp.sum(x * x, axis=-1, keepdims=True) + NORM_EPS) * (HEAD_DIM ** -0.5) for x in qs]
    ks = [x * lax.rsqrt(jnp.sum(x * x, axis=-1, keepdims=True) + NORM_EPS) for x in ks]
    gch = [gcum_c[g][:, h:h + 1] for g, h in units]
    beta = [beta_c[g][:, h:h + 1] for g, h in units]
    eg = [jnp.exp(x) for x in gch]
    dec_incl = [jnp.exp(jnp.where(incl, gch[u] - gcum_r[g][h:h + 1, :], -jnp.inf)) for u, (g, h) in enumerate(units)]
    s_old = [s_s[g, h] for g, h in units]
    prod = [_dot_nt(jnp.concatenate([ks[u], qs[u]], axis=0), ks[u]) for u in nu]
    kk = [x[0:L] for x in prod]
    qk = [x[L:2 * L] for x in prod]
    a_mat = [jnp.where(strict, beta[u] * kk[u] * dec_incl[u], 0.0) for u in nu]
    t_inv = _unit_lower_inverse(a_mat, L)
    uw = [_dot(t_inv[u], jnp.concatenate([vs[u] * beta[u], ks[u] * (beta[u] * eg[u])], axis=1)) for u in nu]
    uu = [x[:, 0:HEAD_DIM] for x in uw]
    w = [x[:, HEAD_DIM:2 * HEAD_DIM] for x in uw]
    ws = [_dot(jnp.concatenate([w[u], qs[u] * eg[u]], axis=0), s_old[u]) for u in nu]
    w_s = [x[0:L] for x in ws]
    q_s = [x[L:2 * L] for x in ws]
    delta = [uu[u] - w_s[u] for u in nu]
    qkd = [_dot(qk[u] * dec_incl[u], delta[u]) for u in nu]
    g_last = [x[L - 1:L, :] for x in gch]
    kd = [_dot_tn(ks[u] * jnp.exp(g_last[u] - gch[u]), delta[u]) for u in nu]
    for u, (g, h) in enumerate(units):
        s_s[g, h] = jnp.exp(g_last[u]) * s_old[u] + kd[u]
        o = q_s[u] + qkd[u]
        on = o * lax.rsqrt(jnp.mean(o * o, axis=-1, keepdims=True) + NORM_EPS) * norm_ref[...]
        out_ref[g, :, sl(h)] = (on * jax.nn.silu(z[g][:, sl(h)])).astype(out_ref.dtype)

    @pl.when(c == pl.num_programs(1) - 1)
    def _():
        sout_ref[...] = s_s[...]
        cout_ref[...] = ext_s[:, HIST + L - (CONV_W - 1):HIST + L, :]


def _hist_rows(state_rows, n_seq, width):
    k = state_rows.shape[1]
    return jnp.concatenate([jnp.zeros((n_seq, HIST - k, width), F32), state_rows], axis=1)


def _delta(proj3, small_c, small_r, blk0, n_seq, n_chunks, L, conv_w, a_log, dt_bias, dn_norm, conv0, s0,
           slab=None, prev=None):
    bias_c = jnp.zeros((1, LANE), F32).at[0, 0:N_HEADS].set(a_log).at[0, N_HEADS:2 * N_HEADS].set(dt_bias)
    bias_r = jnp.zeros((32, 1), F32).at[0:N_HEADS, 0].set(a_log).at[N_HEADS:2 * N_HEADS, 0].set(dt_bias)
    assert blk0 == 0
    G = _group_size(n_seq, n_chunks, G_DELTA)
    hist = _hist_rows(conv0, n_seq, 3 * MIX)
    segs = [_seg(L, n_chunks, G, C_DN + j * MIX) for j in range(4)]
    off = segs[0][1]
    assert all(s[1] == off for s in segs)

    full2 = lambda shape: pl.BlockSpec(shape, lambda b, c: (0, 0))
    s0, s0_spec = _state_in(s0, (N_HEADS, HEAD_DIM, HEAD_DIM), G)
    s_shape, s_spec = _state_out((N_HEADS, HEAD_DIM, HEAD_DIM), n_seq, slab, G)
    inputs = [proj3, proj3, proj3, proj3, small_c, small_r, bias_c, bias_r, conv_w, hist,
              dn_norm.reshape(1, HEAD_DIM), s0]
    body, extra_in, extra_specs, aliases = _alias_prev(functools.partial(_delta_body, L=L, G=G, off=off),
                                                       len(inputs), 1, prev)
    out, s_new, conv_new = pl.pallas_call(
        body,
        grid=(n_seq // G, n_chunks),
        in_specs=[segs[0][0], segs[1][0], segs[2][0], segs[3][0],
                  _per_chunk(G, n_chunks, (L, 32)), _per_chunk(G, n_chunks, (32, L)),
                  full2((1, LANE)), full2((32, 1)), full2((CONV_W, 3 * MIX)),
                  _per_seq(G, (HIST, 3 * MIX)),
                  full2((1, HEAD_DIM)),
                  s0_spec] + extra_specs,
        out_specs=[_per_chunk(G, n_chunks, (L, MIX)),
                   s_spec,
                   _per_seq(G, (CONV_W - 1, 3 * MIX))],
        out_shape=[jax.ShapeDtypeStruct((n_seq * n_chunks, L, MIX), BF16),
                   s_shape,
                   jax.ShapeDtypeStruct((n_seq, CONV_W - 1, 3 * MIX), F32)],
        scratch_shapes=[pltpu.VMEM((G, N_HEADS, HEAD_DIM, HEAD_DIM), F32),
                        pltpu.VMEM((G, HIST + max(L, HIST), 3 * MIX), F32)],
        input_output_aliases=aliases,
        compiler_params=_cparams(("parallel", "arbitrary")),
        name="deltanet",
    )(*inputs, *extra_in)
    return out, s_new, conv_new


RWX_BLOCK = 768
RW_GATE_PAD = 512


def _rwkv_pairs(pre, rk_ref, lnw_ref, lnb_ref, out_ref, s_s, *, L, G):
    r, k2, v, a, gate, kkk, e_cum, e_prev, e_neg = pre
    pw = 2 * RW_HD
    lo = lax.broadcasted_iota(jnp.int32, (1, pw), 1) < RW_HD
    row2 = lax.broadcasted_iota(jnp.int32, (2 * L, 2 * L), 0)
    col2 = lax.broadcasted_iota(jnp.int32, (2 * L, 2 * L), 1)
    incl_rows = jnp.where(row2 >= L, 1, 0)
    t_idx = row2 - L * incl_rows
    s_idx = jnp.where(col2 >= L, col2 - L, col2)
    mask2 = s_idx < t_idx + incl_rows
    col_lo = lax.broadcasted_iota(jnp.int32, (1, 2 * L), 1) < L
    blk_r = jnp.where(lax.broadcasted_iota(jnp.int32, (pw, pw), 0) >= RW_HD, 1, 0)
    blk_c = jnp.where(lax.broadcasted_iota(jnp.int32, (pw, pw), 1) >= RW_HD, 1, 0)
    bd_state = blk_r == blk_c

    def split_rows(x, sel):
        return jnp.concatenate([jnp.where(sel, x, 0.0), jnp.where(sel, 0.0, x)], axis=0)

    def seg_sum(x):
        s_lo = jnp.sum(jnp.where(lo, x, 0.0), axis=-1, keepdims=True)
        s_hi = jnp.sum(jnp.where(lo, 0.0, x), axis=-1, keepdims=True)
        return jnp.where(lo, s_lo, s_hi)

    units = [(g, p) for g in range(G) for p in range(RW_HEADS // 2)]
    nu = range(len(units))
    sl = lambda p: slice(p * pw, (p + 1) * pw)
    kkn = [kkk[g][:, sl(p)] for g, p in units]
    kkn = [x * lax.rsqrt(seg_sum(x * x) + NORM_EPS) for x in kkn]
    rs = [r[g][:, sl(p)] for g, p in units]
    ks = [k2[g][:, sl(p)] for g, p in units]
    vs = [v[g][:, sl(p)] for g, p in units]
    k_til = [kkn[u] * e_prev[g][:, sl(p)] for u, (g, p) in enumerate(units)]
    r_til = [rs[u] * e_cum[g][:, sl(p)] for u, (g, p) in enumerate(units)]
    k_hat = [ks[u] * e_neg[g][:, sl(p)] for u, (g, p) in enumerate(units)]
    b_hat = [kkn[u] * a[g][:, sl(p)] * e_neg[g][:, sl(p)] for u, (g, p) in enumerate(units)]
    s_old = [s_s[g, p] for g, p in units]
    kr = [jnp.concatenate([k_til[u], r_til[u]], axis=0) for u in nu]
    p_k = [jnp.where(mask2, _dot_nt(kr[u], split_rows(k_hat[u], lo)), 0.0) for u in nu]
    p_b = [jnp.where(mask2, _dot_nt(kr[u], split_rows(b_hat[u], lo)), 0.0) for u in nu]
    t_bd = _unit_lower_inverse([split_rows(x[0:L], col_lo) for x in p_b], L)
    t_pair = [x[0:L] + x[L:2 * L] for x in t_bd]
    kv = [_dot(p_k[u], split_rows(vs[u], lo)) for u in nu]
    ksd = [_dot_nt(kr[u], s_old[u]) for u in nu]
    inner = [ksd[u][0:L] + kv[u][0:L] for u in nu]
    y_part = [ksd[u][L:2 * L] + kv[u][L:2 * L] for u in nu]
    uu = [_dot(t_pair[u], split_rows(inner[u], lo)) for u in nu]
    bu = [_dot(p_b[u][L:2 * L], split_rows(uu[u], lo)) for u in nu]
    el = [e_cum[g][L - 1:L, sl(p)] for g, p in units]
    upd = [_dot_tn(jnp.concatenate([vs[u], -uu[u]], axis=0),
                   jnp.concatenate([k_hat[u] * el[u], b_hat[u] * el[u]], axis=0)) for u in nu]
    for u, (g, p) in enumerate(units):
        s_s[g, p] = s_old[u] * el[u] + jnp.where(bd_state, upd[u], 0.0)
        y = y_part[u] - bu[u]
        mean = seg_sum(y) * (1.0 / RW_HD)
        d = y - mean
        var = seg_sum(d * d) * (1.0 / RW_HD)
        y = d * lax.rsqrt(var + RW_LN_EPS) * lnw_ref[:, sl(p)] + lnb_ref[:, sl(p)]
        y = y + seg_sum(rs[u] * ks[u] * rk_ref[:, sl(p)]) * vs[u]
        out_ref[g, :, sl(p)] = (y * gate[g][:, sl(p)]).astype(out_ref.dtype)


def _rwkv_body(r_ref, k_ref, v_ref, x_ref, hist_ref, mu_ref, w0_ref, w2_ref, a0_ref, a2_ref, g2_ref,
               kk_ref, ka_ref, rk_ref, lnw_ref, lnb_ref, s0_ref,
               out_ref, sout_ref, shout_ref, s_s, ext_s, *, L, G, off):
    c = pl.program_id(1)
    width = 3 * MIX + RWX_BLOCK

    @pl.when(c == 0)
    def _():
        s_s[...] = jnp.zeros(s_s.shape, F32)
        for h in range(RW_HEADS):
            o = (h % 2) * RW_HD
            s_s[:, h // 2, o:o + RW_HD, o:o + RW_HD] = s0_ref[:, h]
        ext_s[:, 0:HIST, :] = hist_ref[...]

    incl, _ = _tri_masks(L)
    r, k2, v, a, gate, kkk, e_cum, e_prev, e_neg = [], [], [], [], [], [], [], [], []
    for g in range(G):
        ext_s[g, HIST:HIST + L, 0:MIX] = r_ref[g, :, off:off + MIX]
        ext_s[g, HIST:HIST + L, MIX:2 * MIX] = k_ref[g, :, off:off + MIX]
        ext_s[g, HIST:HIST + L, 2 * MIX:3 * MIX] = v_ref[g, :, off:off + MIX]
        ext_s[g, HIST:HIST + L, 3 * MIX:3 * MIX + RW_CODES] = x_ref[g, :, off:off + RW_CODES]
        ext_s[g, HIST:HIST + L, 3 * MIX + RW_CODES:width] = jnp.zeros((L, width - 3 * MIX - RW_CODES), F32)
        cur = ext_s[g, HIST:HIST + L, :]
        prev = ext_s[g, HIST - 1:HIST - 1 + L, :]
        ext_s[g, 0:HIST, :] = ext_s[g, L:L + HIST, :]
        z = cur + (prev - cur) * mu_ref[...]
        k = z[:, MIX:2 * MIX]
        zw = z[:, 3 * MIX + RWX_W:3 * MIX + RWX_W + RW_DECAY_RANK]
        za = z[:, 3 * MIX + RWX_A:3 * MIX + RWX_A + RW_A_RANK]
        zg = z[:, 3 * MIX + RWX_G:3 * MIX + RWX_G + RW_GATE_PAD]
        w_log = -jax.nn.softplus(-(w0_ref[...] + _dot(jnp.tanh(zw), w2_ref[...]))) - 0.5
        a_g = jax.nn.sigmoid(a0_ref[...] + _dot(za, a2_ref[...]))
        log_decay = -jnp.exp(w_log)
        cum = _dot_exact(incl.astype(F32), log_decay)
        r.append(z[:, 0:MIX])
        v.append(z[:, 2 * MIX:3 * MIX])
        a.append(a_g)
        gate.append(_dot(jax.nn.sigmoid(zg), g2_ref[...]))
        kkk.append(k * kk_ref[...])
        k2.append(k * (1.0 + (a_g - 1.0) * ka_ref[...]))
        e_cum.append(jnp.exp(cum))
        e_prev.append(jnp.exp(cum - log_decay))
        e_neg.append(jnp.exp(-cum))
    pre = (r, k2, v, a, gate, kkk, e_cum, e_prev, e_neg)
    _rwkv_pairs(pre, rk_ref, lnw_ref, lnb_ref, out_ref, s_s, L=L, G=G)

    @pl.when(c == pl.num_programs(1) - 1)
    def _():
        for h in range(RW_HEADS):
            o = (h % 2) * RW_HD
            sout_ref[:, h] = s_s[:, h // 2, o:o + RW_HD, o:o + RW_HD]
        shout_ref[...] = ext_s[:, HIST + L - 1:HIST + L, 0:_RW_COLS]


def _rw_cols(vec):
    pad = jnp.zeros((RWX_BLOCK - (_RW_COLS - 3 * MIX),), F32)
    return jnp.concatenate([vec, pad]).reshape(1, 3 * MIX + RWX_BLOCK)


def _rwkv(proj3, blk0, n_seq, n_chunks, L, p, shift0, s0, slab=None, prev=None):
    width = 3 * MIX + RWX_BLOCK
    hist = jnp.concatenate([jnp.zeros((n_seq, HIST - 1, width), F32),
                            jnp.pad(shift0, ((0, 0), (0, width - _RW_COLS)))[:, None, :]], axis=1)
    g2 = jnp.pad(p['rw_g2'], ((0, RW_GATE_PAD - RW_GATE_RANK), (0, 0))).astype(BF16)
    row = lambda vec: vec.reshape(1, MIX)
    assert blk0 == 0
    G = _group_size(n_seq, n_chunks, G_RWKV)
    segs = [_seg(L, n_chunks, G, C_RW + j * MIX) for j in range(3)] + [_seg(L, n_chunks, G, C_RWX, RW_CODES)]
    off = segs[0][1]
    assert all(s[1] == off for s in segs)

    full2 = lambda shape: pl.BlockSpec(shape, lambda b, c: (0, 0))
    s0, s0_spec = _state_in(s0, (RW_HEADS, RW_HD, RW_HD), G)
    s_shape, s_spec = _state_out((RW_HEADS, RW_HD, RW_HD), n_seq, slab, G)
    inputs = [proj3, proj3, proj3, proj3, hist, _rw_cols(p['rw_mu']),
              row(p['rw_w0']), p['rw_w2'].astype(BF16), row(p['rw_a0']), p['rw_a2'].astype(BF16), g2,
              row(p['rw_k_k']), row(p['rw_k_a']), row(p['rw_r_k'].reshape(-1)), row(p['rw_ln_w']), row(p['rw_ln_b']),
              s0]
    body, extra_in, extra_specs, aliases = _alias_prev(functools.partial(_rwkv_body, L=L, G=G, off=off),
                                                       len(inputs), 1, prev)
    out, s_new, shift_new = pl.pallas_call(
        body,
        grid=(n_seq // G, n_chunks),
        in_specs=[segs[0][0], segs[1][0], segs[2][0], segs[3][0],
                  _per_seq(G, (HIST, width)),
                  full2((1, width)),
                  full2((1, MIX)), full2((RW_DECAY_RANK, MIX)),
                  full2((1, MIX)), full2((RW_A_RANK, MIX)),
                  full2((RW_GATE_PAD, MIX)),
                  full2((1, MIX)), full2((1, MIX)), full2((1, MIX)), full2((1, MIX)), full2((1, MIX)),
                  s0_spec] + extra_specs,
        out_specs=[_per_chunk(G, n_chunks, (L, MIX)),
                   s_spec,
                   _per_seq(G, (1, _RW_COLS))],
        out_shape=[jax.ShapeDtypeStruct((n_seq * n_chunks, L, MIX), BF16),
                   s_shape,
                   jax.ShapeDtypeStruct((n_seq, 1, _RW_COLS), F32)],
        scratch_shapes=[pltpu.VMEM((G, RW_HEADS // 2, 2 * RW_HD, 2 * RW_HD), F32),
                        pltpu.VMEM((G, HIST + max(L, HIST), width), F32)],
        input_output_aliases=aliases,
        compiler_params=_cparams(("parallel", "arbitrary")),
        name="rwkv7",
    )(*inputs, *extra_in)
    return out, s_new, shift_new.reshape(n_seq, _RW_COLS)


def _shift_rows(x, d, fill):
    rows = lax.broadcasted_iota(jnp.int32, x.shape, 0)
    return jnp.where(rows >= d, pltpu.roll(x, d, axis=0), fill)


def _lru_body(x_ref, g_ref, hist_ref, cw_ref, cb_ref, wa_ref, ba_ref, wx_ref, bx_ref, lam_ref, h0_ref,
              out_ref, hout_ref, cout_ref, h_s, ext_s, xc_s, *, L, G, off):
    c = pl.program_id(1)

    @pl.when(c == 0)
    def _():
        h_s[...] = h0_ref[...]
        ext_s[:, 0:HIST, :] = hist_ref[...]

    base = HIST - (CONV_W - 1)
    for g in range(G):
        ext_s[g, HIST:HIST + L, :] = x_ref[g, :, off:off + MIX]
        conv = ext_s[g, base:base + L, :] * cw_ref[0:1, :]
        for j in range(1, CONV_W):
            conv = conv + ext_s[g, base + j:base + j + L, :] * cw_ref[j:j + 1, :]
        ext_s[g, 0:HIST, :] = ext_s[g, L:L + HIST, :]
        xc_s[g * L:(g + 1) * L, :] = conv + cb_ref[...]
    xc = xc_s[...]
    r = jax.nn.sigmoid(_dot(xc, wa_ref[...]) + ba_ref[...])
    i = jax.nn.sigmoid(_dot(xc, wx_ref[...]) + bx_ref[...])
    log_a = -LRU_C * r * jax.nn.softplus(-lam_ref[...])
    a = jnp.exp(log_a)
    u_all = jnp.sqrt(-jnp.tanh(log_a) * (a * a + 1.0)) * (i * xc)
    for g in range(G):
        a_g = a[g * L:(g + 1) * L, :]
        u_g = u_all[g * L:(g + 1) * L, :]
        h_prev = h_s[g]
        if L % 8 == 0:
            acc_a, acc_h = a_g, u_g
            d = 1
            while d < L:
                acc_h = acc_a * _shift_rows(acc_h, d, 0.0) + acc_h
                acc_a = acc_a * _shift_rows(acc_a, d, 1.0)
                d *= 2
            hs = acc_a * h_prev + acc_h
            h_s[g] = hs[L - 1:L, :]
        else:
            rows = []
            for t in range(L):
                h_prev = a_g[t:t + 1, :] * h_prev + u_g[t:t + 1, :]
                rows.append(h_prev)
            hs = jnp.concatenate(rows, axis=0)
            h_s[g] = h_prev
        out_ref[g] = (hs * jax.nn.gelu(g_ref[g, :, off:off + MIX])).astype(out_ref.dtype)

    @pl.when(c == pl.num_programs(1) - 1)
    def _():
        hout_ref[...] = h_s[...]
        cout_ref[...] = ext_s[:, HIST + L - (CONV_W - 1):HIST + L, :]


def _block_diag(w):
    nb, bs, _ = w.shape
    eye = jnp.eye(nb, dtype=w.dtype)
    return (eye[:, None, :, None] * w[:, :, None, :]).reshape(nb * bs, nb * bs)


def _lru(proj3, blk0, n_seq, n_chunks, L, p, conv0, h0):
    hist = _hist_rows(conv0, n_seq, MIX)
    row = lambda vec: vec.reshape(1, MIX)
    full2 = lambda shape: pl.BlockSpec(shape, lambda b, c: (0, 0))
    assert blk0 == 0
    G = _group_size(n_seq, n_chunks, G_LRU)
    segs = [_seg(L, n_chunks, G, C_LRU + j * MIX) for j in range(2)]
    off = segs[0][1]
    assert all(s[1] == off for s in segs)
    out, h_new, conv_new = pl.pallas_call(
        functools.partial(_lru_body, L=L, G=G, off=off),
        grid=(n_seq // G, n_chunks),
        in_specs=[segs[0][0], segs[1][0],
                  _per_seq(G, (HIST, MIX)),
                  full2((CONV_W, MIX)), full2((1, MIX)),
                  full2((MIX, MIX)), full2((1, MIX)), full2((MIX, MIX)), full2((1, MIX)), full2((1, MIX)),
                  _per_seq(G, (1, MIX))],
        out_specs=[_per_chunk(G, n_chunks, (L, MIX)),
                   _per_seq(G, (1, MIX)),
                   _per_seq(G, (CONV_W - 1, MIX))],
        out_shape=[jax.ShapeDtypeStruct((n_seq * n_chunks, L, MIX), BF16),
                   jax.ShapeDtypeStruct((n_seq, 1, MIX), F32),
                   jax.ShapeDtypeStruct((n_seq, CONV_W - 1, MIX), F32)],
        scratch_shapes=[pltpu.VMEM((G, 1, MIX), F32),
                        pltpu.VMEM((G, HIST + max(L, HIST), MIX), F32),
                        pltpu.VMEM((G * L, MIX), F32)],
        compiler_params=_cparams(("parallel", "arbitrary")),
        name="rglru",
    )(proj3, proj3, hist, p['lru_conv_w'], row(p['lru_conv_b']),
      _block_diag(p['lru_wa']).astype(BF16), row(p['lru_ba']),
      _block_diag(p['lru_wx']).astype(BF16), row(p['lru_bx']), row(p['lru_lambda']),
      h0.reshape(n_seq, 1, MIX))
    return out, h_new.reshape(n_seq, MIX), conv_new


L_MLSTM = 256
L_DELTA = 64
L_RWKV = 64
L_LRU = 256


def _row_tile(m, cap):
    return max(t for t in range(16, cap + 1, 16) if m % t == 0)


def _chunk_len(t, want):
    return math.gcd(t, want)


def _small_views(small, L):
    m = small.shape[0]
    small_c = small.reshape(m // L, L, 32)
    small_r = jnp.swapaxes(small_c, 1, 2)
    return small_c, small_r


def _mixers(proj, n_seq, t_len, p, st, slab, prev):
    m = proj.shape[0]
    ml_c, ml_n, ml_m, dn_s, dn_conv, lru_h, lru_conv, rw_s, rw_shift = st
    prev_c, prev_dn, prev_rw = prev if prev is not None else (None, None, None)
    small = jnp.concatenate([proj[:, C_ML + 4 * MIX:C_ML + 4 * MIX + 2 * N_HEADS],
                             proj[:, C_DN + 4 * MIX:C_DN + 4 * MIX + 2 * N_HEADS]], axis=1)

    def view(want):
        L = _chunk_len(t_len, want)
        assert m % L == 0
        return proj.reshape(m // L, L, N_PROJ), L, t_len // L

    p3, L, nch = view(L_MLSTM)
    sc, sr = _small_views(small, L)
    out_ml, ml_c, ml_n, ml_m = _mlstm(p3, sc, sr, 0, n_seq, nch, L, p['ml_b_i'], p['ml_b_f'], p['ml_norm'],
                                      ml_c, ml_n, ml_m, slab, prev_c)
    p3, L, nch = view(L_DELTA)
    sc, sr = _small_views(small, L)
    out_dn, dn_s, dn_conv = _delta(p3, sc, sr, 0, n_seq, nch, L, p['dn_conv'], p['dn_A_log'], p['dn_dt_bias'],
                                   p['dn_norm'], dn_conv, dn_s, slab, prev_dn)
    p3, L, nch = view(L_LRU)
    out_lru, lru_h, lru_conv = _lru(p3, 0, n_seq, nch, L, p, lru_conv, lru_h)
    p3, L, nch = view(L_RWKV)
    out_rw, rw_s, rw_shift = _rwkv(p3, 0, n_seq, nch, L, p, rw_shift, rw_s, slab, prev_rw)
    outs = tuple(o.reshape(n_seq * t_len, MIX) for o in (out_ml, out_dn, out_lru, out_rw))
    return outs, (ml_c, ml_n, ml_m, dn_s, dn_conv, lru_h, lru_conv, rw_s, rw_shift)


_STATE_SHAPES = ((N_HEADS, HEAD_DIM, HEAD_DIM), (N_HEADS, HEAD_DIM), (N_HEADS,), (N_HEADS, HEAD_DIM, HEAD_DIM),
                 (CONV_W - 1, 3 * MIX), (MIX,), (CONV_W - 1, MIX), (RW_HEADS, RW_HD, RW_HD), (_RW_COLS,))
_BIG_STATES = (0, 3, 7)


def kernel(x_prompt, x_sample, state_mlstm_C, state_mlstm_n, state_mlstm_m, state_delta_S, state_delta_conv, state_rglru_h, state_rglru_conv, state_rwkv_S, state_rwkv_shift, norm_mix, w_in, ml_b_i, ml_b_f, ml_norm, dn_conv, dn_A_log, dn_dt_bias, dn_norm, lru_conv_w, lru_conv_b, lru_wa, lru_ba, lru_wx, lru_bx, lru_lambda, rw_mu, rw_w0, rw_w2, rw_a0, rw_a2, rw_g2, rw_k_k, rw_k_a, rw_r_k, rw_ln_w, rw_ln_b, w_branch, w_out, norm_ffn, w_ffn_gate, w_ffn_up, w_ffn_down, norm_final):
    params = {
        'ml_b_i': ml_b_i, 'ml_b_f': ml_b_f, 'ml_norm': ml_norm,
        'dn_conv': dn_conv, 'dn_A_log': dn_A_log, 'dn_dt_bias': dn_dt_bias, 'dn_norm': dn_norm,
        'lru_conv_w': lru_conv_w, 'lru_conv_b': lru_conv_b, 'lru_wa': lru_wa, 'lru_ba': lru_ba,
        'lru_wx': lru_wx, 'lru_bx': lru_bx, 'lru_lambda': lru_lambda,
        'rw_mu': rw_mu, 'rw_w0': rw_w0, 'rw_w2': rw_w2, 'rw_a0': rw_a0, 'rw_a2': rw_a2, 'rw_g2': rw_g2,
        'rw_k_k': rw_k_k, 'rw_k_a': rw_k_a, 'rw_r_k': rw_r_k, 'rw_ln_w': rw_ln_w, 'rw_ln_b': rw_ln_b,
    }
    depth = w_in.shape[0]
    bp, tp, d = x_prompt.shape
    bs, ts, _ = x_sample.shape
    n_p, n_s = bp * tp, bs * ts
    m = n_p + n_s
    x = jnp.concatenate([x_prompt.reshape(n_p, d), x_sample.reshape(n_s, d)], axis=0)
    tm = _row_tile(m, 1088)
    tm_panel = _row_tile(m, 2176)
    tm_norm = _row_tile(m, 544)
    sample_states = (state_mlstm_C, state_mlstm_n, state_mlstm_m, state_delta_S, state_delta_conv,
                     state_rglru_h, state_rglru_conv, state_rwkv_S, state_rwkv_shift)
    big_p = big_s = None
    new_p = [[] for _ in _STATE_SHAPES]
    new_s = [[] for _ in _STATE_SHAPES]
    wd_b = w_ffn_down.astype(BF16)
    w_out_b = w_out.astype(BF16)
    w_branch_b = w_branch.astype(BF16)
    for l in range(depth):
        p = {name: w[l] for name, w in params.items()}
        h = _rmsnorm(x, norm_mix[l], BF16, tm_norm)
        proj = _matmul(h, w_in, l, tm_panel, 512)
        st_p = tuple(jnp.zeros((bp,) + shp, F32) for shp in _STATE_SHAPES)
        st_s = tuple((s, l) if s.ndim == 5 else s[l] for s in sample_states)
        outs_p, st_p = _mixers(proj, bp, tp, p, st_p, (l, depth), big_p)
        outs_s, st_s = _mixers(proj[n_p:], bs, ts, p, st_s, (l, depth), big_s)
        big_p = tuple(st_p[i] for i in _BIG_STATES)
        big_s = tuple(st_s[i] for i in _BIG_STATES)
        for lst, s in zip(new_p, st_p):
            lst.append(s)
        for lst, s in zip(new_s, st_s):
            lst.append(s)
        branches = jnp.stack([jnp.concatenate([a, b], axis=0) for a, b in zip(outs_p, outs_s)])
        mix = _merge(proj, branches, w_branch_b, l, tm, 512)
        x = _matmul_residual(mix, w_out_b, l, x, tm, 1024, 2048)
        h2 = _rmsnorm(x, norm_ffn[l], BF16, tm_norm)
        act = _ffn_up(h2, w_ffn_gate, w_ffn_up, l, tm_panel, 256)
        x = _matmul_residual(act, wd_b, l, x, tm // 2, 512, D_FF)
    y = _rmsnorm(x, norm_final, F32, tm_norm)
    y_prompt = y[:n_p].reshape(bp, tp, d)
    y_sample = y[n_p:].reshape(bs, ts, d)
    def stacked(per_layer):
        return tuple(lst[-1] if i in _BIG_STATES else jnp.stack(lst) for i, lst in enumerate(per_layer))

    return (y_prompt, y_sample) + stacked(new_p) + stacked(new_s)
```

```python
import functools
import math

import jax
import jax.numpy as jnp
from jax import lax
from jax.experimental import pallas as pl
from jax.experimental.pallas import tpu as pltpu

F32 = jnp.float32
BF16 = jnp.bfloat16

D_MODEL = 4096
MIX = D_MODEL // 4
HEAD_DIM = 128
N_HEADS = MIX // HEAD_DIM
RW_HD = 64
RW_HEADS = MIX // RW_HD
RW_DECAY_RANK = 128
RW_A_RANK = 128
RW_GATE_RANK = 480
CONV_W = 4
LRU_C = 8.0
D_FF = -(-(8 * D_MODEL) // (3 * 256)) * 256
NORM_EPS = 1e-6
RW_LN_EPS = 64e-5

_ML_COLS = 4 * MIX + 2 * N_HEADS
_DN_COLS = 4 * MIX + 2 * N_HEADS
_LRU_COLS = 2 * MIX
_RW_COLS = 3 * MIX + RW_DECAY_RANK + RW_A_RANK + RW_GATE_RANK
C_ML = 0
C_DN = _ML_COLS
C_LRU = C_DN + _DN_COLS
C_RW = C_LRU + _LRU_COLS
C_RWX = C_RW + 3 * MIX
C_GATE = C_RW + _RW_COLS
N_PROJ = C_GATE + 4 * D_MODEL
RW_CODES = _RW_COLS - 3 * MIX
RWX_W = 0
RWX_A = RW_DECAY_RANK
RWX_G = RW_DECAY_RANK + RW_A_RANK

VMEM_LIMIT_BYTES = 56 * 1024 * 1024
LANE = 128


def _cparams(sem):
    return pltpu.CompilerParams(dimension_semantics=sem, vmem_limit_bytes=VMEM_LIMIT_BYTES)


def _rmsnorm_body(x_ref, g_ref, o_ref):
    x = x_ref[...]
    y = x * lax.rsqrt(jnp.mean(x * x, axis=-1, keepdims=True) + NORM_EPS)
    o_ref[...] = (y * g_ref[...]).astype(o_ref.dtype)


def _rmsnorm(x, g, out_dtype, tm):
    m, d = x.shape
    return pl.pallas_call(
        _rmsnorm_body,
        grid=(m // tm,),
        in_specs=[pl.BlockSpec((tm, d), lambda i: (i, 0)), pl.BlockSpec((1, d), lambda i: (0, 0))],
        out_specs=pl.BlockSpec((tm, d), lambda i: (i, 0)),
        out_shape=jax.ShapeDtypeStruct((m, d), out_dtype),
        compiler_params=_cparams(("parallel",)),
        name="rmsnorm",
    )(x, g.reshape(1, d))


def _rmsnorm_split_body(x_ref, g_ref, oa_ref, ob_ref, *, tiles_a):
    x = x_ref[...]
    y = x * lax.rsqrt(jnp.mean(x * x, axis=-1, keepdims=True) + NORM_EPS) * g_ref[...]
    i = pl.program_id(0)

    @pl.when(i < tiles_a)
    def _():
        oa_ref[...] = y

    @pl.when(i >= tiles_a)
    def _():
        ob_ref[...] = y


def _rmsnorm_split(x, g, rows_a, tm):
    m, d = x.shape
    rows_b = m - rows_a
    assert rows_a % tm == 0 and rows_b % tm == 0
    tiles_a = rows_a // tm
    return pl.pallas_call(
        functools.partial(_rmsnorm_split_body, tiles_a=tiles_a),
        grid=(m // tm,),
        in_specs=[pl.BlockSpec((tm, d), lambda i: (i, 0)), pl.BlockSpec((1, d), lambda i: (0, 0))],
        out_specs=[pl.BlockSpec((tm, d), lambda i: (jnp.minimum(i, tiles_a - 1), 0)),
                   pl.BlockSpec((tm, d), lambda i: (jnp.maximum(i - tiles_a, 0), 0))],
        out_shape=[jax.ShapeDtypeStruct((rows_a, d), F32), jax.ShapeDtypeStruct((rows_b, d), F32)],
        compiler_params=_cparams(("arbitrary",)),
        name="rmsnorm_split",
    )(x, g.reshape(1, d))


def _mm_body(x_ref, w_ref, o_ref):
    o_ref[...] = jnp.dot(x_ref[...], w_ref[0].astype(BF16), preferred_element_type=F32).astype(o_ref.dtype)


def _matmul(x, w, layer, tm, tn, out_dtype=F32):
    m, k = x.shape
    n = w.shape[2]
    return pl.pallas_call(
        _mm_body,
        grid=(m // tm, pl.cdiv(n, tn)),
        in_specs=[pl.BlockSpec((tm, k), lambda i, j: (i, 0), pipeline_mode=pl.Buffered(1)),
                  pl.BlockSpec((1, k, tn), lambda i, j: (layer, 0, j))],
        out_specs=pl.BlockSpec((tm, tn), lambda i, j: (i, j)),
        out_shape=jax.ShapeDtypeStruct((m, n), out_dtype),
        compiler_params=_cparams(("parallel", "arbitrary")),
        name="matmul",
    )(x, w)


def _mm_res_body(x_ref, w_ref, r_ref, o_ref, acc_ref):
    kk = pl.program_id(2)

    @pl.when(kk == 0)
    def _():
        acc_ref[...] = r_ref[...]

    acc_ref[...] += jnp.dot(x_ref[...], w_ref[0], preferred_element_type=F32)

    @pl.when(kk == pl.num_programs(2) - 1)
    def _():
        o_ref[...] = acc_ref[...]


def _matmul_residual(x, w, layer, res, tm, tn, tk):
    m, k = x.shape
    n = w.shape[2]
    return pl.pallas_call(
        _mm_res_body,
        grid=(n // tn, m // tm, k // tk),
        in_specs=[pl.BlockSpec((tm, tk), lambda j, i, kk: (i, kk)),
                  pl.BlockSpec((1, tk, tn), lambda j, i, kk: (layer, kk, j)),
                  pl.BlockSpec((tm, tn), lambda j, i, kk: (i, j))],
        out_specs=pl.BlockSpec((tm, tn), lambda j, i, kk: (i, j)),
        out_shape=jax.ShapeDtypeStruct((m, n), F32),
        scratch_shapes=[pltpu.VMEM((tm, tn), F32)],
        compiler_params=_cparams(("parallel", "parallel", "arbitrary")),
        name="matmul_residual",
    )(x, w, res)


def _ffn_up_body(x_ref, wg_ref, wu_ref, o_ref):
    x = x_ref[...]
    g = jnp.dot(x, wg_ref[0].astype(BF16), preferred_element_type=F32)
    u = jnp.dot(x, wu_ref[0].astype(BF16), preferred_element_type=F32)
    o_ref[...] = (jax.nn.silu(g) * u).astype(o_ref.dtype)


def _ffn_up(x, wg, wu, layer, tm, tn):
    m, k = x.shape
    n = wg.shape[2]
    assert n % tn == 0
    wspec = pl.BlockSpec((1, k, tn), lambda i, j: (layer, 0, j))
    return pl.pallas_call(
        _ffn_up_body,
        grid=(m // tm, n // tn),
        in_specs=[pl.BlockSpec((tm, k), lambda i, j: (i, 0), pipeline_mode=pl.Buffered(1)), wspec, wspec],
        out_specs=pl.BlockSpec((tm, tn), lambda i, j: (i, j)),
        out_shape=jax.ShapeDtypeStruct((m, n), BF16),
        compiler_params=_cparams(("parallel", "arbitrary")),
        name="ffn_up",
    )(x, wg, wu)


def _merge_body(*refs, nb, tiles_a):
    g_refs, a_refs, b_refs = refs[:nb], refs[nb:2 * nb], refs[2 * nb:3 * nb]
    w_ref, o_ref = refs[3 * nb], refs[3 * nb + 1]

    def mix(x_refs):
        acc = jax.nn.sigmoid(g_refs[0][...]) * jnp.dot(x_refs[0][...], w_ref[0, 0], preferred_element_type=F32)
        for n in range(1, nb):
            acc = acc + jax.nn.sigmoid(g_refs[n][...]) * jnp.dot(x_refs[n][...], w_ref[0, n],
                                                                 preferred_element_type=F32)
        o_ref[...] = acc.astype(o_ref.dtype)

    i = pl.program_id(1)
    pl.when(i < tiles_a)(lambda: mix(a_refs))
    pl.when(i >= tiles_a)(lambda: mix(b_refs))


def _merge(proj, branches_a, branches_b, wb, layer, tm, tn):
    nb = len(branches_a)
    rows_a, k = branches_a[0].shape
    rows_b = branches_b[0].shape[0]
    assert rows_a % tm == 0 and rows_b % tm == 0
    tiles_a, tiles_b = rows_a // tm, rows_b // tm
    n = wb.shape[3]

    def gate_spec(b):
        return pl.BlockSpec((pl.Element(tm), pl.Element(tn)),
                            lambda j, i: (i * tm, (C_GATE // LANE + b * (n // LANE) + j * (tn // LANE)) * LANE))

    a_spec = pl.BlockSpec((tm, k), lambda j, i: (jnp.minimum(i, tiles_a - 1), 0))
    b_spec = pl.BlockSpec((tm, k), lambda j, i: (jnp.maximum(i - tiles_a, 0), 0))
    return pl.pallas_call(
        functools.partial(_merge_body, nb=nb, tiles_a=tiles_a),
        grid=(n // tn, tiles_a + tiles_b),
        in_specs=[gate_spec(b) for b in range(nb)] + [a_spec] * nb + [b_spec] * nb
                 + [pl.BlockSpec((1, nb, k, tn), lambda j, i: (layer, 0, 0, j))],
        out_specs=pl.BlockSpec((tm, tn), lambda j, i: (i, j)),
        out_shape=jax.ShapeDtypeStruct((rows_a + rows_b, n), BF16),
        compiler_params=_cparams(("parallel", "arbitrary")),
        name="merge",
    )(*([proj] * nb), *branches_a, *branches_b, wb)


G_MLSTM = 2
G_DELTA = 4
G_RWKV = 8
G_LRU = 8


def _group_size(n_seq, n_chunks, want):
    return math.gcd(n_seq, want) if n_chunks == 1 else 1


def _seg(L, n_chunks, G, col, width=MIX):
    start = col // LANE * LANE
    off = col - start
    win = -(-(off + width) // LANE) * LANE
    spec = pl.BlockSpec((pl.Element(G), pl.Element(L), pl.Element(win)),
                        lambda b, c: ((b * n_chunks + c) * G, 0, start))
    return spec, off


def _per_seq(G, tail):
    zeros = (0,) * len(tail)
    return pl.BlockSpec((G,) + tail, lambda b, c: (b,) + zeros)


def _per_chunk(G, n_chunks, tail):
    zeros = (0,) * len(tail)
    return pl.BlockSpec((G,) + tail, lambda b, c: (b * n_chunks + c,) + zeros)


def _state_in(state, tail, G):
    zeros = (0,) * len(tail)
    if isinstance(state, tuple):
        arr, layer = state
        return arr, pl.BlockSpec((None, G) + tail, lambda b, c: (layer, b) + zeros)
    return state, _per_seq(G, tail)


def _state_out(tail, n_seq, slab, G):
    zeros = (0,) * len(tail)
    if slab is None:
        return jax.ShapeDtypeStruct((n_seq,) + tail, F32), _per_seq(G, tail)
    layer, depth = slab
    return (jax.ShapeDtypeStruct((depth, n_seq) + tail, F32),
            pl.BlockSpec((None, G) + tail, lambda b, c: (layer, b) + zeros))


def _alias_prev(body, n_inputs, out_index, prev):
    if prev is None:
        return body, [], [], {}

    def aliased_body(*refs):
        return body(*refs[:n_inputs], *refs[n_inputs + 1:])

    return aliased_body, [prev], [pl.BlockSpec(memory_space=pl.ANY)], {n_inputs: out_index}


def _tri_masks(L):
    row = lax.broadcasted_iota(jnp.int32, (L, L), 0)
    col = lax.broadcasted_iota(jnp.int32, (L, L), 1)
    return col <= row, col < row


def _dot(a, b):
    return jnp.dot(a.astype(BF16), b.astype(BF16), preferred_element_type=F32)


def _dot_nt(a, b):
    return lax.dot_general(a.astype(BF16), b.astype(BF16), (((1,), (1,)), ((), ())), preferred_element_type=F32)


def _dot_tn(a, b):
    return lax.dot_general(a.astype(BF16), b.astype(BF16), (((0,), (0,)), ((), ())), preferred_element_type=F32)


def _dot_exact(a, b):
    return jnp.dot(a, b, precision=lax.Precision.HIGHEST, preferred_element_type=F32)


def _unit_lower_inverse(ns, L):
    size = ns[0].shape[0]
    eye = (lax.broadcasted_iota(jnp.int32, (size, size), 0)
           == lax.broadcasted_iota(jnp.int32, (size, size), 1)).astype(F32)
    ms = [-n for n in ns]
    ts = [eye + m for m in ms]
    span = 2
    while span < L:
        ms = [_dot(m, m) for m in ms]
        ts = [t + _dot(t, m) for t, m in zip(ts, ms)]
        span *= 2
    return ts


def _mlstm_body(q_ref, k_ref, v_ref, o_ref, gc_ref, gr_ref, bc_ref, br_ref, norm_ref, c0_ref, n0_ref, m0_ref,
                out_ref, cout_ref, nout_ref, mout_ref, c_s, n_s, m_s, *, L, G):
    c = pl.program_id(1)

    @pl.when(c == 0)
    def _():
        c_s[...] = c0_ref[...]
        n_s[...] = n0_ref[...]
        m_s[...] = m0_ref[...]

    incl, _ = _tri_masks(L)
    tril = incl.astype(F32)
    ig_c, ig_r, b_c, b_r = [], [], [], []
    for g in range(G):
        gc = gc_ref[g]
        gr = gr_ref[g]
        ig_c.append(gc[:, 0:N_HEADS] + bc_ref[:, 0:N_HEADS])
        ig_r.append(gr[0:N_HEADS, :] + br_ref[0:N_HEADS, :])
        lf_c = jax.nn.log_sigmoid(gc[:, N_HEADS:2 * N_HEADS] + bc_ref[:, N_HEADS:2 * N_HEADS])
        lf_r = jax.nn.log_sigmoid(gr[N_HEADS:2 * N_HEADS, :] + br_ref[N_HEADS:2 * N_HEADS, :])
        b_c.append(_dot_exact(tril, lf_c))
        b_r.append(_dot_exact(lf_r, tril.T))
    units = [(g, h) for g in range(G) for h in range(N_HEADS)]
    sl = lambda h: slice(h * HEAD_DIM, (h + 1) * HEAD_DIM)
    qs = [q_ref[g, :, sl(h)] * (HEAD_DIM ** -0.5) for g, h in units]
    ks = [k_ref[g, :, sl(h)] for g, h in units]
    vs = [v_ref[g, :, sl(h)] for g, h in units]
    c_old = [c_s[g, h] for g, h in units]
    n_old = [n_s[g, h:h + 1, :] for g, h in units]
    qk = [_dot_nt(q, k) for q, k in zip(qs, ks)]
    qc = [_dot(q, co) for q, co in zip(qs, c_old)]
    bcs = [b_c[g][:, h:h + 1] for g, h in units]
    m_prev = [m_s[g][:, h:h + 1] for g, h in units]
    log_d = [jnp.where(incl, bcs[u] - b_r[g][h:h + 1, :] + ig_r[g][h:h + 1, :], -jnp.inf)
             for u, (g, h) in enumerate(units)]
    state_w = [b + m for b, m in zip(bcs, m_prev)]
    m_t = [jnp.maximum(s, jnp.max(ld, axis=-1, keepdims=True)) for s, ld in zip(state_w, log_d)]
    scores = [a * jnp.exp(ld - mt) for a, ld, mt in zip(qk, log_d, m_t)]
    sw = [jnp.exp(s - mt) for s, mt in zip(state_w, m_t)]
    sv = [_dot(s, v) for s, v in zip(scores, vs)]
    m_last = [mt[L - 1:L, :] for mt in m_t]
    b_last = [b[L - 1:L, :] for b in bcs]
    kw = [ks[u] * jnp.exp(b_last[u] - bcs[u] + ig_c[g][:, h:h + 1] - m_last[u]) for u, (g, h) in enumerate(units)]
    kv = [_dot_tn(a, v) for a, v in zip(kw, vs)]
    for u, (g, h) in enumerate(units):
        decay = jnp.exp(b_last[u] + m_prev[u] - m_last[u])
        c_s[g, h] = decay * c_old[u] + kv[u]
        n_s[g, h:h + 1, :] = decay * n_old[u] + jnp.sum(kw[u], axis=0, keepdims=True)
        num = sv[u] + sw[u] * qc[u]
        den = jnp.sum(scores[u], axis=-1, keepdims=True) + sw[u] * jnp.sum(qs[u] * n_old[u], axis=-1, keepdims=True)
        hid = num / jnp.maximum(jnp.abs(den), jnp.exp(-m_t[u]))
        hn = hid * lax.rsqrt(jnp.mean(hid * hid, axis=-1, keepdims=True) + NORM_EPS) * norm_ref[:, sl(h)]
        out_ref[g, :, sl(h)] = (jax.nn.sigmoid(o_ref[g, :, sl(h)]) * hn).astype(out_ref.dtype)
    for g in range(G):
        m_s[g] = jnp.concatenate(m_last[g * N_HEADS:(g + 1) * N_HEADS], axis=1)

    @pl.when(c == pl.num_programs(1) - 1)
    def _():
        cout_ref[...] = c_s[...]
        nout_ref[...] = n_s[...]
        mout_ref[...] = m_s[...]


def _mlstm(proj3, small_c, small_r, blk0, n_seq, n_chunks, L, b_i, b_f, ml_norm, c0, n0, m0, slab=None, prev=None):
    assert C_ML % LANE == 0 and blk0 == 0
    G = _group_size(n_seq, n_chunks, G_MLSTM)
    bias_c = jnp.zeros((1, LANE), F32).at[0, 0:N_HEADS].set(b_i).at[0, N_HEADS:2 * N_HEADS].set(b_f)
    bias_r = jnp.zeros((32, 1), F32).at[0:N_HEADS, 0].set(b_i).at[N_HEADS:2 * N_HEADS, 0].set(b_f)

    def col(j):
        return _seg(L, n_chunks, G, C_ML + j * MIX)[0]

    full2 = lambda shape: pl.BlockSpec(shape, lambda b, c: (0, 0))
    c0, c0_spec = _state_in(c0, (N_HEADS, HEAD_DIM, HEAD_DIM), G)
    c_shape, c_spec = _state_out((N_HEADS, HEAD_DIM, HEAD_DIM), n_seq, slab, G)
    inputs = [proj3, proj3, proj3, proj3, small_c, small_r, bias_c, bias_r, ml_norm.reshape(1, MIX),
              c0, n0, m0.reshape(n_seq, 1, N_HEADS)]
    body, extra_in, extra_specs, aliases = _alias_prev(functools.partial(_mlstm_body, L=L, G=G), len(inputs), 1, prev)
    outs = pl.pallas_call(
        body,
        grid=(n_seq // G, n_chunks),
        in_specs=[col(0), col(1), col(2), col(3),
                  _per_chunk(G, n_chunks, (L, 32)), _per_chunk(G, n_chunks, (32, L)),
                  full2((1, LANE)), full2((32, 1)), full2((1, MIX)),
                  c0_spec, _per_seq(G, (N_HEADS, HEAD_DIM)), _per_seq(G, (1, N_HEADS))] + extra_specs,
        out_specs=[_per_chunk(G, n_chunks, (L, MIX)),
                   c_spec, _per_seq(G, (N_HEADS, HEAD_DIM)), _per_seq(G, (1, N_HEADS))],
        out_shape=[jax.ShapeDtypeStruct((n_seq * n_chunks, L, MIX), BF16),
                   c_shape,
                   jax.ShapeDtypeStruct((n_seq, N_HEADS, HEAD_DIM), F32),
                   jax.ShapeDtypeStruct((n_seq, 1, N_HEADS), F32)],
        scratch_shapes=[pltpu.VMEM((G, N_HEADS, HEAD_DIM, HEAD_DIM), F32),
                        pltpu.VMEM((G, N_HEADS, HEAD_DIM), F32),
                        pltpu.VMEM((G, 1, N_HEADS), F32)],
        input_output_aliases=aliases,
        compiler_params=_cparams(("parallel", "arbitrary")),
        name="mlstm",
    )(*inputs, *extra_in)
    out, c_new, n_new, m_new = outs
    return out, c_new, n_new, m_new.reshape(n_seq, N_HEADS)


HIST = 8


def _delta_body(q_ref, k_ref, v_ref, z_ref, gc_ref, gr_ref, bc_ref, br_ref, cw_ref, hist_ref, norm_ref, s0_ref,
                out_ref, sout_ref, cout_ref, s_s, ext_s, *, L, G, off):
    c = pl.program_id(1)

    @pl.when(c == 0)
    def _():
        s_s[...] = s0_ref[...]
        ext_s[:, 0:HIST, :] = hist_ref[...]

    incl, strict = _tri_masks(L)
    tril = incl.astype(F32)
    a_log_c, dt_c = bc_ref[:, 0:N_HEADS], bc_ref[:, N_HEADS:2 * N_HEADS]
    a_log_r, dt_r = br_ref[0:N_HEADS, :], br_ref[N_HEADS:2 * N_HEADS, :]
    base = HIST - (CONV_W - 1)
    qkv, z, beta_c, gcum_c, gcum_r = [], [], [], [], []
    for g in range(G):
        ext_s[g, HIST:HIST + L, 0:MIX] = q_ref[g, :, off:off + MIX]
        ext_s[g, HIST:HIST + L, MIX:2 * MIX] = k_ref[g, :, off:off + MIX]
        ext_s[g, HIST:HIST + L, 2 * MIX:3 * MIX] = v_ref[g, :, off:off + MIX]
        z.append(z_ref[g, :, off:off + MIX])
        y = ext_s[g, base:base + L, :] * cw_ref[0:1, :]
        for j in range(1, CONV_W):
            y = y + ext_s[g, base + j:base + j + L, :] * cw_ref[j:j + 1, :]
        ext_s[g, 0:HIST, :] = ext_s[g, L:L + HIST, :]
        qkv.append(jax.nn.silu(y))
        gc = gc_ref[g]
        gr = gr_ref[g]
        g_c = -jnp.exp(a_log_c) * jax.nn.softplus(gc[:, 2 * N_HEADS:3 * N_HEADS] + dt_c)
        g_r = -jnp.exp(a_log_r) * jax.nn.softplus(gr[2 * N_HEADS:3 * N_HEADS, :] + dt_r)
        beta_c.append(jax.nn.sigmoid(gc[:, 3 * N_HEADS:4 * N_HEADS]))
        gcum_c.append(_dot_exact(tril, g_c))
        gcum_r.append(_dot_exact(g_r, tril.T))
    units = [(g, h) for g in range(G) for h in range(N_HEADS)]
    nu = range(len(units))
    sl = lambda h: slice(h * HEAD_DIM, (h + 1) * HEAD_DIM)
    qs = [qkv[g][:, h * HEAD_DIM:(h + 1) * HEAD_DIM] for g, h in units]
    ks = [qkv[g][:, MIX + h * HEAD_DIM:MIX + (h + 1) * HEAD_DIM] for g, h in units]
    vs = [qkv[g][:, 2 * MIX + h * HEAD_DIM:2 * MIX + (h + 1) * HEAD_DIM] for g, h in units]
    qs = [x * lax.rsqrt(jnp.sum(x * x, axis=-1, keepdims=True) + NORM_EPS) * (HEAD_DIM ** -0.5) for x in qs]
    ks = [x * lax.rsqrt(jnp.sum(x * x, axis=-1, keepdims=True) + NORM_EPS) for x in ks]
    gch = [gcum_c[g][:, h:h + 1] for g, h in units]
    beta = [beta_c[g][:, h:h + 1] for g, h in units]
    eg = [jnp.exp(x) for x in gch]
    dec_incl = [jnp.exp(jnp.where(incl, gch[u] - gcum_r[g][h:h + 1, :], -jnp.inf)) for u, (g, h) in enumerate(units)]
    s_old = [s_s[g, h] for g, h in units]
    prod = [_dot_nt(jnp.concatenate([ks[u], qs[u]], axis=0), ks[u]) for u in nu]
    kk = [x[0:L] for x in prod]
    qk = [x[L:2 * L] for x in prod]
    a_mat = [jnp.where(strict, beta[u] * kk[u] * dec_incl[u], 0.0) for u in nu]
    t_inv = _unit_lower_inverse(a_mat, L)
    uw = [_dot(t_inv[u], jnp.concatenate([vs[u] * beta[u], ks[u] * (beta[u] * eg[u])], axis=1)) for u in nu]
    uu = [x[:, 0:HEAD_DIM] for x in uw]
    w = [x[:, HEAD_DIM:2 * HEAD_DIM] for x in uw]
    ws = [_dot(jnp.concatenate([w[u], qs[u] * eg[u]], axis=0), s_old[u]) for u in nu]
    w_s = [x[0:L] for x in ws]
    q_s = [x[L:2 * L] for x in ws]
    delta = [uu[u] - w_s[u] for u in nu]
    qkd = [_dot(qk[u] * dec_incl[u], delta[u]) for u in nu]
    g_last = [x[L - 1:L, :] for x in gch]
    kd = [_dot_tn(ks[u] * jnp.exp(g_last[u] - gch[u]), delta[u]) for u in nu]
    for u, (g, h) in enumerate(units):
        s_s[g, h] = jnp.exp(g_last[u]) * s_old[u] + kd[u]
        o = q_s[u] + qkd[u]
        on = o * lax.rsqrt(jnp.mean(o * o, axis=-1, keepdims=True) + NORM_EPS) * norm_ref[...]
        out_ref[g, :, sl(h)] = (on * jax.nn.silu(z[g][:, sl(h)])).astype(out_ref.dtype)

    @pl.when(c == pl.num_programs(1) - 1)
    def _():
        sout_ref[...] = s_s[...]
        cout_ref[...] = ext_s[:, HIST + L - (CONV_W - 1):HIST + L, :]


def _hist_rows(state_rows, n_seq, width):
    k = state_rows.shape[1]
    return jnp.concatenate([jnp.zeros((n_seq, HIST - k, width), F32), state_rows], axis=1)


def _delta(proj3, small_c, small_r, blk0, n_seq, n_chunks, L, conv_w, a_log, dt_bias, dn_norm, conv0, s0,
           slab=None, prev=None):
    bias_c = jnp.zeros((1, LANE), F32).at[0, 0:N_HEADS].set(a_log).at[0, N_HEADS:2 * N_HEADS].set(dt_bias)
    bias_r = jnp.zeros((32, 1), F32).at[0:N_HEADS, 0].set(a_log).at[N_HEADS:2 * N_HEADS, 0].set(dt_bias)
    assert blk0 == 0
    G = _group_size(n_seq, n_chunks, G_DELTA)
    hist = _hist_rows(conv0, n_seq, 3 * MIX)
    segs = [_seg(L, n_chunks, G, C_DN + j * MIX) for j in range(4)]
    off = segs[0][1]
    assert all(s[1] == off for s in segs)

    full2 = lambda shape: pl.BlockSpec(shape, lambda b, c: (0, 0))
    s0, s0_spec = _state_in(s0, (N_HEADS, HEAD_DIM, HEAD_DIM), G)
    s_shape, s_spec = _state_out((N_HEADS, HEAD_DIM, HEAD_DIM), n_seq, slab, G)
    inputs = [proj3, proj3, proj3, proj3, small_c, small_r, bias_c, bias_r, conv_w, hist,
              dn_norm.reshape(1, HEAD_DIM), s0]
    body, extra_in, extra_specs, aliases = _alias_prev(functools.partial(_delta_body, L=L, G=G, off=off),
                                                       len(inputs), 1, prev)
    out, s_new, conv_new = pl.pallas_call(
        body,
        grid=(n_seq // G, n_chunks),
        in_specs=[segs[0][0], segs[1][0], segs[2][0], segs[3][0],
                  _per_chunk(G, n_chunks, (L, 32)), _per_chunk(G, n_chunks, (32, L)),
                  full2((1, LANE)), full2((32, 1)), full2((CONV_W, 3 * MIX)),
                  _per_seq(G, (HIST, 3 * MIX)),
                  full2((1, HEAD_DIM)),
                  s0_spec] + extra_specs,
        out_specs=[_per_chunk(G, n_chunks, (L, MIX)),
                   s_spec,
                   _per_seq(G, (CONV_W - 1, 3 * MIX))],
        out_shape=[jax.ShapeDtypeStruct((n_seq * n_chunks, L, MIX), BF16),
                   s_shape,
                   jax.ShapeDtypeStruct((n_seq, CONV_W - 1, 3 * MIX), F32)],
        scratch_shapes=[pltpu.VMEM((G, N_HEADS, HEAD_DIM, HEAD_DIM), F32),
                        pltpu.VMEM((G, HIST + max(L, HIST), 3 * MIX), F32)],
        input_output_aliases=aliases,
        compiler_params=_cparams(("parallel", "arbitrary")),
        name="deltanet",
    )(*inputs, *extra_in)
    return out, s_new, conv_new


RWX_BLOCK = 768
RW_GATE_PAD = 512


def _rwkv_pairs(pre, rk_ref, lnw_ref, lnb_ref, out_ref, s_s, *, L, G):
    r, k2, v, a, gate, kkk, e_cum, e_prev, e_neg = pre
    pw = 2 * RW_HD
    lo = lax.broadcasted_iota(jnp.int32, (1, pw), 1) < RW_HD
    row2 = lax.broadcasted_iota(jnp.int32, (2 * L, 2 * L), 0)
    col2 = lax.broadcasted_iota(jnp.int32, (2 * L, 2 * L), 1)
    incl_rows = jnp.where(row2 >= L, 1, 0)
    t_idx = row2 - L * incl_rows
    s_idx = jnp.where(col2 >= L, col2 - L, col2)
    mask2 = s_idx < t_idx + incl_rows
    col_lo = lax.broadcasted_iota(jnp.int32, (1, 2 * L), 1) < L
    blk_r = jnp.where(lax.broadcasted_iota(jnp.int32, (pw, pw), 0) >= RW_HD, 1, 0)
    blk_c = jnp.where(lax.broadcasted_iota(jnp.int32, (pw, pw), 1) >= RW_HD, 1, 0)
    bd_state = blk_r == blk_c

    def split_rows(x, sel):
        return jnp.concatenate([jnp.where(sel, x, 0.0), jnp.where(sel, 0.0, x)], axis=0)

    def seg_sum(x):
        s_lo = jnp.sum(jnp.where(lo, x, 0.0), axis=-1, keepdims=True)
        s_hi = jnp.sum(jnp.where(lo, 0.0, x), axis=-1, keepdims=True)
        return jnp.where(lo, s_lo, s_hi)

    units = [(g, p) for g in range(G) for p in range(RW_HEADS // 2)]
    nu = range(len(units))
    sl = lambda p: slice(p * pw, (p + 1) * pw)
    kkn = [kkk[g][:, sl(p)] for g, p in units]
    kkn = [x * lax.rsqrt(seg_sum(x * x) + NORM_EPS) for x in kkn]
    rs = [r[g][:, sl(p)] for g, p in units]
    ks = [k2[g][:, sl(p)] for g, p in units]
    vs = [v[g][:, sl(p)] for g, p in units]
    k_til = [kkn[u] * e_prev[g][:, sl(p)] for u, (g, p) in enumerate(units)]
    r_til = [rs[u] * e_cum[g][:, sl(p)] for u, (g, p) in enumerate(units)]
    k_hat = [ks[u] * e_neg[g][:, sl(p)] for u, (g, p) in enumerate(units)]
    b_hat = [kkn[u] * a[g][:, sl(p)] * e_neg[g][:, sl(p)] for u, (g, p) in enumerate(units)]
    s_old = [s_s[g, p] for g, p in units]
    kr = [jnp.concatenate([k_til[u], r_til[u]], axis=0) for u in nu]
    p_k = [jnp.where(mask2, _dot_nt(kr[u], split_rows(k_hat[u], lo)), 0.0) for u in nu]
    p_b = [jnp.where(mask2, _dot_nt(kr[u], split_rows(b_hat[u], lo)), 0.0) for u in nu]
    t_bd = _unit_lower_inverse([split_rows(x[0:L], col_lo) for x in p_b], L)
    t_pair = [x[0:L] + x[L:2 * L] for x in t_bd]
    kv = [_dot(p_k[u], split_rows(vs[u], lo)) for u in nu]
    ksd = [_dot_nt(kr[u], s_old[u]) for u in nu]
    inner = [ksd[u][0:L] + kv[u][0:L] for u in nu]
    y_part = [ksd[u][L:2 * L] + kv[u][L:2 * L] for u in nu]
    uu = [_dot(t_pair[u], split_rows(inner[u], lo)) for u in nu]
    bu = [_dot(p_b[u][L:2 * L], split_rows(uu[u], lo)) for u in nu]
    el = [e_cum[g][L - 1:L, sl(p)] for g, p in units]
    upd = [_dot_tn(jnp.concatenate([vs[u], -uu[u]], axis=0),
                   jnp.concatenate([k_hat[u] * el[u], b_hat[u] * el[u]], axis=0)) for u in nu]
    for u, (g, p) in enumerate(units):
        s_s[g, p] = s_old[u] * el[u] + jnp.where(bd_state, upd[u], 0.0)
        y = y_part[u] - bu[u]
        mean = seg_sum(y) * (1.0 / RW_HD)
        d = y - mean
        var = seg_sum(d * d) * (1.0 / RW_HD)
        y = d * lax.rsqrt(var + RW_LN_EPS) * lnw_ref[:, sl(p)] + lnb_ref[:, sl(p)]
        y = y + seg_sum(rs[u] * ks[u] * rk_ref[:, sl(p)]) * vs[u]
        out_ref[g, :, sl(p)] = (y * gate[g][:, sl(p)]).astype(out_ref.dtype)


def _rwkv_body(r_ref, k_ref, v_ref, x_ref, hist_ref, mu_ref, w0_ref, w2_ref, a0_ref, a2_ref, g2_ref,
               kk_ref, ka_ref, rk_ref, lnw_ref, lnb_ref, s0_ref,
               out_ref, sout_ref, shout_ref, s_s, ext_s, *, L, G, off):
    c = pl.program_id(1)
    width = 3 * MIX + RWX_BLOCK

    @pl.when(c == 0)
    def _():
        s_s[...] = jnp.zeros(s_s.shape, F32)
        for h in range(RW_HEADS):
            o = (h % 2) * RW_HD
            s_s[:, h // 2, o:o + RW_HD, o:o + RW_HD] = s0_ref[:, h]
        ext_s[:, 0:HIST, :] = hist_ref[...]

    incl, _ = _tri_masks(L)
    r, k2, v, a, gate, kkk, e_cum, e_prev, e_neg = [], [], [], [], [], [], [], [], []
    for g in range(G):
        ext_s[g, HIST:HIST + L, 0:MIX] = r_ref[g, :, off:off + MIX]
        ext_s[g, HIST:HIST + L, MIX:2 * MIX] = k_ref[g, :, off:off + MIX]
        ext_s[g, HIST:HIST + L, 2 * MIX:3 * MIX] = v_ref[g, :, off:off + MIX]
        ext_s[g, HIST:HIST + L, 3 * MIX:3 * MIX + RW_CODES] = x_ref[g, :, off:off + RW_CODES]
        ext_s[g, HIST:HIST + L, 3 * MIX + RW_CODES:width] = jnp.zeros((L, width - 3 * MIX - RW_CODES), F32)
        cur = ext_s[g, HIST:HIST + L, :]
        prev = ext_s[g, HIST - 1:HIST - 1 + L, :]
        ext_s[g, 0:HIST, :] = ext_s[g, L:L + HIST, :]
        z = cur + (prev - cur) * mu_ref[...]
        k = z[:, MIX:2 * MIX]
        zw = z[:, 3 * MIX + RWX_W:3 * MIX + RWX_W + RW_DECAY_RANK]
        za = z[:, 3 * MIX + RWX_A:3 * MIX + RWX_A + RW_A_RANK]
        zg = z[:, 3 * MIX + RWX_G:3 * MIX + RWX_G + RW_GATE_PAD]
        w_log = -jax.nn.softplus(-(w0_ref[...] + _dot(jnp.tanh(zw), w2_ref[...]))) - 0.5
        a_g = jax.nn.sigmoid(a0_ref[...] + _dot(za, a2_ref[...]))
        log_decay = -jnp.exp(w_log)
        cum = _dot_exact(incl.astype(F32), log_decay)
        r.append(z[:, 0:MIX])
        v.append(z[:, 2 * MIX:3 * MIX])
        a.append(a_g)
        gate.append(_dot(jax.nn.sigmoid(zg), g2_ref[...]))
        kkk.append(k * kk_ref[...])
        k2.append(k * (1.0 + (a_g - 1.0) * ka_ref[...]))
        e_cum.append(jnp.exp(cum))
        e_prev.append(jnp.exp(cum - log_decay))
        e_neg.append(jnp.exp(-cum))
    pre = (r, k2, v, a, gate, kkk, e_cum, e_prev, e_neg)
    _rwkv_pairs(pre, rk_ref, lnw_ref, lnb_ref, out_ref, s_s, L=L, G=G)

    @pl.when(c == pl.num_programs(1) - 1)
    def _():
        for h in range(RW_HEADS):
            o = (h % 2) * RW_HD
            sout_ref[:, h] = s_s[:, h // 2, o:o + RW_HD, o:o + RW_HD]
        shout_ref[...] = ext_s[:, HIST + L - 1:HIST + L, 0:_RW_COLS]


def _rw_cols(vec):
    pad = jnp.zeros((RWX_BLOCK - (_RW_COLS - 3 * MIX),), F32)
    return jnp.concatenate([vec, pad]).reshape(1, 3 * MIX + RWX_BLOCK)


def _rwkv(proj3, blk0, n_seq, n_chunks, L, p, shift0, s0, slab=None, prev=None):
    width = 3 * MIX + RWX_BLOCK
    hist = jnp.concatenate([jnp.zeros((n_seq, HIST - 1, width), F32),
                            jnp.pad(shift0, ((0, 0), (0, width - _RW_COLS)))[:, None, :]], axis=1)
    g2 = jnp.pad(p['rw_g2'], ((0, RW_GATE_PAD - RW_GATE_RANK), (0, 0))).astype(BF16)
    row = lambda vec: vec.reshape(1, MIX)
    assert blk0 == 0
    G = _group_size(n_seq, n_chunks, G_RWKV)
    segs = [_seg(L, n_chunks, G, C_RW + j * MIX) for j in range(3)] + [_seg(L, n_chunks, G, C_RWX, RW_CODES)]
    off = segs[0][1]
    assert all(s[1] == off for s in segs)

    full2 = lambda shape: pl.BlockSpec(shape, lambda b, c: (0, 0))
    s0, s0_spec = _state_in(s0, (RW_HEADS, RW_HD, RW_HD), G)
    s_shape, s_spec = _state_out((RW_HEADS, RW_HD, RW_HD), n_seq, slab, G)
    inputs = [proj3, proj3, proj3, proj3, hist, _rw_cols(p['rw_mu']),
              row(p['rw_w0']), p['rw_w2'].astype(BF16), row(p['rw_a0']), p['rw_a2'].astype(BF16), g2,
              row(p['rw_k_k']), row(p['rw_k_a']), row(p['rw_r_k'].reshape(-1)), row(p['rw_ln_w']), row(p['rw_ln_b']),
              s0]
    body, extra_in, extra_specs, aliases = _alias_prev(functools.partial(_rwkv_body, L=L, G=G, off=off),
                                                       len(inputs), 1, prev)
    out, s_new, shift_new = pl.pallas_call(
        body,
        grid=(n_seq // G, n_chunks),
        in_specs=[segs[0][0], segs[1][0], segs[2][0], segs[3][0],
                  _per_seq(G, (HIST, width)),
                  full2((1, width)),
                  full2((1, MIX)), full2((RW_DECAY_RANK, MIX)),
                  full2((1, MIX)), full2((RW_A_RANK, MIX)),
                  full2((RW_GATE_PAD, MIX)),
                  full2((1, MIX)), full2((1, MIX)), full2((1, MIX)), full2((1, MIX)), full2((1, MIX)),
                  s0_spec] + extra_specs,
        out_specs=[_per_chunk(G, n_chunks, (L, MIX)),
                   s_spec,
                   _per_seq(G, (1, _RW_COLS))],
        out_shape=[jax.ShapeDtypeStruct((n_seq * n_chunks, L, MIX), BF16),
                   s_shape,
                   jax.ShapeDtypeStruct((n_seq, 1, _RW_COLS), F32)],
        scratch_shapes=[pltpu.VMEM((G, RW_HEADS // 2, 2 * RW_HD, 2 * RW_HD), F32),
                        pltpu.VMEM((G, HIST + max(L, HIST), width), F32)],
        input_output_aliases=aliases,
        compiler_params=_cparams(("parallel", "arbitrary")),
        name="rwkv7",
    )(*inputs, *extra_in)
    return out, s_new, shift_new.reshape(n_seq, _RW_COLS)


def _shift_rows(x, d, fill):
    rows = lax.broadcasted_iota(jnp.int32, x.shape, 0)
    return jnp.where(rows >= d, pltpu.roll(x, d, axis=0), fill)


def _lru_body(x_ref, g_ref, hist_ref, cw_ref, cb_ref, wa_ref, ba_ref, wx_ref, bx_ref, lam_ref, h0_ref,
              out_ref, hout_ref, cout_ref, h_s, ext_s, xc_s, *, L, G, off):
    c = pl.program_id(1)

    @pl.when(c == 0)
    def _():
        h_s[...] = h0_ref[...]
        ext_s[:, 0:HIST, :] = hist_ref[...]

    base = HIST - (CONV_W - 1)
    for g in range(G):
        ext_s[g, HIST:HIST + L, :] = x_ref[g, :, off:off + MIX]
        conv = ext_s[g, base:base + L, :] * cw_ref[0:1, :]
        for j in range(1, CONV_W):
            conv = conv + ext_s[g, base + j:base + j + L, :] * cw_ref[j:j + 1, :]
        ext_s[g, 0:HIST, :] = ext_s[g, L:L + HIST, :]
        xc_s[g * L:(g + 1) * L, :] = conv + cb_ref[...]
    xc = xc_s[...]
    r = jax.nn.sigmoid(_dot(xc, wa_ref[...]) + ba_ref[...])
    i = jax.nn.sigmoid(_dot(xc, wx_ref[...]) + bx_ref[...])
    log_a = -LRU_C * r * jax.nn.softplus(-lam_ref[...])
    a = jnp.exp(log_a)
    u_all = jnp.sqrt(-jnp.tanh(log_a) * (a * a + 1.0)) * (i * xc)
    for g in range(G):
        a_g = a[g * L:(g + 1) * L, :]
        u_g = u_all[g * L:(g + 1) * L, :]
        h_prev = h_s[g]
        if L % 8 == 0:
            acc_a, acc_h = a_g, u_g
            d = 1
            while d < L:
                acc_h = acc_a * _shift_rows(acc_h, d, 0.0) + acc_h
                acc_a = acc_a * _shift_rows(acc_a, d, 1.0)
                d *= 2
            hs = acc_a * h_prev + acc_h
            h_s[g] = hs[L - 1:L, :]
        else:
            rows = []
            for t in range(L):
                h_prev = a_g[t:t + 1, :] * h_prev + u_g[t:t + 1, :]
                rows.append(h_prev)
            hs = jnp.concatenate(rows, axis=0)
            h_s[g] = h_prev
        out_ref[g] = (hs * jax.nn.gelu(g_ref[g, :, off:off + MIX])).astype(out_ref.dtype)

    @pl.when(c == pl.num_programs(1) - 1)
    def _():
        hout_ref[...] = h_s[...]
        cout_ref[...] = ext_s[:, HIST + L - (CONV_W - 1):HIST + L, :]


def _block_diag(w):
    nb, bs, _ = w.shape
    eye = jnp.eye(nb, dtype=w.dtype)
    return (eye[:, None, :, None] * w[:, :, None, :]).reshape(nb * bs, nb * bs)


def _lru(proj3, blk0, n_seq, n_chunks, L, p, conv0, h0):
    hist = _hist_rows(conv0, n_seq, MIX)
    row = lambda vec: vec.reshape(1, MIX)
    full2 = lambda shape: pl.BlockSpec(shape, lambda b, c: (0, 0))
    assert blk0 == 0
    G = _group_size(n_seq, n_chunks, G_LRU)
    segs = [_seg(L, n_chunks, G, C_LRU + j * MIX) for j in range(2)]
    off = segs[0][1]
    assert all(s[1] == off for s in segs)
    out, h_new, conv_new = pl.pallas_call(
        functools.partial(_lru_body, L=L, G=G, off=off),
        grid=(n_seq // G, n_chunks),
        in_specs=[segs[0][0], segs[1][0],
                  _per_seq(G, (HIST, MIX)),
                  full2((CONV_W, MIX)), full2((1, MIX)),
                  full2((MIX, MIX)), full2((1, MIX)), full2((MIX, MIX)), full2((1, MIX)), full2((1, MIX)),
                  _per_seq(G, (1, MIX))],
        out_specs=[_per_chunk(G, n_chunks, (L, MIX)),
                   _per_seq(G, (1, MIX)),
                   _per_seq(G, (CONV_W - 1, MIX))],
        out_shape=[jax.ShapeDtypeStruct((n_seq * n_chunks, L, MIX), BF16),
                   jax.ShapeDtypeStruct((n_seq, 1, MIX), F32),
                   jax.ShapeDtypeStruct((n_seq, CONV_W - 1, MIX), F32)],
        scratch_shapes=[pltpu.VMEM((G, 1, MIX), F32),
                        pltpu.VMEM((G, HIST + max(L, HIST), MIX), F32),
                        pltpu.VMEM((G * L, MIX), F32)],
        compiler_params=_cparams(("parallel", "arbitrary")),
        name="rglru",
    )(proj3, proj3, hist, p['lru_conv_w'], row(p['lru_conv_b']),
      _block_diag(p['lru_wa']).astype(BF16), row(p['lru_ba']),
      _block_diag(p['lru_wx']).astype(BF16), row(p['lru_bx']), row(p['lru_lambda']),
      h0.reshape(n_seq, 1, MIX))
    return out, h_new.reshape(n_seq, MIX), conv_new


L_MLSTM = 256
L_DELTA = 64
L_RWKV = 64
L_LRU = 256


def _row_tile(m, cap):
    return max(t for t in range(16, cap + 1, 16) if m % t == 0)


def _chunk_len(t, want):
    return math.gcd(t, want)


def _small_views(small, L):
    m = small.shape[0]
    small_c = small.reshape(m // L, L, 32)
    small_r = jnp.swapaxes(small_c, 1, 2)
    return small_c, small_r


def _mixers(proj, n_seq, t_len, p, st, slab, prev):
    m = proj.shape[0]
    ml_c, ml_n, ml_m, dn_s, dn_conv, lru_h, lru_conv, rw_s, rw_shift = st
    prev_c, prev_dn, prev_rw = prev if prev is not None else (None, None, None)
    small = jnp.concatenate([proj[:, C_ML + 4 * MIX:C_ML + 4 * MIX + 2 * N_HEADS],
                             proj[:, C_DN + 4 * MIX:C_DN + 4 * MIX + 2 * N_HEADS]], axis=1)

    def view(want):
        L = _chunk_len(t_len, want)
        assert m % L == 0
        return proj.reshape(m // L, L, N_PROJ), L, t_len // L

    p3, L, nch = view(L_MLSTM)
    sc, sr = _small_views(small, L)
    out_ml, ml_c, ml_n, ml_m = _mlstm(p3, sc, sr, 0, n_seq, nch, L, p['ml_b_i'], p['ml_b_f'], p['ml_norm'],
                                      ml_c, ml_n, ml_m, slab, prev_c)
    p3, L, nch = view(L_DELTA)
    sc, sr = _small_views(small, L)
    out_dn, dn_s, dn_conv = _delta(p3, sc, sr, 0, n_seq, nch, L, p['dn_conv'], p['dn_A_log'], p['dn_dt_bias'],
                                   p['dn_norm'], dn_conv, dn_s, slab, prev_dn)
    p3, L, nch = view(L_LRU)
    out_lru, lru_h, lru_conv = _lru(p3, 0, n_seq, nch, L, p, lru_conv, lru_h)
    p3, L, nch = view(L_RWKV)
    out_rw, rw_s, rw_shift = _rwkv(p3, 0, n_seq, nch, L, p, rw_shift, rw_s, slab, prev_rw)
    outs = tuple(o.reshape(n_seq * t_len, MIX) for o in (out_ml, out_dn, out_lru, out_rw))
    return outs, (ml_c, ml_n, ml_m, dn_s, dn_conv, lru_h, lru_conv, rw_s, rw_shift)


_STATE_SHAPES = ((N_HEADS, HEAD_DIM, HEAD_DIM), (N_HEADS, HEAD_DIM), (N_HEADS,), (N_HEADS, HEAD_DIM, HEAD_DIM),
                 (CONV_W - 1, 3 * MIX), (MIX,), (CONV_W - 1, MIX), (RW_HEADS, RW_HD, RW_HD), (_RW_COLS,))
_BIG_STATES = (0, 3, 7)


def kernel(x_prompt, x_sample, state_mlstm_C, state_mlstm_n, state_mlstm_m, state_delta_S, state_delta_conv, state_rglru_h, state_rglru_conv, state_rwkv_S, state_rwkv_shift, norm_mix, w_in, ml_b_i, ml_b_f, ml_norm, dn_conv, dn_A_log, dn_dt_bias, dn_norm, lru_conv_w, lru_conv_b, lru_wa, lru_ba, lru_wx, lru_bx, lru_lambda, rw_mu, rw_w0, rw_w2, rw_a0, rw_a2, rw_g2, rw_k_k, rw_k_a, rw_r_k, rw_ln_w, rw_ln_b, w_branch, w_out, norm_ffn, w_ffn_gate, w_ffn_up, w_ffn_down, norm_final):
    params = {
        'ml_b_i': ml_b_i, 'ml_b_f': ml_b_f, 'ml_norm': ml_norm,
        'dn_conv': dn_conv, 'dn_A_log': dn_A_log, 'dn_dt_bias': dn_dt_bias, 'dn_norm': dn_norm,
        'lru_conv_w': lru_conv_w, 'lru_conv_b': lru_conv_b, 'lru_wa': lru_wa, 'lru_ba': lru_ba,
        'lru_wx': lru_wx, 'lru_bx': lru_bx, 'lru_lambda': lru_lambda,
        'rw_mu': rw_mu, 'rw_w0': rw_w0, 'rw_w2': rw_w2, 'rw_a0': rw_a0, 'rw_a2': rw_a2, 'rw_g2': rw_g2,
        'rw_k_k': rw_k_k, 'rw_k_a': rw_k_a, 'rw_r_k': rw_r_k, 'rw_ln_w': rw_ln_w, 'rw_ln_b': rw_ln_b,
    }
    depth = w_in.shape[0]
    bp, tp, d = x_prompt.shape
    bs, ts, _ = x_sample.shape
    n_p, n_s = bp * tp, bs * ts
    m = n_p + n_s
    x = jnp.concatenate([x_prompt.reshape(n_p, d), x_sample.reshape(n_s, d)], axis=0)
    tm = _row_tile(m, 1088)
    tm_panel = _row_tile(m, 2176)
    tm_norm = _row_tile(m, 544)
    tm_group = _row_tile(math.gcd(n_p, n_s), 512)
    sample_states = (state_mlstm_C, state_mlstm_n, state_mlstm_m, state_delta_S, state_delta_conv,
                     state_rglru_h, state_rglru_conv, state_rwkv_S, state_rwkv_shift)
    big_p = big_s = None
    new_p = [[] for _ in _STATE_SHAPES]
    new_s = [[] for _ in _STATE_SHAPES]
    wd_b = w_ffn_down.astype(BF16)
    w_out_b = w_out.astype(BF16)
    w_branch_b = w_branch.astype(BF16)
    for l in range(depth):
        p = {name: w[l] for name, w in params.items()}
        h = _rmsnorm(x, norm_mix[l], BF16, tm_norm)
        proj = _matmul(h, w_in, l, tm_panel, 512)
        st_p = tuple(jnp.zeros((bp,) + shp, F32) for shp in _STATE_SHAPES)
        st_s = tuple((s, l) if s.ndim == 5 else s[l] for s in sample_states)
        outs_p, st_p = _mixers(proj, bp, tp, p, st_p, (l, depth), big_p)
        outs_s, st_s = _mixers(proj[n_p:], bs, ts, p, st_s, (l, depth), big_s)
        big_p = tuple(st_p[i] for i in _BIG_STATES)
        big_s = tuple(st_s[i] for i in _BIG_STATES)
        for lst, s in zip(new_p, st_p):
            lst.append(s)
        for lst, s in zip(new_s, st_s):
            lst.append(s)
        mix = _merge(proj, outs_p, outs_s, w_branch_b, l, tm_group, 1024)
        x = _matmul_residual(mix, w_out_b, l, x, tm, 1024, 2048)
        h2 = _rmsnorm(x, norm_ffn[l], BF16, tm_norm)
        act = _ffn_up(h2, w_ffn_gate, w_ffn_up, l, tm_panel, 256)
        x = _matmul_residual(act, wd_b, l, x, tm // 2, 512, D_FF)
    y_prompt, y_sample = _rmsnorm_split(x, norm_final, n_p, tm_group)
    y_prompt = y_prompt.reshape(bp, tp, d)
    y_sample = y_sample.reshape(bs, ts, d)
    def stacked(per_layer):
        return tuple(lst[-1] if i in _BIG_STATES else jnp.stack(lst) for i, lst in enumerate(per_layer))

    return (y_prompt, y_sample) + stacked(new_p) + stacked(new_s)
```

```python
import functools
import math

import jax
import jax.numpy as jnp
from jax import lax
from jax.experimental import pallas as pl
from jax.experimental.pallas import tpu as pltpu

F32 = jnp.float32
BF16 = jnp.bfloat16

D_MODEL = 4096
MIX = D_MODEL // 4
HEAD_DIM = 128
N_HEADS = MIX // HEAD_DIM
RW_HD = 64
RW_HEADS = MIX // RW_HD
RW_DECAY_RANK = 128
RW_A_RANK = 128
RW_GATE_RANK = 480
CONV_W = 4
LRU_C = 8.0
D_FF = -(-(8 * D_MODEL) // (3 * 256)) * 256
NORM_EPS = 1e-6
RW_LN_EPS = 64e-5

_ML_COLS = 4 * MIX + 2 * N_HEADS
_DN_COLS = 4 * MIX + 2 * N_HEADS
_LRU_COLS = 2 * MIX
_RW_COLS = 3 * MIX + RW_DECAY_RANK + RW_A_RANK + RW_GATE_RANK
C_ML = 0
C_DN = _ML_COLS
C_LRU = C_DN + _DN_COLS
C_RW = C_LRU + _LRU_COLS
C_RWX = C_RW + 3 * MIX
C_GATE = C_RW + _RW_COLS
N_PROJ = C_GATE + 4 * D_MODEL
RW_CODES = _RW_COLS - 3 * MIX
RWX_W = 0
RWX_A = RW_DECAY_RANK
RWX_G = RW_DECAY_RANK + RW_A_RANK

VMEM_LIMIT_BYTES = 56 * 1024 * 1024
LANE = 128


def _cparams(sem):
    return pltpu.CompilerParams(dimension_semantics=sem, vmem_limit_bytes=VMEM_LIMIT_BYTES)


def _rmsnorm_body(x_ref, g_ref, o_ref):
    x = x_ref[...]
    y = x * lax.rsqrt(jnp.mean(x * x, axis=-1, keepdims=True) + NORM_EPS)
    o_ref[...] = (y * g_ref[...]).astype(o_ref.dtype)


def _rmsnorm(x, g, out_dtype, tm):
    m, d = x.shape
    return pl.pallas_call(
        _rmsnorm_body,
        grid=(m // tm,),
        in_specs=[pl.BlockSpec((tm, d), lambda i: (i, 0)), pl.BlockSpec((1, d), lambda i: (0, 0))],
        out_specs=pl.BlockSpec((tm, d), lambda i: (i, 0)),
        out_shape=jax.ShapeDtypeStruct((m, d), out_dtype),
        compiler_params=_cparams(("parallel",)),
        name="rmsnorm",
    )(x, g.reshape(1, d))


def _rmsnorm_split_body(x_ref, g_ref, oa_ref, ob_ref, *, tiles_a):
    x = x_ref[...]
    y = x * lax.rsqrt(jnp.mean(x * x, axis=-1, keepdims=True) + NORM_EPS) * g_ref[...]
    i = pl.program_id(0)

    @pl.when(i < tiles_a)
    def _():
        oa_ref[...] = y

    @pl.when(i >= tiles_a)
    def _():
        ob_ref[...] = y


def _rmsnorm_split(x, g, rows_a, tm):
    m, d = x.shape
    rows_b = m - rows_a
    assert rows_a % tm == 0 and rows_b % tm == 0
    tiles_a = rows_a // tm
    return pl.pallas_call(
        functools.partial(_rmsnorm_split_body, tiles_a=tiles_a),
        grid=(m // tm,),
        in_specs=[pl.BlockSpec((tm, d), lambda i: (i, 0)), pl.BlockSpec((1, d), lambda i: (0, 0))],
        out_specs=[pl.BlockSpec((tm, d), lambda i: (jnp.minimum(i, tiles_a - 1), 0)),
                   pl.BlockSpec((tm, d), lambda i: (jnp.maximum(i - tiles_a, 0), 0))],
        out_shape=[jax.ShapeDtypeStruct((rows_a, d), F32), jax.ShapeDtypeStruct((rows_b, d), F32)],
        compiler_params=_cparams(("arbitrary",)),
        name="rmsnorm_split",
    )(x, g.reshape(1, d))


def _mm_body(x_ref, w_ref, o_ref):
    o_ref[...] = jnp.dot(x_ref[...], w_ref[0].astype(BF16), preferred_element_type=F32).astype(o_ref.dtype)


def _matmul(x, w, layer, tm, tn, out_dtype=F32):
    m, k = x.shape
    n = w.shape[2]
    return pl.pallas_call(
        _mm_body,
        grid=(m // tm, pl.cdiv(n, tn)),
        in_specs=[pl.BlockSpec((tm, k), lambda i, j: (i, 0), pipeline_mode=pl.Buffered(1)),
                  pl.BlockSpec((1, k, tn), lambda i, j: (layer, 0, j))],
        out_specs=pl.BlockSpec((tm, tn), lambda i, j: (i, j)),
        out_shape=jax.ShapeDtypeStruct((m, n), out_dtype),
        compiler_params=_cparams(("parallel", "arbitrary")),
        name="matmul",
    )(x, w)


def _mm_res_body(x_ref, w_ref, r_ref, o_ref, acc_ref):
    kk = pl.program_id(2)

    @pl.when(kk == 0)
    def _():
        acc_ref[...] = r_ref[...]

    acc_ref[...] += jnp.dot(x_ref[...], w_ref[0], preferred_element_type=F32)

    @pl.when(kk == pl.num_programs(2) - 1)
    def _():
        o_ref[...] = acc_ref[...]


def _matmul_residual(x, w, layer, res, tm, tn, tk):
    m, k = x.shape
    n = w.shape[2]
    return pl.pallas_call(
        _mm_res_body,
        grid=(n // tn, m // tm, k // tk),
        in_specs=[pl.BlockSpec((tm, tk), lambda j, i, kk: (i, kk)),
                  pl.BlockSpec((1, tk, tn), lambda j, i, kk: (layer, kk, j)),
                  pl.BlockSpec((tm, tn), lambda j, i, kk: (i, j))],
        out_specs=pl.BlockSpec((tm, tn), lambda j, i, kk: (i, j)),
        out_shape=jax.ShapeDtypeStruct((m, n), F32),
        scratch_shapes=[pltpu.VMEM((tm, tn), F32)],
        compiler_params=_cparams(("parallel", "parallel", "arbitrary")),
        name="matmul_residual",
    )(x, w, res)


def _ffn_up_body(x_ref, wg_ref, wu_ref, o_ref):
    x = x_ref[...]
    g = jnp.dot(x, wg_ref[0].astype(BF16), preferred_element_type=F32)
    u = jnp.dot(x, wu_ref[0].astype(BF16), preferred_element_type=F32)
    o_ref[...] = (jax.nn.silu(g) * u).astype(o_ref.dtype)


def _ffn_up(x, wg, wu, layer, tm, tn):
    m, k = x.shape
    n = wg.shape[2]
    assert n % tn == 0
    wspec = pl.BlockSpec((1, k, tn), lambda i, j: (layer, 0, j))
    return pl.pallas_call(
        _ffn_up_body,
        grid=(m // tm, n // tn),
        in_specs=[pl.BlockSpec((tm, k), lambda i, j: (i, 0), pipeline_mode=pl.Buffered(1)), wspec, wspec],
        out_specs=pl.BlockSpec((tm, tn), lambda i, j: (i, j)),
        out_shape=jax.ShapeDtypeStruct((m, n), BF16),
        compiler_params=_cparams(("parallel", "arbitrary")),
        name="ffn_up",
    )(x, wg, wu)


def _merge_body(*refs, nb, tiles_a):
    g_refs, a_refs, b_refs = refs[:nb], refs[nb:2 * nb], refs[2 * nb:3 * nb]
    w_ref, o_ref = refs[3 * nb], refs[3 * nb + 1]

    def mix(x_refs):
        acc = jax.nn.sigmoid(g_refs[0][...]) * jnp.dot(x_refs[0][...], w_ref[0, 0], preferred_element_type=F32)
        for n in range(1, nb):
            acc = acc + jax.nn.sigmoid(g_refs[n][...]) * jnp.dot(x_refs[n][...], w_ref[0, n],
                                                                 preferred_element_type=F32)
        o_ref[...] = acc.astype(o_ref.dtype)

    i = pl.program_id(1)
    pl.when(i < tiles_a)(lambda: mix(a_refs))
    pl.when(i >= tiles_a)(lambda: mix(b_refs))


def _merge(proj, branches_a, branches_b, wb, layer, tm, tn):
    nb = len(branches_a)
    rows_a, k = branches_a[0].shape
    rows_b = branches_b[0].shape[0]
    assert rows_a % tm == 0 and rows_b % tm == 0
    tiles_a, tiles_b = rows_a // tm, rows_b // tm
    n = wb.shape[3]

    def gate_spec(b):
        return pl.BlockSpec((pl.Element(tm), pl.Element(tn)),
                            lambda j, i: (i * tm, (C_GATE // LANE + b * (n // LANE) + j * (tn // LANE)) * LANE))

    a_spec = pl.BlockSpec((tm, k), lambda j, i: (jnp.minimum(i, tiles_a - 1), 0))
    b_spec = pl.BlockSpec((tm, k), lambda j, i: (jnp.maximum(i - tiles_a, 0), 0))
    return pl.pallas_call(
        functools.partial(_merge_body, nb=nb, tiles_a=tiles_a),
        grid=(n // tn, tiles_a + tiles_b),
        in_specs=[gate_spec(b) for b in range(nb)] + [a_spec] * nb + [b_spec] * nb
                 + [pl.BlockSpec((1, nb, k, tn), lambda j, i: (layer, 0, 0, j))],
        out_specs=pl.BlockSpec((tm, tn), lambda j, i: (i, j)),
        out_shape=jax.ShapeDtypeStruct((rows_a + rows_b, n), BF16),
        compiler_params=_cparams(("parallel", "arbitrary")),
        name="merge",
    )(*([proj] * nb), *branches_a, *branches_b, wb)


G_MLSTM = 2
G_DELTA = 4
G_RWKV = 8
G_LRU = 8


def _group_size(n_seq, n_chunks, want):
    return math.gcd(n_seq, want) if n_chunks == 1 else 1


SUBLANES = 8


def _seg(L, n_chunks, G, row0, col, width=MIX):
    start = col // LANE * LANE
    off = col - start
    win = -(-(off + width) // LANE) * LANE
    rows = G * L
    assert rows % SUBLANES == 0 and row0 % SUBLANES == 0
    spec = pl.BlockSpec((pl.Element(rows), pl.Element(win)),
                        lambda b, c: ((row0 // SUBLANES + (b * n_chunks + c) * (rows // SUBLANES)) * SUBLANES, start))
    return spec, off


def _per_seq(G, tail):
    zeros = (0,) * len(tail)
    return pl.BlockSpec((G,) + tail, lambda b, c: (b,) + zeros)


def _per_chunk(G, n_chunks, tail):
    zeros = (0,) * len(tail)
    return pl.BlockSpec((G,) + tail, lambda b, c: (b * n_chunks + c,) + zeros)


def _state_in(state, tail, G):
    zeros = (0,) * len(tail)
    if isinstance(state, tuple):
        arr, layer = state
        return arr, pl.BlockSpec((None, G) + tail, lambda b, c: (layer, b) + zeros)
    return state, _per_seq(G, tail)


def _state_out(tail, n_seq, slab, G):
    zeros = (0,) * len(tail)
    if slab is None:
        return jax.ShapeDtypeStruct((n_seq,) + tail, F32), _per_seq(G, tail)
    layer, depth = slab
    return (jax.ShapeDtypeStruct((depth, n_seq) + tail, F32),
            pl.BlockSpec((None, G) + tail, lambda b, c: (layer, b) + zeros))


def _alias_prev(body, n_inputs, out_index, prev):
    if prev is None:
        return body, [], [], {}

    def aliased_body(*refs):
        return body(*refs[:n_inputs], *refs[n_inputs + 1:])

    return aliased_body, [prev], [pl.BlockSpec(memory_space=pl.ANY)], {n_inputs: out_index}


def _tri_masks(L):
    row = lax.broadcasted_iota(jnp.int32, (L, L), 0)
    col = lax.broadcasted_iota(jnp.int32, (L, L), 1)
    return col <= row, col < row


def _dot(a, b):
    return jnp.dot(a.astype(BF16), b.astype(BF16), preferred_element_type=F32)


def _dot_nt(a, b):
    return lax.dot_general(a.astype(BF16), b.astype(BF16), (((1,), (1,)), ((), ())), preferred_element_type=F32)


def _dot_tn(a, b):
    return lax.dot_general(a.astype(BF16), b.astype(BF16), (((0,), (0,)), ((), ())), preferred_element_type=F32)


def _dot_exact(a, b):
    return jnp.dot(a, b, precision=lax.Precision.HIGHEST, preferred_element_type=F32)


def _unit_lower_inverse(ns, L):
    size = ns[0].shape[0]
    eye = (lax.broadcasted_iota(jnp.int32, (size, size), 0)
           == lax.broadcasted_iota(jnp.int32, (size, size), 1)).astype(F32)
    ms = [-n for n in ns]
    ts = [eye + m for m in ms]
    span = 2
    while span < L:
        ms = [_dot(m, m) for m in ms]
        ts = [t + _dot(t, m) for t, m in zip(ts, ms)]
        span *= 2
    return ts


def _mlstm_body(q_ref, k_ref, v_ref, o_ref, gc_ref, gr_ref, bc_ref, br_ref, norm_ref, c0_ref, n0_ref, m0_ref,
                out_ref, cout_ref, nout_ref, mout_ref, c_s, n_s, m_s, *, L, G):
    c = pl.program_id(1)

    @pl.when(c == 0)
    def _():
        c_s[...] = c0_ref[...]
        n_s[...] = n0_ref[...]
        m_s[...] = m0_ref[...]

    incl, _ = _tri_masks(L)
    tril = incl.astype(F32)
    ig_c, ig_r, b_c, b_r = [], [], [], []
    for g in range(G):
        gc = gc_ref[g]
        gr = gr_ref[g]
        ig_c.append(gc[:, 0:N_HEADS] + bc_ref[:, 0:N_HEADS])
        ig_r.append(gr[0:N_HEADS, :] + br_ref[0:N_HEADS, :])
        lf_c = jax.nn.log_sigmoid(gc[:, N_HEADS:2 * N_HEADS] + bc_ref[:, N_HEADS:2 * N_HEADS])
        lf_r = jax.nn.log_sigmoid(gr[N_HEADS:2 * N_HEADS, :] + br_ref[N_HEADS:2 * N_HEADS, :])
        b_c.append(_dot_exact(tril, lf_c))
        b_r.append(_dot_exact(lf_r, tril.T))
    units = [(g, h) for g in range(G) for h in range(N_HEADS)]
    sl = lambda h: slice(h * HEAD_DIM, (h + 1) * HEAD_DIM)
    qs = [q_ref[g * L:(g + 1) * L,sl(h)] * (HEAD_DIM ** -0.5) for g, h in units]
    ks = [k_ref[g * L:(g + 1) * L,sl(h)] for g, h in units]
    vs = [v_ref[g * L:(g + 1) * L,sl(h)] for g, h in units]
    c_old = [c_s[g, h] for g, h in units]
    n_old = [n_s[g, h:h + 1, :] for g, h in units]
    qk = [_dot_nt(q, k) for q, k in zip(qs, ks)]
    qc = [_dot(q, co) for q, co in zip(qs, c_old)]
    bcs = [b_c[g][:, h:h + 1] for g, h in units]
    m_prev = [m_s[g][:, h:h + 1] for g, h in units]
    log_d = [jnp.where(incl, bcs[u] - b_r[g][h:h + 1, :] + ig_r[g][h:h + 1, :], -jnp.inf)
             for u, (g, h) in enumerate(units)]
    state_w = [b + m for b, m in zip(bcs, m_prev)]
    m_t = [jnp.maximum(s, jnp.max(ld, axis=-1, keepdims=True)) for s, ld in zip(state_w, log_d)]
    scores = [a * jnp.exp(ld - mt) for a, ld, mt in zip(qk, log_d, m_t)]
    sw = [jnp.exp(s - mt) for s, mt in zip(state_w, m_t)]
    sv = [_dot(s, v) for s, v in zip(scores, vs)]
    m_last = [mt[L - 1:L, :] for mt in m_t]
    b_last = [b[L - 1:L, :] for b in bcs]
    kw = [ks[u] * jnp.exp(b_last[u] - bcs[u] + ig_c[g][:, h:h + 1] - m_last[u]) for u, (g, h) in enumerate(units)]
    kv = [_dot_tn(a, v) for a, v in zip(kw, vs)]
    for u, (g, h) in enumerate(units):
        decay = jnp.exp(b_last[u] + m_prev[u] - m_last[u])
        c_s[g, h] = decay * c_old[u] + kv[u]
        n_s[g, h:h + 1, :] = decay * n_old[u] + jnp.sum(kw[u], axis=0, keepdims=True)
        num = sv[u] + sw[u] * qc[u]
        den = jnp.sum(scores[u], axis=-1, keepdims=True) + sw[u] * jnp.sum(qs[u] * n_old[u], axis=-1, keepdims=True)
        hid = num / jnp.maximum(jnp.abs(den), jnp.exp(-m_t[u]))
        hn = hid * lax.rsqrt(jnp.mean(hid * hid, axis=-1, keepdims=True) + NORM_EPS) * norm_ref[:, sl(h)]
        out_ref[g, :, sl(h)] = (jax.nn.sigmoid(o_ref[g * L:(g + 1) * L, sl(h)]) * hn).astype(out_ref.dtype)
    for g in range(G):
        m_s[g] = jnp.concatenate(m_last[g * N_HEADS:(g + 1) * N_HEADS], axis=1)

    @pl.when(c == pl.num_programs(1) - 1)
    def _():
        cout_ref[...] = c_s[...]
        nout_ref[...] = n_s[...]
        mout_ref[...] = m_s[...]


def _mlstm(proj, small_c, small_r, row0, n_seq, n_chunks, L, b_i, b_f, ml_norm, c0, n0, m0, slab=None, prev=None):
    assert C_ML % LANE == 0
    G = _group_size(n_seq, n_chunks, G_MLSTM)
    bias_c = jnp.zeros((1, LANE), F32).at[0, 0:N_HEADS].set(b_i).at[0, N_HEADS:2 * N_HEADS].set(b_f)
    bias_r = jnp.zeros((32, 1), F32).at[0:N_HEADS, 0].set(b_i).at[N_HEADS:2 * N_HEADS, 0].set(b_f)

    def col(j):
        return _seg(L, n_chunks, G, row0, C_ML + j * MIX)[0]

    full2 = lambda shape: pl.BlockSpec(shape, lambda b, c: (0, 0))
    c0, c0_spec = _state_in(c0, (N_HEADS, HEAD_DIM, HEAD_DIM), G)
    c_shape, c_spec = _state_out((N_HEADS, HEAD_DIM, HEAD_DIM), n_seq, slab, G)
    inputs = [proj, proj, proj, proj, small_c, small_r, bias_c, bias_r, ml_norm.reshape(1, MIX),
              c0, n0, m0.reshape(n_seq, 1, N_HEADS)]
    body, extra_in, extra_specs, aliases = _alias_prev(functools.partial(_mlstm_body, L=L, G=G), len(inputs), 1, prev)
    outs = pl.pallas_call(
        body,
        grid=(n_seq // G, n_chunks),
        in_specs=[col(0), col(1), col(2), col(3),
                  _per_chunk(G, n_chunks, (L, 32)), _per_chunk(G, n_chunks, (32, L)),
                  full2((1, LANE)), full2((32, 1)), full2((1, MIX)),
                  c0_spec, _per_seq(G, (N_HEADS, HEAD_DIM)), _per_seq(G, (1, N_HEADS))] + extra_specs,
        out_specs=[_per_chunk(G, n_chunks, (L, MIX)),
                   c_spec, _per_seq(G, (N_HEADS, HEAD_DIM)), _per_seq(G, (1, N_HEADS))],
        out_shape=[jax.ShapeDtypeStruct((n_seq * n_chunks, L, MIX), BF16),
                   c_shape,
                   jax.ShapeDtypeStruct((n_seq, N_HEADS, HEAD_DIM), F32),
                   jax.ShapeDtypeStruct((n_seq, 1, N_HEADS), F32)],
        scratch_shapes=[pltpu.VMEM((G, N_HEADS, HEAD_DIM, HEAD_DIM), F32),
                        pltpu.VMEM((G, N_HEADS, HEAD_DIM), F32),
                        pltpu.VMEM((G, 1, N_HEADS), F32)],
        input_output_aliases=aliases,
        compiler_params=_cparams(("parallel", "arbitrary")),
        name="mlstm",
    )(*inputs, *extra_in)
    out, c_new, n_new, m_new = outs
    return out, c_new, n_new, m_new.reshape(n_seq, N_HEADS)


HIST = 8


def _delta_body(q_ref, k_ref, v_ref, z_ref, gc_ref, gr_ref, bc_ref, br_ref, cw_ref, hist_ref, norm_ref, s0_ref,
                out_ref, sout_ref, cout_ref, s_s, ext_s, *, L, G, off):
    c = pl.program_id(1)

    @pl.when(c == 0)
    def _():
        s_s[...] = s0_ref[...]
        ext_s[:, 0:HIST, :] = hist_ref[...]

    incl, strict = _tri_masks(L)
    tril = incl.astype(F32)
    a_log_c, dt_c = bc_ref[:, 0:N_HEADS], bc_ref[:, N_HEADS:2 * N_HEADS]
    a_log_r, dt_r = br_ref[0:N_HEADS, :], br_ref[N_HEADS:2 * N_HEADS, :]
    base = HIST - (CONV_W - 1)
    qkv, z, beta_c, gcum_c, gcum_r = [], [], [], [], []
    for g in range(G):
        ext_s[g, HIST:HIST + L, 0:MIX] = q_ref[g * L:(g + 1) * L,off:off + MIX]
        ext_s[g, HIST:HIST + L, MIX:2 * MIX] = k_ref[g * L:(g + 1) * L,off:off + MIX]
        ext_s[g, HIST:HIST + L, 2 * MIX:3 * MIX] = v_ref[g * L:(g + 1) * L,off:off + MIX]
        z.append(z_ref[g * L:(g + 1) * L,off:off + MIX])
        y = ext_s[g, base:base + L, :] * cw_ref[0:1, :]
        for j in range(1, CONV_W):
            y = y + ext_s[g, base + j:base + j + L, :] * cw_ref[j:j + 1, :]
        ext_s[g, 0:HIST, :] = ext_s[g, L:L + HIST, :]
        qkv.append(jax.nn.silu(y))
        gc = gc_ref[g]
        gr = gr_ref[g]
        g_c = -jnp.exp(a_log_c) * jax.nn.softplus(gc[:, 2 * N_HEADS:3 * N_HEADS] + dt_c)
        g_r = -jnp.exp(a_log_r) * jax.nn.softplus(gr[2 * N_HEADS:3 * N_HEADS, :] + dt_r)
        beta_c.append(jax.nn.sigmoid(gc[:, 3 * N_HEADS:4 * N_HEADS]))
        gcum_c.append(_dot_exact(tril, g_c))
        gcum_r.append(_dot_exact(g_r, tril.T))
    units = [(g, h) for g in range(G) for h in range(N_HEADS)]
    nu = range(len(units))
    sl = lambda h: slice(h * HEAD_DIM, (h + 1) * HEAD_DIM)
    qs = [qkv[g][:, h * HEAD_DIM:(h + 1) * HEAD_DIM] for g, h in units]
    ks = [qkv[g][:, MIX + h * HEAD_DIM:MIX + (h + 1) * HEAD_DIM] for g, h in units]
    vs = [qkv[g][:, 2 * MIX + h * HEAD_DIM:2 * MIX + (h + 1) * HEAD_DIM] for g, h in units]
    qs = [x * lax.rsqrt(jnp.sum(x * x, axis=-1, keepdims=True) + NORM_EPS) * (HEAD_DIM ** -0.5) for x in qs]
    ks = [x * lax.rsqrt(jnp.sum(x * x, axis=-1, keepdims=True) + NORM_EPS) for x in ks]
    gch = [gcum_c[g][:, h:h + 1] for g, h in units]
    beta = [beta_c[g][:, h:h + 1] for g, h in units]
    eg = [jnp.exp(x) for x in gch]
    dec_incl = [jnp.exp(jnp.where(incl, gch[u] - gcum_r[g][h:h + 1, :], -jnp.inf)) for u, (g, h) in enumerate(units)]
    s_old = [s_s[g, h] for g, h in units]
    prod = [_dot_nt(jnp.concatenate([ks[u], qs[u]], axis=0), ks[u]) for u in nu]
    kk = [x[0:L] for x in prod]
    qk = [x[L:2 * L] for x in prod]
    a_mat = [jnp.where(strict, beta[u] * kk[u] * dec_incl[u], 0.0) for u in nu]
    t_inv = _unit_lower_inverse(a_mat, L)
    uw = [_dot(t_inv[u], jnp.concatenate([vs[u] * beta[u], ks[u] * (beta[u] * eg[u])], axis=1)) for u in nu]
    uu = [x[:, 0:HEAD_DIM] for x in uw]
    w = [x[:, HEAD_DIM:2 * HEAD_DIM] for x in uw]
    ws = [_dot(jnp.concatenate([w[u], qs[u] * eg[u]], axis=0), s_old[u]) for u in nu]
    w_s = [x[0:L] for x in ws]
    q_s = [x[L:2 * L] for x in ws]
    delta = [uu[u] - w_s[u] for u in nu]
    qkd = [_dot(qk[u] * dec_incl[u], delta[u]) for u in nu]
    g_last = [x[L - 1:L, :] for x in gch]
    kd = [_dot_tn(ks[u] * jnp.exp(g_last[u] - gch[u]), delta[u]) for u in nu]
    for u, (g, h) in enumerate(units):
        s_s[g, h] = jnp.exp(g_last[u]) * s_old[u] + kd[u]
        o = q_s[u] + qkd[u]
        on = o * lax.rsqrt(jnp.mean(o * o, axis=-1, keepdims=True) + NORM_EPS) * norm_ref[...]
        out_ref[g, :, sl(h)] = (on * jax.nn.silu(z[g][:, sl(h)])).astype(out_ref.dtype)

    @pl.when(c == pl.num_programs(1) - 1)
    def _():
        sout_ref[...] = s_s[...]
        cout_ref[...] = ext_s[:, HIST + L - (CONV_W - 1):HIST + L, :]


def _hist_rows(state_rows, n_seq, width):
    k = state_rows.shape[1]
    return jnp.concatenate([jnp.zeros((n_seq, HIST - k, width), F32), state_rows], axis=1)


def _delta(proj, small_c, small_r, row0, n_seq, n_chunks, L, conv_w, a_log, dt_bias, dn_norm, conv0, s0,
           slab=None, prev=None):
    bias_c = jnp.zeros((1, LANE), F32).at[0, 0:N_HEADS].set(a_log).at[0, N_HEADS:2 * N_HEADS].set(dt_bias)
    bias_r = jnp.zeros((32, 1), F32).at[0:N_HEADS, 0].set(a_log).at[N_HEADS:2 * N_HEADS, 0].set(dt_bias)
    G = _group_size(n_seq, n_chunks, G_DELTA)
    hist = _hist_rows(conv0, n_seq, 3 * MIX)
    segs = [_seg(L, n_chunks, G, row0, C_DN + j * MIX) for j in range(4)]
    off = segs[0][1]
    assert all(s[1] == off for s in segs)

    full2 = lambda shape: pl.BlockSpec(shape, lambda b, c: (0, 0))
    s0, s0_spec = _state_in(s0, (N_HEADS, HEAD_DIM, HEAD_DIM), G)
    s_shape, s_spec = _state_out((N_HEADS, HEAD_DIM, HEAD_DIM), n_seq, slab, G)
    inputs = [proj, proj, proj, proj, small_c, small_r, bias_c, bias_r, conv_w, hist,
              dn_norm.reshape(1, HEAD_DIM), s0]
    body, extra_in, extra_specs, aliases = _alias_prev(functools.partial(_delta_body, L=L, G=G, off=off),
                                                       len(inputs), 1, prev)
    out, s_new, conv_new = pl.pallas_call(
        body,
        grid=(n_seq // G, n_chunks),
        in_specs=[segs[0][0], segs[1][0], segs[2][0], segs[3][0],
                  _per_chunk(G, n_chunks, (L, 32)), _per_chunk(G, n_chunks, (32, L)),
                  full2((1, LANE)), full2((32, 1)), full2((CONV_W, 3 * MIX)),
                  _per_seq(G, (HIST, 3 * MIX)),
                  full2((1, HEAD_DIM)),
                  s0_spec] + extra_specs,
        out_specs=[_per_chunk(G, n_chunks, (L, MIX)),
                   s_spec,
                   _per_seq(G, (CONV_W - 1, 3 * MIX))],
        out_shape=[jax.ShapeDtypeStruct((n_seq * n_chunks, L, MIX), BF16),
                   s_shape,
                   jax.ShapeDtypeStruct((n_seq, CONV_W - 1, 3 * MIX), F32)],
        scratch_shapes=[pltpu.VMEM((G, N_HEADS, HEAD_DIM, HEAD_DIM), F32),
                        pltpu.VMEM((G, HIST + max(L, HIST), 3 * MIX), F32)],
        input_output_aliases=aliases,
        compiler_params=_cparams(("parallel", "arbitrary")),
        name="deltanet",
    )(*inputs, *extra_in)
    return out, s_new, conv_new


RWX_BLOCK = 768
RW_GATE_PAD = 512


def _rwkv_pairs(pre, rk_ref, lnw_ref, lnb_ref, out_ref, s_s, *, L, G):
    r, k2, v, a, gate, kkk, e_cum, e_prev, e_neg = pre
    pw = 2 * RW_HD
    lo = lax.broadcasted_iota(jnp.int32, (1, pw), 1) < RW_HD
    row2 = lax.broadcasted_iota(jnp.int32, (2 * L, 2 * L), 0)
    col2 = lax.broadcasted_iota(jnp.int32, (2 * L, 2 * L), 1)
    incl_rows = jnp.where(row2 >= L, 1, 0)
    t_idx = row2 - L * incl_rows
    s_idx = jnp.where(col2 >= L, col2 - L, col2)
    mask2 = s_idx < t_idx + incl_rows
    col_lo = lax.broadcasted_iota(jnp.int32, (1, 2 * L), 1) < L
    blk_r = jnp.where(lax.broadcasted_iota(jnp.int32, (pw, pw), 0) >= RW_HD, 1, 0)
    blk_c = jnp.where(lax.broadcasted_iota(jnp.int32, (pw, pw), 1) >= RW_HD, 1, 0)
    bd_state = blk_r == blk_c

    def split_rows(x, sel):
        return jnp.concatenate([jnp.where(sel, x, 0.0), jnp.where(sel, 0.0, x)], axis=0)

    def seg_sum(x):
        s_lo = jnp.sum(jnp.where(lo, x, 0.0), axis=-1, keepdims=True)
        s_hi = jnp.sum(jnp.where(lo, 0.0, x), axis=-1, keepdims=True)
        return jnp.where(lo, s_lo, s_hi)

    units = [(g, p) for g in range(G) for p in range(RW_HEADS // 2)]
    nu = range(len(units))
    sl = lambda p: slice(p * pw, (p + 1) * pw)
    kkn = [kkk[g][:, sl(p)] for g, p in units]
    kkn = [x * lax.rsqrt(seg_sum(x * x) + NORM_EPS) for x in kkn]
    rs = [r[g][:, sl(p)] for g, p in units]
    ks = [k2[g][:, sl(p)] for g, p in units]
    vs = [v[g][:, sl(p)] for g, p in units]
    k_til = [kkn[u] * e_prev[g][:, sl(p)] for u, (g, p) in enumerate(units)]
    r_til = [rs[u] * e_cum[g][:, sl(p)] for u, (g, p) in enumerate(units)]
    k_hat = [ks[u] * e_neg[g][:, sl(p)] for u, (g, p) in enumerate(units)]
    b_hat = [kkn[u] * a[g][:, sl(p)] * e_neg[g][:, sl(p)] for u, (g, p) in enumerate(units)]
    s_old = [s_s[g, p] for g, p in units]
    kr = [jnp.concatenate([k_til[u], r_til[u]], axis=0) for u in nu]
    p_k = [jnp.where(mask2, _dot_nt(kr[u], split_rows(k_hat[u], lo)), 0.0) for u in nu]
    p_b = [jnp.where(mask2, _dot_nt(kr[u], split_rows(b_hat[u], lo)), 0.0) for u in nu]
    t_bd = _unit_lower_inverse([split_rows(x[0:L], col_lo) for x in p_b], L)
    t_pair = [x[0:L] + x[L:2 * L] for x in t_bd]
    kv = [_dot(p_k[u], split_rows(vs[u], lo)) for u in nu]
    ksd = [_dot_nt(kr[u], s_old[u]) for u in nu]
    inner = [ksd[u][0:L] + kv[u][0:L] for u in nu]
    y_part = [ksd[u][L:2 * L] + kv[u][L:2 * L] for u in nu]
    uu = [_dot(t_pair[u], split_rows(inner[u], lo)) for u in nu]
    bu = [_dot(p_b[u][L:2 * L], split_rows(uu[u], lo)) for u in nu]
    el = [e_cum[g][L - 1:L, sl(p)] for g, p in units]
    upd = [_dot_tn(jnp.concatenate([vs[u], -uu[u]], axis=0),
                   jnp.concatenate([k_hat[u] * el[u], b_hat[u] * el[u]], axis=0)) for u in nu]
    for u, (g, p) in enumerate(units):
        s_s[g, p] = s_old[u] * el[u] + jnp.where(bd_state, upd[u], 0.0)
        y = y_part[u] - bu[u]
        mean = seg_sum(y) * (1.0 / RW_HD)
        d = y - mean
        var = seg_sum(d * d) * (1.0 / RW_HD)
        y = d * lax.rsqrt(var + RW_LN_EPS) * lnw_ref[:, sl(p)] + lnb_ref[:, sl(p)]
        y = y + seg_sum(rs[u] * ks[u] * rk_ref[:, sl(p)]) * vs[u]
        out_ref[g, :, sl(p)] = (y * gate[g][:, sl(p)]).astype(out_ref.dtype)


def _rwkv_body(r_ref, k_ref, v_ref, x_ref, hist_ref, mu_ref, w0_ref, w2_ref, a0_ref, a2_ref, g2_ref,
               kk_ref, ka_ref, rk_ref, lnw_ref, lnb_ref, s0_ref,
               out_ref, sout_ref, shout_ref, s_s, ext_s, *, L, G, off):
    c = pl.program_id(1)
    width = 3 * MIX + RWX_BLOCK

    @pl.when(c == 0)
    def _():
        s_s[...] = jnp.zeros(s_s.shape, F32)
        for h in range(RW_HEADS):
            o = (h % 2) * RW_HD
            s_s[:, h // 2, o:o + RW_HD, o:o + RW_HD] = s0_ref[:, h]
        ext_s[:, 0:HIST, :] = hist_ref[...]

    incl, _ = _tri_masks(L)
    r, k2, v, a, gate, kkk, e_cum, e_prev, e_neg = [], [], [], [], [], [], [], [], []
    for g in range(G):
        ext_s[g, HIST:HIST + L, 0:MIX] = r_ref[g * L:(g + 1) * L,off:off + MIX]
        ext_s[g, HIST:HIST + L, MIX:2 * MIX] = k_ref[g * L:(g + 1) * L,off:off + MIX]
        ext_s[g, HIST:HIST + L, 2 * MIX:3 * MIX] = v_ref[g * L:(g + 1) * L,off:off + MIX]
        ext_s[g, HIST:HIST + L, 3 * MIX:3 * MIX + RW_CODES] = x_ref[g * L:(g + 1) * L,off:off + RW_CODES]
        ext_s[g, HIST:HIST + L, 3 * MIX + RW_CODES:width] = jnp.zeros((L, width - 3 * MIX - RW_CODES), F32)
        cur = ext_s[g, HIST:HIST + L, :]
        prev = ext_s[g, HIST - 1:HIST - 1 + L, :]
        ext_s[g, 0:HIST, :] = ext_s[g, L:L + HIST, :]
        z = cur + (prev - cur) * mu_ref[...]
        k = z[:, MIX:2 * MIX]
        zw = z[:, 3 * MIX + RWX_W:3 * MIX + RWX_W + RW_DECAY_RANK]
        za = z[:, 3 * MIX + RWX_A:3 * MIX + RWX_A + RW_A_RANK]
        zg = z[:, 3 * MIX + RWX_G:3 * MIX + RWX_G + RW_GATE_PAD]
        w_log = -jax.nn.softplus(-(w0_ref[...] + _dot(jnp.tanh(zw), w2_ref[...]))) - 0.5
        a_g = jax.nn.sigmoid(a0_ref[...] + _dot(za, a2_ref[...]))
        log_decay = -jnp.exp(w_log)
        cum = _dot_exact(incl.astype(F32), log_decay)
        r.append(z[:, 0:MIX])
        v.append(z[:, 2 * MIX:3 * MIX])
        a.append(a_g)
        gate.append(_dot(jax.nn.sigmoid(zg), g2_ref[...]))
        kkk.append(k * kk_ref[...])
        k2.append(k * (1.0 + (a_g - 1.0) * ka_ref[...]))
        e_cum.append(jnp.exp(cum))
        e_prev.append(jnp.exp(cum - log_decay))
        e_neg.append(jnp.exp(-cum))
    pre = (r, k2, v, a, gate, kkk, e_cum, e_prev, e_neg)
    _rwkv_pairs(pre, rk_ref, lnw_ref, lnb_ref, out_ref, s_s, L=L, G=G)

    @pl.when(c == pl.num_programs(1) - 1)
    def _():
        for h in range(RW_HEADS):
            o = (h % 2) * RW_HD
            sout_ref[:, h] = s_s[:, h // 2, o:o + RW_HD, o:o + RW_HD]
        shout_ref[...] = ext_s[:, HIST + L - 1:HIST + L, 0:_RW_COLS]


def _rw_cols(vec):
    pad = jnp.zeros((RWX_BLOCK - (_RW_COLS - 3 * MIX),), F32)
    return jnp.concatenate([vec, pad]).reshape(1, 3 * MIX + RWX_BLOCK)


def _rwkv(proj, row0, n_seq, n_chunks, L, p, shift0, s0, slab=None, prev=None):
    width = 3 * MIX + RWX_BLOCK
    hist = jnp.concatenate([jnp.zeros((n_seq, HIST - 1, width), F32),
                            jnp.pad(shift0, ((0, 0), (0, width - _RW_COLS)))[:, None, :]], axis=1)
    g2 = jnp.pad(p['rw_g2'], ((0, RW_GATE_PAD - RW_GATE_RANK), (0, 0))).astype(BF16)
    row = lambda vec: vec.reshape(1, MIX)
    G = _group_size(n_seq, n_chunks, G_RWKV)
    segs = [_seg(L, n_chunks, G, row0, C_RW + j * MIX) for j in range(3)] + [_seg(L, n_chunks, G, row0, C_RWX, RW_CODES)]
    off = segs[0][1]
    assert all(s[1] == off for s in segs)

    full2 = lambda shape: pl.BlockSpec(shape, lambda b, c: (0, 0))
    s0, s0_spec = _state_in(s0, (RW_HEADS, RW_HD, RW_HD), G)
    s_shape, s_spec = _state_out((RW_HEADS, RW_HD, RW_HD), n_seq, slab, G)
    inputs = [proj, proj, proj, proj, hist, _rw_cols(p['rw_mu']),
              row(p['rw_w0']), p['rw_w2'].astype(BF16), row(p['rw_a0']), p['rw_a2'].astype(BF16), g2,
              row(p['rw_k_k']), row(p['rw_k_a']), row(p['rw_r_k'].reshape(-1)), row(p['rw_ln_w']), row(p['rw_ln_b']),
              s0]
    body, extra_in, extra_specs, aliases = _alias_prev(functools.partial(_rwkv_body, L=L, G=G, off=off),
                                                       len(inputs), 1, prev)
    out, s_new, shift_new = pl.pallas_call(
        body,
        grid=(n_seq // G, n_chunks),
        in_specs=[segs[0][0], segs[1][0], segs[2][0], segs[3][0],
                  _per_seq(G, (HIST, width)),
                  full2((1, width)),
                  full2((1, MIX)), full2((RW_DECAY_RANK, MIX)),
                  full2((1, MIX)), full2((RW_A_RANK, MIX)),
                  full2((RW_GATE_PAD, MIX)),
                  full2((1, MIX)), full2((1, MIX)), full2((1, MIX)), full2((1, MIX)), full2((1, MIX)),
                  s0_spec] + extra_specs,
        out_specs=[_per_chunk(G, n_chunks, (L, MIX)),
                   s_spec,
                   _per_seq(G, (1, _RW_COLS))],
        out_shape=[jax.ShapeDtypeStruct((n_seq * n_chunks, L, MIX), BF16),
                   s_shape,
                   jax.ShapeDtypeStruct((n_seq, 1, _RW_COLS), F32)],
        scratch_shapes=[pltpu.VMEM((G, RW_HEADS // 2, 2 * RW_HD, 2 * RW_HD), F32),
                        pltpu.VMEM((G, HIST + max(L, HIST), width), F32)],
        input_output_aliases=aliases,
        compiler_params=_cparams(("parallel", "arbitrary")),
        name="rwkv7",
    )(*inputs, *extra_in)
    return out, s_new, shift_new.reshape(n_seq, _RW_COLS)


def _shift_rows(x, d, fill):
    rows = lax.broadcasted_iota(jnp.int32, x.shape, 0)
    return jnp.where(rows >= d, pltpu.roll(x, d, axis=0), fill)


def _lru_body(x_ref, g_ref, hist_ref, cw_ref, cb_ref, wa_ref, ba_ref, wx_ref, bx_ref, lam_ref, h0_ref,
              out_ref, hout_ref, cout_ref, h_s, ext_s, xc_s, *, L, G, off):
    c = pl.program_id(1)

    @pl.when(c == 0)
    def _():
        h_s[...] = h0_ref[...]
        ext_s[:, 0:HIST, :] = hist_ref[...]

    base = HIST - (CONV_W - 1)
    for g in range(G):
        ext_s[g, HIST:HIST + L, :] = x_ref[g * L:(g + 1) * L,off:off + MIX]
        conv = ext_s[g, base:base + L, :] * cw_ref[0:1, :]
        for j in range(1, CONV_W):
            conv = conv + ext_s[g, base + j:base + j + L, :] * cw_ref[j:j + 1, :]
        ext_s[g, 0:HIST, :] = ext_s[g, L:L + HIST, :]
        xc_s[g * L:(g + 1) * L, :] = conv + cb_ref[...]
    xc = xc_s[...]
    r = jax.nn.sigmoid(_dot(xc, wa_ref[...]) + ba_ref[...])
    i = jax.nn.sigmoid(_dot(xc, wx_ref[...]) + bx_ref[...])
    log_a = -LRU_C * r * jax.nn.softplus(-lam_ref[...])
    a = jnp.exp(log_a)
    u_all = jnp.sqrt(-jnp.tanh(log_a) * (a * a + 1.0)) * (i * xc)
    for g in range(G):
        a_g = a[g * L:(g + 1) * L, :]
        u_g = u_all[g * L:(g + 1) * L, :]
        h_prev = h_s[g]
        if L % 8 == 0:
            acc_a, acc_h = a_g, u_g
            d = 1
            while d < L:
                acc_h = acc_a * _shift_rows(acc_h, d, 0.0) + acc_h
                acc_a = acc_a * _shift_rows(acc_a, d, 1.0)
                d *= 2
            hs = acc_a * h_prev + acc_h
            h_s[g] = hs[L - 1:L, :]
        else:
            rows = []
            for t in range(L):
                h_prev = a_g[t:t + 1, :] * h_prev + u_g[t:t + 1, :]
                rows.append(h_prev)
            hs = jnp.concatenate(rows, axis=0)
            h_s[g] = h_prev
        out_ref[g] = (hs * jax.nn.gelu(g_ref[g * L:(g + 1) * L, off:off + MIX])).astype(out_ref.dtype)

    @pl.when(c == pl.num_programs(1) - 1)
    def _():
        hout_ref[...] = h_s[...]
        cout_ref[...] = ext_s[:, HIST + L - (CONV_W - 1):HIST + L, :]


def _block_diag(w):
    nb, bs, _ = w.shape
    eye = jnp.eye(nb, dtype=w.dtype)
    return (eye[:, None, :, None] * w[:, :, None, :]).reshape(nb * bs, nb * bs)


def _lru(proj, row0, n_seq, n_chunks, L, p, conv0, h0):
    hist = _hist_rows(conv0, n_seq, MIX)
    row = lambda vec: vec.reshape(1, MIX)
    full2 = lambda shape: pl.BlockSpec(shape, lambda b, c: (0, 0))
    G = _group_size(n_seq, n_chunks, G_LRU)
    segs = [_seg(L, n_chunks, G, row0, C_LRU + j * MIX) for j in range(2)]
    off = segs[0][1]
    assert all(s[1] == off for s in segs)
    out, h_new, conv_new = pl.pallas_call(
        functools.partial(_lru_body, L=L, G=G, off=off),
        grid=(n_seq // G, n_chunks),
        in_specs=[segs[0][0], segs[1][0],
                  _per_seq(G, (HIST, MIX)),
                  full2((CONV_W, MIX)), full2((1, MIX)),
                  full2((MIX, MIX)), full2((1, MIX)), full2((MIX, MIX)), full2((1, MIX)), full2((1, MIX)),
                  _per_seq(G, (1, MIX))],
        out_specs=[_per_chunk(G, n_chunks, (L, MIX)),
                   _per_seq(G, (1, MIX)),
                   _per_seq(G, (CONV_W - 1, MIX))],
        out_shape=[jax.ShapeDtypeStruct((n_seq * n_chunks, L, MIX), BF16),
                   jax.ShapeDtypeStruct((n_seq, 1, MIX), F32),
                   jax.ShapeDtypeStruct((n_seq, CONV_W - 1, MIX), F32)],
        scratch_shapes=[pltpu.VMEM((G, 1, MIX), F32),
                        pltpu.VMEM((G, HIST + max(L, HIST), MIX), F32),
                        pltpu.VMEM((G * L, MIX), F32)],
        compiler_params=_cparams(("parallel", "arbitrary")),
        name="rglru",
    )(proj, proj, hist, p['lru_conv_w'], row(p['lru_conv_b']),
      _block_diag(p['lru_wa']).astype(BF16), row(p['lru_ba']),
      _block_diag(p['lru_wx']).astype(BF16), row(p['lru_bx']), row(p['lru_lambda']),
      h0.reshape(n_seq, 1, MIX))
    return out, h_new.reshape(n_seq, MIX), conv_new


L_MLSTM = 256
L_DELTA = 64
L_RWKV = 64
L_LRU = 256


def _row_tile(m, cap):
    return max(t for t in range(16, cap + 1, 16) if m % t == 0)


def _chunk_len(t, want):
    return math.gcd(t, want)


def _small_views(small, L):
    m = small.shape[0]
    small_c = small.reshape(m // L, L, 32)
    small_r = jnp.swapaxes(small_c, 1, 2)
    return small_c, small_r


def _mixers(proj, row0, n_seq, t_len, p, st, slab, prev):
    ml_c, ml_n, ml_m, dn_s, dn_conv, lru_h, lru_conv, rw_s, rw_shift = st
    prev_c, prev_dn, prev_rw = prev if prev is not None else (None, None, None)
    rows = slice(row0, row0 + n_seq * t_len)
    small = jnp.concatenate([proj[rows, C_ML + 4 * MIX:C_ML + 4 * MIX + 2 * N_HEADS],
                             proj[rows, C_DN + 4 * MIX:C_DN + 4 * MIX + 2 * N_HEADS]], axis=1)

    def chunks(want):
        L = _chunk_len(t_len, want)
        return L, t_len // L

    L, nch = chunks(L_MLSTM)
    sc, sr = _small_views(small, L)
    out_ml, ml_c, ml_n, ml_m = _mlstm(proj, sc, sr, row0, n_seq, nch, L, p['ml_b_i'], p['ml_b_f'], p['ml_norm'],
                                      ml_c, ml_n, ml_m, slab, prev_c)
    L, nch = chunks(L_DELTA)
    sc, sr = _small_views(small, L)
    out_dn, dn_s, dn_conv = _delta(proj, sc, sr, row0, n_seq, nch, L, p['dn_conv'], p['dn_A_log'], p['dn_dt_bias'],
                                   p['dn_norm'], dn_conv, dn_s, slab, prev_dn)
    L, nch = chunks(L_LRU)
    out_lru, lru_h, lru_conv = _lru(proj, row0, n_seq, nch, L, p, lru_conv, lru_h)
    L, nch = chunks(L_RWKV)
    out_rw, rw_s, rw_shift = _rwkv(proj, row0, n_seq, nch, L, p, rw_shift, rw_s, slab, prev_rw)
    outs = tuple(o.reshape(n_seq * t_len, MIX) for o in (out_ml, out_dn, out_lru, out_rw))
    return outs, (ml_c, ml_n, ml_m, dn_s, dn_conv, lru_h, lru_conv, rw_s, rw_shift)


_STATE_SHAPES = ((N_HEADS, HEAD_DIM, HEAD_DIM), (N_HEADS, HEAD_DIM), (N_HEADS,), (N_HEADS, HEAD_DIM, HEAD_DIM),
                 (CONV_W - 1, 3 * MIX), (MIX,), (CONV_W - 1, MIX), (RW_HEADS, RW_HD, RW_HD), (_RW_COLS,))
_BIG_STATES = (0, 3, 7)


def kernel(x_prompt, x_sample, state_mlstm_C, state_mlstm_n, state_mlstm_m, state_delta_S, state_delta_conv, state_rglru_h, state_rglru_conv, state_rwkv_S, state_rwkv_shift, norm_mix, w_in, ml_b_i, ml_b_f, ml_norm, dn_conv, dn_A_log, dn_dt_bias, dn_norm, lru_conv_w, lru_conv_b, lru_wa, lru_ba, lru_wx, lru_bx, lru_lambda, rw_mu, rw_w0, rw_w2, rw_a0, rw_a2, rw_g2, rw_k_k, rw_k_a, rw_r_k, rw_ln_w, rw_ln_b, w_branch, w_out, norm_ffn, w_ffn_gate, w_ffn_up, w_ffn_down, norm_final):
    params = {
        'ml_b_i': ml_b_i, 'ml_b_f': ml_b_f, 'ml_norm': ml_norm,
        'dn_conv': dn_conv, 'dn_A_log': dn_A_log, 'dn_dt_bias': dn_dt_bias, 'dn_norm': dn_norm,
        'lru_conv_w': lru_conv_w, 'lru_conv_b': lru_conv_b, 'lru_wa': lru_wa, 'lru_ba': lru_ba,
        'lru_wx': lru_wx, 'lru_bx': lru_bx, 'lru_lambda': lru_lambda,
        'rw_mu': rw_mu, 'rw_w0': rw_w0, 'rw_w2': rw_w2, 'rw_a0': rw_a0, 'rw_a2': rw_a2, 'rw_g2': rw_g2,
        'rw_k_k': rw_k_k, 'rw_k_a': rw_k_a, 'rw_r_k': rw_r_k, 'rw_ln_w': rw_ln_w, 'rw_ln_b': rw_ln_b,
    }
    depth = w_in.shape[0]
    bp, tp, d = x_prompt.shape
    bs, ts, _ = x_sample.shape
    n_p, n_s = bp * tp, bs * ts
    m = n_p + n_s
    x = jnp.concatenate([x_prompt.reshape(n_p, d), x_sample.reshape(n_s, d)], axis=0)
    tm = _row_tile(m, 1088)
    tm_panel = _row_tile(m, 2176)
    tm_norm = _row_tile(m, 544)
    tm_group = _row_tile(math.gcd(n_p, n_s), 512)
    sample_states = (state_mlstm_C, state_mlstm_n, state_mlstm_m, state_delta_S, state_delta_conv,
                     state_rglru_h, state_rglru_conv, state_rwkv_S, state_rwkv_shift)
    big_p = big_s = None
    new_p = [[] for _ in _STATE_SHAPES]
    new_s = [[] for _ in _STATE_SHAPES]
    wd_b = w_ffn_down.astype(BF16)
    w_out_b = w_out.astype(BF16)
    w_branch_b = w_branch.astype(BF16)
    for l in range(depth):
        p = {name: w[l] for name, w in params.items()}
        h = _rmsnorm(x, norm_mix[l], BF16, tm_norm)
        proj = _matmul(h, w_in, l, tm_panel, 512)
        st_p = tuple(jnp.zeros((bp,) + shp, F32) for shp in _STATE_SHAPES)
        st_s = tuple((s, l) if s.ndim == 5 else s[l] for s in sample_states)
        outs_p, st_p = _mixers(proj, 0, bp, tp, p, st_p, (l, depth), big_p)
        outs_s, st_s = _mixers(proj, n_p, bs, ts, p, st_s, (l, depth), big_s)
        big_p = tuple(st_p[i] for i in _BIG_STATES)
        big_s = tuple(st_s[i] for i in _BIG_STATES)
        for lst, s in zip(new_p, st_p):
            lst.append(s)
        for lst, s in zip(new_s, st_s):
            lst.append(s)
        mix = _merge(proj, outs_p, outs_s, w_branch_b, l, tm_group, 1024)
        x = _matmul_residual(mix, w_out_b, l, x, tm, 1024, 2048)
        h2 = _rmsnorm(x, norm_ffn[l], BF16, tm_norm)
        act = _ffn_up(h2, w_ffn_gate, w_ffn_up, l, tm_panel, 256)
        x = _matmul_residual(act, wd_b, l, x, tm // 2, 512, D_FF)
    y_prompt, y_sample = _rmsnorm_split(x, norm_final, n_p, tm_group)
    y_prompt = y_prompt.reshape(bp, tp, d)
    y_sample = y_sample.reshape(bs, ts, d)
    def stacked(per_layer):
        return tuple(lst[-1] if i in _BIG_STATES else jnp.stack(lst) for i, lst in enumerate(per_layer))

    return (y_prompt, y_sample) + stacked(new_p) + stacked(new_s)
```

```python
import functools
import math

import jax
import jax.numpy as jnp
from jax import lax
from jax.experimental import pallas as pl
from jax.experimental.pallas import tpu as pltpu

F32 = jnp.float32
BF16 = jnp.bfloat16

D_MODEL = 4096
MIX = D_MODEL // 4
HEAD_DIM = 128
N_HEADS = MIX // HEAD_DIM
RW_HD = 64
RW_HEADS = MIX // RW_HD
RW_DECAY_RANK = 128
RW_A_RANK = 128
RW_GATE_RANK = 480
CONV_W = 4
LRU_C = 8.0
D_FF = -(-(8 * D_MODEL) // (3 * 256)) * 256
NORM_EPS = 1e-6
RW_LN_EPS = 64e-5

_ML_COLS = 4 * MIX + 2 * N_HEADS
_DN_COLS = 4 * MIX + 2 * N_HEADS
_LRU_COLS = 2 * MIX
_RW_COLS = 3 * MIX + RW_DECAY_RANK + RW_A_RANK + RW_GATE_RANK
C_ML = 0
C_DN = _ML_COLS
C_LRU = C_DN + _DN_COLS
C_RW = C_LRU + _LRU_COLS
C_RWX = C_RW + 3 * MIX
C_GATE = C_RW + _RW_COLS
N_PROJ = C_GATE + 4 * D_MODEL
RW_CODES = _RW_COLS - 3 * MIX
RWX_W = 0
RWX_A = RW_DECAY_RANK
RWX_G = RW_DECAY_RANK + RW_A_RANK

VMEM_LIMIT_BYTES = 56 * 1024 * 1024
LANE = 128


def _cparams(sem):
    return pltpu.CompilerParams(dimension_semantics=sem, vmem_limit_bytes=VMEM_LIMIT_BYTES)


def _rmsnorm_body(x_ref, g_ref, o_ref):
    x = x_ref[...]
    y = x * lax.rsqrt(jnp.mean(x * x, axis=-1, keepdims=True) + NORM_EPS)
    o_ref[...] = (y * g_ref[...]).astype(o_ref.dtype)


def _rmsnorm(x, g, out_dtype, tm):
    m, d = x.shape
    return pl.pallas_call(
        _rmsnorm_body,
        grid=(m // tm,),
        in_specs=[pl.BlockSpec((tm, d), lambda i: (i, 0)), pl.BlockSpec((1, d), lambda i: (0, 0))],
        out_specs=pl.BlockSpec((tm, d), lambda i: (i, 0)),
        out_shape=jax.ShapeDtypeStruct((m, d), out_dtype),
        compiler_params=_cparams(("parallel",)),
        name="rmsnorm",
    )(x, g.reshape(1, d))


def _rmsnorm_split_body(x_ref, g_ref, oa_ref, ob_ref, *, tiles_a):
    x = x_ref[...]
    y = x * lax.rsqrt(jnp.mean(x * x, axis=-1, keepdims=True) + NORM_EPS) * g_ref[...]
    i = pl.program_id(0)

    @pl.when(i < tiles_a)
    def _():
        oa_ref[...] = y

    @pl.when(i >= tiles_a)
    def _():
        ob_ref[...] = y


def _rmsnorm_split(x, g, rows_a, tm):
    m, d = x.shape
    rows_b = m - rows_a
    assert rows_a % tm == 0 and rows_b % tm == 0
    tiles_a = rows_a // tm
    return pl.pallas_call(
        functools.partial(_rmsnorm_split_body, tiles_a=tiles_a),
        grid=(m // tm,),
        in_specs=[pl.BlockSpec((tm, d), lambda i: (i, 0)), pl.BlockSpec((1, d), lambda i: (0, 0))],
        out_specs=[pl.BlockSpec((tm, d), lambda i: (jnp.minimum(i, tiles_a - 1), 0)),
                   pl.BlockSpec((tm, d), lambda i: (jnp.maximum(i - tiles_a, 0), 0))],
        out_shape=[jax.ShapeDtypeStruct((rows_a, d), F32), jax.ShapeDtypeStruct((rows_b, d), F32)],
        compiler_params=_cparams(("arbitrary",)),
        name="rmsnorm_split",
    )(x, g.reshape(1, d))


def _mm_body(x_ref, w_ref, o_ref):
    o_ref[...] = jnp.dot(x_ref[...], w_ref[0].astype(BF16), preferred_element_type=F32).astype(o_ref.dtype)


def _matmul(x, w, layer, tm, tn, out_dtype=F32):
    m, k = x.shape
    n = w.shape[2]
    return pl.pallas_call(
        _mm_body,
        grid=(m // tm, pl.cdiv(n, tn)),
        in_specs=[pl.BlockSpec((tm, k), lambda i, j: (i, 0), pipeline_mode=pl.Buffered(1)),
                  pl.BlockSpec((1, k, tn), lambda i, j: (layer, 0, j))],
        out_specs=pl.BlockSpec((tm, tn), lambda i, j: (i, j)),
        out_shape=jax.ShapeDtypeStruct((m, n), out_dtype),
        compiler_params=_cparams(("parallel", "arbitrary")),
        name="matmul",
    )(x, w)


def _mm_res_body(x_ref, w_ref, r_ref, o_ref, acc_ref):
    kk = pl.program_id(2)

    @pl.when(kk == 0)
    def _():
        acc_ref[...] = r_ref[...]

    acc_ref[...] += jnp.dot(x_ref[...], w_ref[0], preferred_element_type=F32)

    @pl.when(kk == pl.num_programs(2) - 1)
    def _():
        o_ref[...] = acc_ref[...]


def _matmul_residual(x, w, layer, res, tm, tn, tk):
    m, k = x.shape
    n = w.shape[2]
    return pl.pallas_call(
        _mm_res_body,
        grid=(n // tn, m // tm, k // tk),
        in_specs=[pl.BlockSpec((tm, tk), lambda j, i, kk: (i, kk)),
                  pl.BlockSpec((1, tk, tn), lambda j, i, kk: (layer, kk, j)),
                  pl.BlockSpec((tm, tn), lambda j, i, kk: (i, j))],
        out_specs=pl.BlockSpec((tm, tn), lambda j, i, kk: (i, j)),
        out_shape=jax.ShapeDtypeStruct((m, n), F32),
        scratch_shapes=[pltpu.VMEM((tm, tn), F32)],
        compiler_params=_cparams(("parallel", "parallel", "arbitrary")),
        name="matmul_residual",
    )(x, w, res)


def _ffn_up_body(x_ref, wg_ref, wu_ref, o_ref):
    x = x_ref[...]
    g = jnp.dot(x, wg_ref[0].astype(BF16), preferred_element_type=F32)
    u = jnp.dot(x, wu_ref[0].astype(BF16), preferred_element_type=F32)
    o_ref[...] = (jax.nn.silu(g) * u).astype(o_ref.dtype)


def _ffn_up(x, wg, wu, layer, tm, tn):
    m, k = x.shape
    n = wg.shape[2]
    assert n % tn == 0
    wspec = pl.BlockSpec((1, k, tn), lambda i, j: (layer, 0, j))
    return pl.pallas_call(
        _ffn_up_body,
        grid=(m // tm, n // tn),
        in_specs=[pl.BlockSpec((tm, k), lambda i, j: (i, 0), pipeline_mode=pl.Buffered(1)), wspec, wspec],
        out_specs=pl.BlockSpec((tm, tn), lambda i, j: (i, j)),
        out_shape=jax.ShapeDtypeStruct((m, n), BF16),
        compiler_params=_cparams(("parallel", "arbitrary")),
        name="ffn_up",
    )(x, wg, wu)


def _merge_body(*refs, nb, tiles_a):
    g_refs, a_refs, b_refs = refs[:nb], refs[nb:2 * nb], refs[2 * nb:3 * nb]
    w_ref, o_ref = refs[3 * nb], refs[3 * nb + 1]

    def mix(x_refs):
        acc = jax.nn.sigmoid(g_refs[0][...]) * jnp.dot(x_refs[0][...], w_ref[0, 0], preferred_element_type=F32)
        for n in range(1, nb):
            acc = acc + jax.nn.sigmoid(g_refs[n][...]) * jnp.dot(x_refs[n][...], w_ref[0, n],
                                                                 preferred_element_type=F32)
        o_ref[...] = acc.astype(o_ref.dtype)

    i = pl.program_id(1)
    pl.when(i < tiles_a)(lambda: mix(a_refs))
    pl.when(i >= tiles_a)(lambda: mix(b_refs))


def _merge(proj, branches_a, branches_b, wb, layer, tm, tn):
    nb = len(branches_a)
    rows_a, k = branches_a[0].shape
    rows_b = branches_b[0].shape[0]
    assert rows_a % tm == 0 and rows_b % tm == 0
    tiles_a, tiles_b = rows_a // tm, rows_b // tm
    n = wb.shape[3]

    def gate_spec(b):
        return pl.BlockSpec((pl.Element(tm), pl.Element(tn)),
                            lambda j, i: (i * tm, (C_GATE // LANE + b * (n // LANE) + j * (tn // LANE)) * LANE))

    a_spec = pl.BlockSpec((tm, k), lambda j, i: (jnp.minimum(i, tiles_a - 1), 0))
    b_spec = pl.BlockSpec((tm, k), lambda j, i: (jnp.maximum(i - tiles_a, 0), 0))
    return pl.pallas_call(
        functools.partial(_merge_body, nb=nb, tiles_a=tiles_a),
        grid=(n // tn, tiles_a + tiles_b),
        in_specs=[gate_spec(b) for b in range(nb)] + [a_spec] * nb + [b_spec] * nb
                 + [pl.BlockSpec((1, nb, k, tn), lambda j, i: (layer, 0, 0, j))],
        out_specs=pl.BlockSpec((tm, tn), lambda j, i: (i, j)),
        out_shape=jax.ShapeDtypeStruct((rows_a + rows_b, n), BF16),
        compiler_params=_cparams(("parallel", "arbitrary")),
        name="merge",
    )(*([proj] * nb), *branches_a, *branches_b, wb)


G_MLSTM = 2
G_DELTA = 4
G_RWKV = 8
G_LRU = 8


def _group_size(n_seq, n_chunks, want):
    return math.gcd(n_seq, want) if n_chunks == 1 else 1


SUBLANES = 8


def _seg(L, n_chunks, G, row0, col, width=MIX):
    start = col // LANE * LANE
    off = col - start
    win = -(-(off + width) // LANE) * LANE
    rows = G * L
    assert rows % SUBLANES == 0 and row0 % SUBLANES == 0
    spec = pl.BlockSpec((pl.Element(rows), pl.Element(win)),
                        lambda b, c: ((row0 // SUBLANES + (b * n_chunks + c) * (rows // SUBLANES)) * SUBLANES, start))
    return spec, off


def _per_seq(G, tail):
    zeros = (0,) * len(tail)
    return pl.BlockSpec((G,) + tail, lambda b, c: (b,) + zeros)


def _per_chunk(G, n_chunks, tail):
    zeros = (0,) * len(tail)
    return pl.BlockSpec((G,) + tail, lambda b, c: (b * n_chunks + c,) + zeros)


def _state_in(state, tail, G):
    zeros = (0,) * len(tail)
    if isinstance(state, tuple):
        arr, layer = state
        return arr, pl.BlockSpec((None, G) + tail, lambda b, c: (layer, b) + zeros)
    return state, _per_seq(G, tail)


def _state_out(tail, n_seq, slab, G):
    zeros = (0,) * len(tail)
    if slab is None:
        return jax.ShapeDtypeStruct((n_seq,) + tail, F32), _per_seq(G, tail)
    layer, depth = slab
    return (jax.ShapeDtypeStruct((depth, n_seq) + tail, F32),
            pl.BlockSpec((None, G) + tail, lambda b, c: (layer, b) + zeros))


def _alias_prev(body, n_inputs, out_index, prev):
    if prev is None:
        return body, [], [], {}

    def aliased_body(*refs):
        return body(*refs[:n_inputs], *refs[n_inputs + 1:])

    return aliased_body, [prev], [pl.BlockSpec(memory_space=pl.ANY)], {n_inputs: out_index}


def _tri_masks(L):
    row = lax.broadcasted_iota(jnp.int32, (L, L), 0)
    col = lax.broadcasted_iota(jnp.int32, (L, L), 1)
    return col <= row, col < row


def _dot(a, b):
    return jnp.dot(a.astype(BF16), b.astype(BF16), preferred_element_type=F32)


def _dot_nt(a, b):
    return lax.dot_general(a.astype(BF16), b.astype(BF16), (((1,), (1,)), ((), ())), preferred_element_type=F32)


def _dot_tn(a, b):
    return lax.dot_general(a.astype(BF16), b.astype(BF16), (((0,), (0,)), ((), ())), preferred_element_type=F32)


def _dot_exact(a, b):
    return jnp.dot(a, b, precision=lax.Precision.HIGHEST, preferred_element_type=F32)


def _unit_lower_inverse(ns, L):
    size = ns[0].shape[0]
    eye = (lax.broadcasted_iota(jnp.int32, (size, size), 0)
           == lax.broadcasted_iota(jnp.int32, (size, size), 1)).astype(F32)
    ms = [-n for n in ns]
    ts = [eye + m for m in ms]
    span = 2
    while span < L:
        ms = [_dot(m, m) for m in ms]
        ts = [t + _dot(t, m) for t, m in zip(ts, ms)]
        span *= 2
    return ts


def _mlstm_body(q_ref, k_ref, v_ref, o_ref, gc_ref, gr_ref, bc_ref, br_ref, norm_ref, c0_ref, n0_ref, m0_ref,
                out_ref, cout_ref, nout_ref, mout_ref, c_s, n_s, m_s, *, L, G):
    c = pl.program_id(1)

    @pl.when(c == 0)
    def _():
        c_s[...] = c0_ref[...]
        n_s[...] = n0_ref[...]
        m_s[...] = m0_ref[...]

    incl, _ = _tri_masks(L)
    tril = incl.astype(F32)
    ig_c, ig_r, b_c, b_r = [], [], [], []
    for g in range(G):
        gc = gc_ref[g]
        gr = gr_ref[g]
        ig_c.append(gc[:, 0:N_HEADS] + bc_ref[:, 0:N_HEADS])
        ig_r.append(gr[0:N_HEADS, :] + br_ref[0:N_HEADS, :])
        lf_c = jax.nn.log_sigmoid(gc[:, N_HEADS:2 * N_HEADS] + bc_ref[:, N_HEADS:2 * N_HEADS])
        lf_r = jax.nn.log_sigmoid(gr[N_HEADS:2 * N_HEADS, :] + br_ref[N_HEADS:2 * N_HEADS, :])
        b_c.append(_dot_exact(tril, lf_c))
        b_r.append(_dot_exact(lf_r, tril.T))
    units = [(g, h) for g in range(G) for h in range(N_HEADS)]
    sl = lambda h: slice(h * HEAD_DIM, (h + 1) * HEAD_DIM)
    qs = [q_ref[g * L:(g + 1) * L,sl(h)] * (HEAD_DIM ** -0.5) for g, h in units]
    ks = [k_ref[g * L:(g + 1) * L,sl(h)] for g, h in units]
    vs = [v_ref[g * L:(g + 1) * L,sl(h)] for g, h in units]
    c_old = [c_s[g, h] for g, h in units]
    n_old = [n_s[g, h:h + 1, :] for g, h in units]
    qk = [_dot_nt(q, k) for q, k in zip(qs, ks)]
    qc = [_dot(q, co) for q, co in zip(qs, c_old)]
    bcs = [b_c[g][:, h:h + 1] for g, h in units]
    m_prev = [m_s[g][:, h:h + 1] for g, h in units]
    log_d = [jnp.where(incl, bcs[u] - b_r[g][h:h + 1, :] + ig_r[g][h:h + 1, :], -jnp.inf)
             for u, (g, h) in enumerate(units)]
    state_w = [b + m for b, m in zip(bcs, m_prev)]
    m_t = [jnp.maximum(s, jnp.max(ld, axis=-1, keepdims=True)) for s, ld in zip(state_w, log_d)]
    scores = [a * jnp.exp(ld - mt) for a, ld, mt in zip(qk, log_d, m_t)]
    sw = [jnp.exp(s - mt) for s, mt in zip(state_w, m_t)]
    sv = [_dot(s, v) for s, v in zip(scores, vs)]
    m_last = [mt[L - 1:L, :] for mt in m_t]
    b_last = [b[L - 1:L, :] for b in bcs]
    kw = [ks[u] * jnp.exp(b_last[u] - bcs[u] + ig_c[g][:, h:h + 1] - m_last[u]) for u, (g, h) in enumerate(units)]
    kv = [_dot_tn(a, v) for a, v in zip(kw, vs)]
    for u, (g, h) in enumerate(units):
        decay = jnp.exp(b_last[u] + m_prev[u] - m_last[u])
        c_s[g, h] = decay * c_old[u] + kv[u]
        n_s[g, h:h + 1, :] = decay * n_old[u] + jnp.sum(kw[u], axis=0, keepdims=True)
        num = sv[u] + sw[u] * qc[u]
        den = jnp.sum(scores[u], axis=-1, keepdims=True) + sw[u] * jnp.sum(qs[u] * n_old[u], axis=-1, keepdims=True)
        hid = num / jnp.maximum(jnp.abs(den), jnp.exp(-m_t[u]))
        hn = hid * lax.rsqrt(jnp.mean(hid * hid, axis=-1, keepdims=True) + NORM_EPS) * norm_ref[:, sl(h)]
        out_ref[g, :, sl(h)] = (jax.nn.sigmoid(o_ref[g * L:(g + 1) * L, sl(h)]) * hn).astype(out_ref.dtype)
    for g in range(G):
        m_s[g] = jnp.concatenate(m_last[g * N_HEADS:(g + 1) * N_HEADS], axis=1)

    @pl.when(c == pl.num_programs(1) - 1)
    def _():
        cout_ref[...] = c_s[...]
        nout_ref[...] = n_s[...]
        mout_ref[...] = m_s[...]


def _mlstm(proj, small_c, small_r, row0, n_seq, n_chunks, L, b_i, b_f, ml_norm, c0, n0, m0, slab=None, prev=None):
    assert C_ML % LANE == 0
    G = _group_size(n_seq, n_chunks, G_MLSTM)
    bias_c = jnp.zeros((1, LANE), F32).at[0, 0:N_HEADS].set(b_i).at[0, N_HEADS:2 * N_HEADS].set(b_f)
    bias_r = jnp.zeros((32, 1), F32).at[0:N_HEADS, 0].set(b_i).at[N_HEADS:2 * N_HEADS, 0].set(b_f)

    def col(j):
        return _seg(L, n_chunks, G, row0, C_ML + j * MIX)[0]

    full2 = lambda shape: pl.BlockSpec(shape, lambda b, c: (0, 0))
    c0, c0_spec = _state_in(c0, (N_HEADS, HEAD_DIM, HEAD_DIM), G)
    c_shape, c_spec = _state_out((N_HEADS, HEAD_DIM, HEAD_DIM), n_seq, slab, G)
    inputs = [proj, proj, proj, proj, small_c, small_r, bias_c, bias_r, ml_norm.reshape(1, MIX),
              c0, n0, m0.reshape(n_seq, 1, N_HEADS)]
    body, extra_in, extra_specs, aliases = _alias_prev(functools.partial(_mlstm_body, L=L, G=G), len(inputs), 1, prev)
    outs = pl.pallas_call(
        body,
        grid=(n_seq // G, n_chunks),
        in_specs=[col(0), col(1), col(2), col(3),
                  _per_chunk(G, n_chunks, (L, 32)), _per_chunk(G, n_chunks, (32, L)),
                  full2((1, LANE)), full2((32, 1)), full2((1, MIX)),
                  c0_spec, _per_seq(G, (N_HEADS, HEAD_DIM)), _per_seq(G, (1, N_HEADS))] + extra_specs,
        out_specs=[_per_chunk(G, n_chunks, (L, MIX)),
                   c_spec, _per_seq(G, (N_HEADS, HEAD_DIM)), _per_seq(G, (1, N_HEADS))],
        out_shape=[jax.ShapeDtypeStruct((n_seq * n_chunks, L, MIX), BF16),
                   c_shape,
                   jax.ShapeDtypeStruct((n_seq, N_HEADS, HEAD_DIM), F32),
                   jax.ShapeDtypeStruct((n_seq, 1, N_HEADS), F32)],
        scratch_shapes=[pltpu.VMEM((G, N_HEADS, HEAD_DIM, HEAD_DIM), F32),
                        pltpu.VMEM((G, N_HEADS, HEAD_DIM), F32),
                        pltpu.VMEM((G, 1, N_HEADS), F32)],
        input_output_aliases=aliases,
        compiler_params=_cparams(("parallel", "arbitrary")),
        name="mlstm",
    )(*inputs, *extra_in)
    out, c_new, n_new, m_new = outs
    return out, c_new, n_new, m_new.reshape(n_seq, N_HEADS)


HIST = 8


def _causal_conv(ext_s, g, cw_ref, L):
    if L % SUBLANES == 0:
        win = ext_s[g, 0:HIST + L, :]
        y = win[HIST:HIST + L, :] * cw_ref[CONV_W - 1:CONV_W, :]
        for j in range(CONV_W - 1):
            y = y + pltpu.roll(win, CONV_W - 1 - j, axis=0)[HIST:HIST + L, :] * cw_ref[j:j + 1, :]
        return y
    base = HIST - (CONV_W - 1)
    y = ext_s[g, base:base + L, :] * cw_ref[0:1, :]
    for j in range(1, CONV_W):
        y = y + ext_s[g, base + j:base + j + L, :] * cw_ref[j:j + 1, :]
    return y


def _delta_body(q_ref, k_ref, v_ref, z_ref, gc_ref, gr_ref, bc_ref, br_ref, cw_ref, hist_ref, norm_ref, s0_ref,
                out_ref, sout_ref, cout_ref, s_s, ext_s, *, L, G, off):
    c = pl.program_id(1)

    @pl.when(c == 0)
    def _():
        s_s[...] = s0_ref[...]
        ext_s[:, 0:HIST, :] = hist_ref[...]

    incl, strict = _tri_masks(L)
    tril = incl.astype(F32)
    a_log_c, dt_c = bc_ref[:, 0:N_HEADS], bc_ref[:, N_HEADS:2 * N_HEADS]
    a_log_r, dt_r = br_ref[0:N_HEADS, :], br_ref[N_HEADS:2 * N_HEADS, :]
    qkv, z, beta_c, gcum_c, gcum_r = [], [], [], [], []
    for g in range(G):
        ext_s[g, HIST:HIST + L, 0:MIX] = q_ref[g * L:(g + 1) * L,off:off + MIX]
        ext_s[g, HIST:HIST + L, MIX:2 * MIX] = k_ref[g * L:(g + 1) * L,off:off + MIX]
        ext_s[g, HIST:HIST + L, 2 * MIX:3 * MIX] = v_ref[g * L:(g + 1) * L,off:off + MIX]
        z.append(z_ref[g * L:(g + 1) * L,off:off + MIX])
        y = _causal_conv(ext_s, g, cw_ref, L)
        ext_s[g, 0:HIST, :] = ext_s[g, L:L + HIST, :]
        qkv.append(jax.nn.silu(y))
        gc = gc_ref[g]
        gr = gr_ref[g]
        g_c = -jnp.exp(a_log_c) * jax.nn.softplus(gc[:, 2 * N_HEADS:3 * N_HEADS] + dt_c)
        g_r = -jnp.exp(a_log_r) * jax.nn.softplus(gr[2 * N_HEADS:3 * N_HEADS, :] + dt_r)
        beta_c.append(jax.nn.sigmoid(gc[:, 3 * N_HEADS:4 * N_HEADS]))
        gcum_c.append(_dot_exact(tril, g_c))
        gcum_r.append(_dot_exact(g_r, tril.T))
    units = [(g, h) for g in range(G) for h in range(N_HEADS)]
    nu = range(len(units))
    sl = lambda h: slice(h * HEAD_DIM, (h + 1) * HEAD_DIM)
    qs = [qkv[g][:, h * HEAD_DIM:(h + 1) * HEAD_DIM] for g, h in units]
    ks = [qkv[g][:, MIX + h * HEAD_DIM:MIX + (h + 1) * HEAD_DIM] for g, h in units]
    vs = [qkv[g][:, 2 * MIX + h * HEAD_DIM:2 * MIX + (h + 1) * HEAD_DIM] for g, h in units]
    qs = [x * lax.rsqrt(jnp.sum(x * x, axis=-1, keepdims=True) + NORM_EPS) * (HEAD_DIM ** -0.5) for x in qs]
    ks = [x * lax.rsqrt(jnp.sum(x * x, axis=-1, keepdims=True) + NORM_EPS) for x in ks]
    gch = [gcum_c[g][:, h:h + 1] for g, h in units]
    beta = [beta_c[g][:, h:h + 1] for g, h in units]
    eg = [jnp.exp(x) for x in gch]
    dec_incl = [jnp.exp(jnp.where(incl, gch[u] - gcum_r[g][h:h + 1, :], -jnp.inf)) for u, (g, h) in enumerate(units)]
    s_old = [s_s[g, h] for g, h in units]
    prod = [_dot_nt(jnp.concatenate([ks[u], qs[u]], axis=0), ks[u]) for u in nu]
    kk = [x[0:L] for x in prod]
    qk = [x[L:2 * L] for x in prod]
    a_mat = [jnp.where(strict, beta[u] * kk[u] * dec_incl[u], 0.0) for u in nu]
    t_inv = _unit_lower_inverse(a_mat, L)
    uw = [_dot(t_inv[u], jnp.concatenate([vs[u] * beta[u], ks[u] * (beta[u] * eg[u])], axis=1)) for u in nu]
    uu = [x[:, 0:HEAD_DIM] for x in uw]
    w = [x[:, HEAD_DIM:2 * HEAD_DIM] for x in uw]
    ws = [_dot(jnp.concatenate([w[u], qs[u] * eg[u]], axis=0), s_old[u]) for u in nu]
    w_s = [x[0:L] for x in ws]
    q_s = [x[L:2 * L] for x in ws]
    delta = [uu[u] - w_s[u] for u in nu]
    qkd = [_dot(qk[u] * dec_incl[u], delta[u]) for u in nu]
    g_last = [x[L - 1:L, :] for x in gch]
    kd = [_dot_tn(ks[u] * jnp.exp(g_last[u] - gch[u]), delta[u]) for u in nu]
    for u, (g, h) in enumerate(units):
        s_s[g, h] = jnp.exp(g_last[u]) * s_old[u] + kd[u]
        o = q_s[u] + qkd[u]
        on = o * lax.rsqrt(jnp.mean(o * o, axis=-1, keepdims=True) + NORM_EPS) * norm_ref[...]
        out_ref[g, :, sl(h)] = (on * jax.nn.silu(z[g][:, sl(h)])).astype(out_ref.dtype)

    @pl.when(c == pl.num_programs(1) - 1)
    def _():
        sout_ref[...] = s_s[...]
        cout_ref[...] = ext_s[:, HIST + L - (CONV_W - 1):HIST + L, :]


def _hist_rows(state_rows, n_seq, width):
    k = state_rows.shape[1]
    return jnp.concatenate([jnp.zeros((n_seq, HIST - k, width), F32), state_rows], axis=1)


def _delta(proj, small_c, small_r, row0, n_seq, n_chunks, L, conv_w, a_log, dt_bias, dn_norm, conv0, s0,
           slab=None, prev=None):
    bias_c = jnp.zeros((1, LANE), F32).at[0, 0:N_HEADS].set(a_log).at[0, N_HEADS:2 * N_HEADS].set(dt_bias)
    bias_r = jnp.zeros((32, 1), F32).at[0:N_HEADS, 0].set(a_log).at[N_HEADS:2 * N_HEADS, 0].set(dt_bias)
    G = _group_size(n_seq, n_chunks, G_DELTA)
    hist = _hist_rows(conv0, n_seq, 3 * MIX)
    segs = [_seg(L, n_chunks, G, row0, C_DN + j * MIX) for j in range(4)]
    off = segs[0][1]
    assert all(s[1] == off for s in segs)

    full2 = lambda shape: pl.BlockSpec(shape, lambda b, c: (0, 0))
    s0, s0_spec = _state_in(s0, (N_HEADS, HEAD_DIM, HEAD_DIM), G)
    s_shape, s_spec = _state_out((N_HEADS, HEAD_DIM, HEAD_DIM), n_seq, slab, G)
    inputs = [proj, proj, proj, proj, small_c, small_r, bias_c, bias_r, conv_w, hist,
              dn_norm.reshape(1, HEAD_DIM), s0]
    body, extra_in, extra_specs, aliases = _alias_prev(functools.partial(_delta_body, L=L, G=G, off=off),
                                                       len(inputs), 1, prev)
    out, s_new, conv_new = pl.pallas_call(
        body,
        grid=(n_seq // G, n_chunks),
        in_specs=[segs[0][0], segs[1][0], segs[2][0], segs[3][0],
                  _per_chunk(G, n_chunks, (L, 32)), _per_chunk(G, n_chunks, (32, L)),
                  full2((1, LANE)), full2((32, 1)), full2((CONV_W, 3 * MIX)),
                  _per_seq(G, (HIST, 3 * MIX)),
                  full2((1, HEAD_DIM)),
                  s0_spec] + extra_specs,
        out_specs=[_per_chunk(G, n_chunks, (L, MIX)),
                   s_spec,
                   _per_seq(G, (CONV_W - 1, 3 * MIX))],
        out_shape=[jax.ShapeDtypeStruct((n_seq * n_chunks, L, MIX), BF16),
                   s_shape,
                   jax.ShapeDtypeStruct((n_seq, CONV_W - 1, 3 * MIX), F32)],
        scratch_shapes=[pltpu.VMEM((G, N_HEADS, HEAD_DIM, HEAD_DIM), F32),
                        pltpu.VMEM((G, HIST + max(L, HIST), 3 * MIX), F32)],
        input_output_aliases=aliases,
        compiler_params=_cparams(("parallel", "arbitrary")),
        name="deltanet",
    )(*inputs, *extra_in)
    return out, s_new, conv_new


RWX_BLOCK = 768
RW_GATE_PAD = 512


def _rwkv_pairs(pre, rk_ref, lnw_ref, lnb_ref, out_ref, s_s, *, L, G):
    r, k2, v, a, gate, kkk, e_cum, e_prev, e_neg = pre
    pw = 2 * RW_HD
    lo = lax.broadcasted_iota(jnp.int32, (1, pw), 1) < RW_HD
    row2 = lax.broadcasted_iota(jnp.int32, (2 * L, 2 * L), 0)
    col2 = lax.broadcasted_iota(jnp.int32, (2 * L, 2 * L), 1)
    incl_rows = jnp.where(row2 >= L, 1, 0)
    t_idx = row2 - L * incl_rows
    s_idx = jnp.where(col2 >= L, col2 - L, col2)
    mask2 = s_idx < t_idx + incl_rows
    col_lo = lax.broadcasted_iota(jnp.int32, (1, 2 * L), 1) < L
    blk_r = jnp.where(lax.broadcasted_iota(jnp.int32, (pw, pw), 0) >= RW_HD, 1, 0)
    blk_c = jnp.where(lax.broadcasted_iota(jnp.int32, (pw, pw), 1) >= RW_HD, 1, 0)
    bd_state = blk_r == blk_c

    def split_rows(x, sel):
        return jnp.concatenate([jnp.where(sel, x, 0.0), jnp.where(sel, 0.0, x)], axis=0)

    def seg_sum(x):
        s_lo = jnp.sum(jnp.where(lo, x, 0.0), axis=-1, keepdims=True)
        s_hi = jnp.sum(jnp.where(lo, 0.0, x), axis=-1, keepdims=True)
        return jnp.where(lo, s_lo, s_hi)

    units = [(g, p) for g in range(G) for p in range(RW_HEADS // 2)]
    nu = range(len(units))
    sl = lambda p: slice(p * pw, (p + 1) * pw)
    kkn = [kkk[g][:, sl(p)] for g, p in units]
    kkn = [x * lax.rsqrt(seg_sum(x * x) + NORM_EPS) for x in kkn]
    rs = [r[g][:, sl(p)] for g, p in units]
    ks = [k2[g][:, sl(p)] for g, p in units]
    vs = [v[g][:, sl(p)] for g, p in units]
    k_til = [kkn[u] * e_prev[g][:, sl(p)] for u, (g, p) in enumerate(units)]
    r_til = [rs[u] * e_cum[g][:, sl(p)] for u, (g, p) in enumerate(units)]
    k_hat = [ks[u] * e_neg[g][:, sl(p)] for u, (g, p) in enumerate(units)]
    b_hat = [kkn[u] * a[g][:, sl(p)] * e_neg[g][:, sl(p)] for u, (g, p) in enumerate(units)]
    s_old = [s_s[g, p] for g, p in units]
    kr = [jnp.concatenate([k_til[u], r_til[u]], axis=0) for u in nu]
    p_k = [jnp.where(mask2, _dot_nt(kr[u], split_rows(k_hat[u], lo)), 0.0) for u in nu]
    p_b = [jnp.where(mask2, _dot_nt(kr[u], split_rows(b_hat[u], lo)), 0.0) for u in nu]
    t_bd = _unit_lower_inverse([split_rows(x[0:L], col_lo) for x in p_b], L)
    t_pair = [x[0:L] + x[L:2 * L] for x in t_bd]
    kv = [_dot(p_k[u], split_rows(vs[u], lo)) for u in nu]
    ksd = [_dot_nt(kr[u], s_old[u]) for u in nu]
    inner = [ksd[u][0:L] + kv[u][0:L] for u in nu]
    y_part = [ksd[u][L:2 * L] + kv[u][L:2 * L] for u in nu]
    uu = [_dot(t_pair[u], split_rows(inner[u], lo)) for u in nu]
    bu = [_dot(p_b[u][L:2 * L], split_rows(uu[u], lo)) for u in nu]
    el = [e_cum[g][L - 1:L, sl(p)] for g, p in units]
    upd = [_dot_tn(jnp.concatenate([vs[u], -uu[u]], axis=0),
                   jnp.concatenate([k_hat[u] * el[u], b_hat[u] * el[u]], axis=0)) for u in nu]
    for u, (g, p) in enumerate(units):
        s_s[g, p] = s_old[u] * el[u] + jnp.where(bd_state, upd[u], 0.0)
        y = y_part[u] - bu[u]
        mean = seg_sum(y) * (1.0 / RW_HD)
        d = y - mean
        var = seg_sum(d * d) * (1.0 / RW_HD)
        y = d * lax.rsqrt(var + RW_LN_EPS) * lnw_ref[:, sl(p)] + lnb_ref[:, sl(p)]
        y = y + seg_sum(rs[u] * ks[u] * rk_ref[:, sl(p)]) * vs[u]
        out_ref[g, :, sl(p)] = (y * gate[g][:, sl(p)]).astype(out_ref.dtype)


def _rwkv_body(r_ref, k_ref, v_ref, x_ref, hist_ref, mu_ref, w0_ref, w2_ref, a0_ref, a2_ref, g2_ref,
               kk_ref, ka_ref, rk_ref, lnw_ref, lnb_ref, s0_ref,
               out_ref, sout_ref, shout_ref, s_s, ext_s, *, L, G, off):
    c = pl.program_id(1)
    width = 3 * MIX + RWX_BLOCK

    @pl.when(c == 0)
    def _():
        s_s[...] = jnp.zeros(s_s.shape, F32)
        for h in range(RW_HEADS):
            o = (h % 2) * RW_HD
            s_s[:, h // 2, o:o + RW_HD, o:o + RW_HD] = s0_ref[:, h]
        ext_s[:, 0:HIST, :] = hist_ref[...]

    incl, _ = _tri_masks(L)
    r, k2, v, a, gate, kkk, e_cum, e_prev, e_neg = [], [], [], [], [], [], [], [], []
    for g in range(G):
        ext_s[g, HIST:HIST + L, 0:MIX] = r_ref[g * L:(g + 1) * L,off:off + MIX]
        ext_s[g, HIST:HIST + L, MIX:2 * MIX] = k_ref[g * L:(g + 1) * L,off:off + MIX]
        ext_s[g, HIST:HIST + L, 2 * MIX:3 * MIX] = v_ref[g * L:(g + 1) * L,off:off + MIX]
        ext_s[g, HIST:HIST + L, 3 * MIX:3 * MIX + RW_CODES] = x_ref[g * L:(g + 1) * L,off:off + RW_CODES]
        ext_s[g, HIST:HIST + L, 3 * MIX + RW_CODES:width] = jnp.zeros((L, width - 3 * MIX - RW_CODES), F32)
        cur = ext_s[g, HIST:HIST + L, :]
        if L % SUBLANES == 0:
            prev = pltpu.roll(ext_s[g, 0:HIST + L, :], 1, axis=0)[HIST:HIST + L, :]
        else:
            prev = ext_s[g, HIST - 1:HIST - 1 + L, :]
        ext_s[g, 0:HIST, :] = ext_s[g, L:L + HIST, :]
        z = cur + (prev - cur) * mu_ref[...]
        k = z[:, MIX:2 * MIX]
        zw = z[:, 3 * MIX + RWX_W:3 * MIX + RWX_W + RW_DECAY_RANK]
        za = z[:, 3 * MIX + RWX_A:3 * MIX + RWX_A + RW_A_RANK]
        zg = z[:, 3 * MIX + RWX_G:3 * MIX + RWX_G + RW_GATE_PAD]
        w_log = -jax.nn.softplus(-(w0_ref[...] + _dot(jnp.tanh(zw), w2_ref[...]))) - 0.5
        a_g = jax.nn.sigmoid(a0_ref[...] + _dot(za, a2_ref[...]))
        log_decay = -jnp.exp(w_log)
        cum = _dot_exact(incl.astype(F32), log_decay)
        r.append(z[:, 0:MIX])
        v.append(z[:, 2 * MIX:3 * MIX])
        a.append(a_g)
        gate.append(_dot(jax.nn.sigmoid(zg), g2_ref[...]))
        kkk.append(k * kk_ref[...])
        k2.append(k * (1.0 + (a_g - 1.0) * ka_ref[...]))
        e_cum.append(jnp.exp(cum))
        e_prev.append(jnp.exp(cum - log_decay))
        e_neg.append(jnp.exp(-cum))
    pre = (r, k2, v, a, gate, kkk, e_cum, e_prev, e_neg)
    _rwkv_pairs(pre, rk_ref, lnw_ref, lnb_ref, out_ref, s_s, L=L, G=G)

    @pl.when(c == pl.num_programs(1) - 1)
    def _():
        for h in range(RW_HEADS):
            o = (h % 2) * RW_HD
            sout_ref[:, h] = s_s[:, h // 2, o:o + RW_HD, o:o + RW_HD]
        shout_ref[...] = ext_s[:, HIST + L - 1:HIST + L, 0:_RW_COLS]


def _rw_cols(vec):
    pad = jnp.zeros((RWX_BLOCK - (_RW_COLS - 3 * MIX),), F32)
    return jnp.concatenate([vec, pad]).reshape(1, 3 * MIX + RWX_BLOCK)


def _rwkv(proj, row0, n_seq, n_chunks, L, p, shift0, s0, slab=None, prev=None):
    width = 3 * MIX + RWX_BLOCK
    hist = jnp.concatenate([jnp.zeros((n_seq, HIST - 1, width), F32),
                            jnp.pad(shift0, ((0, 0), (0, width - _RW_COLS)))[:, None, :]], axis=1)
    g2 = jnp.pad(p['rw_g2'], ((0, RW_GATE_PAD - RW_GATE_RANK), (0, 0))).astype(BF16)
    row = lambda vec: vec.reshape(1, MIX)
    G = _group_size(n_seq, n_chunks, G_RWKV)
    segs = [_seg(L, n_chunks, G, row0, C_RW + j * MIX) for j in range(3)] + [_seg(L, n_chunks, G, row0, C_RWX, RW_CODES)]
    off = segs[0][1]
    assert all(s[1] == off for s in segs)

    full2 = lambda shape: pl.BlockSpec(shape, lambda b, c: (0, 0))
    s0, s0_spec = _state_in(s0, (RW_HEADS, RW_HD, RW_HD), G)
    s_shape, s_spec = _state_out((RW_HEADS, RW_HD, RW_HD), n_seq, slab, G)
    inputs = [proj, proj, proj, proj, hist, _rw_cols(p['rw_mu']),
              row(p['rw_w0']), p['rw_w2'].astype(BF16), row(p['rw_a0']), p['rw_a2'].astype(BF16), g2,
              row(p['rw_k_k']), row(p['rw_k_a']), row(p['rw_r_k'].reshape(-1)), row(p['rw_ln_w']), row(p['rw_ln_b']),
              s0]
    body, extra_in, extra_specs, aliases = _alias_prev(functools.partial(_rwkv_body, L=L, G=G, off=off),
                                                       len(inputs), 1, prev)
    out, s_new, shift_new = pl.pallas_call(
        body,
        grid=(n_seq // G, n_chunks),
        in_specs=[segs[0][0], segs[1][0], segs[2][0], segs[3][0],
                  _per_seq(G, (HIST, width)),
                  full2((1, width)),
                  full2((1, MIX)), full2((RW_DECAY_RANK, MIX)),
                  full2((1, MIX)), full2((RW_A_RANK, MIX)),
                  full2((RW_GATE_PAD, MIX)),
                  full2((1, MIX)), full2((1, MIX)), full2((1, MIX)), full2((1, MIX)), full2((1, MIX)),
                  s0_spec] + extra_specs,
        out_specs=[_per_chunk(G, n_chunks, (L, MIX)),
                   s_spec,
                   _per_seq(G, (1, _RW_COLS))],
        out_shape=[jax.ShapeDtypeStruct((n_seq * n_chunks, L, MIX), BF16),
                   s_shape,
                   jax.ShapeDtypeStruct((n_seq, 1, _RW_COLS), F32)],
        scratch_shapes=[pltpu.VMEM((G, RW_HEADS // 2, 2 * RW_HD, 2 * RW_HD), F32),
                        pltpu.VMEM((G, HIST + max(L, HIST), width), F32)],
        input_output_aliases=aliases,
        compiler_params=_cparams(("parallel", "arbitrary")),
        name="rwkv7",
    )(*inputs, *extra_in)
    return out, s_new, shift_new.reshape(n_seq, _RW_COLS)


def _shift_rows(x, d, fill):
    rows = lax.broadcasted_iota(jnp.int32, x.shape, 0)
    return jnp.where(jnp.bitwise_and(rows, SUBLANES - 1) >= d, pltpu.roll(x, d, axis=0), fill)


def _lru_body(x_ref, g_ref, hist_ref, cw_ref, cb_ref, wa_ref, ba_ref, wx_ref, bx_ref, lam_ref, h0_ref,
              out_ref, hout_ref, cout_ref, h_s, ext_s, xc_s, *, L, G, off):
    c = pl.program_id(1)

    @pl.when(c == 0)
    def _():
        h_s[...] = h0_ref[...]
        ext_s[:, 0:HIST, :] = hist_ref[...]

    for g in range(G):
        ext_s[g, HIST:HIST + L, :] = x_ref[g * L:(g + 1) * L,off:off + MIX]
        conv = _causal_conv(ext_s, g, cw_ref, L)
        ext_s[g, 0:HIST, :] = ext_s[g, L:L + HIST, :]
        xc_s[g * L:(g + 1) * L, :] = conv + cb_ref[...]
    xc = xc_s[...]
    r = jax.nn.sigmoid(_dot(xc, wa_ref[...]) + ba_ref[...])
    i = jax.nn.sigmoid(_dot(xc, wx_ref[...]) + bx_ref[...])
    log_a = -LRU_C * r * jax.nn.softplus(-lam_ref[...])
    a = jnp.exp(log_a)
    u_all = jnp.sqrt(-jnp.tanh(log_a) * (a * a + 1.0)) * (i * xc)
    for g in range(G):
        a_g = a[g * L:(g + 1) * L, :]
        u_g = u_all[g * L:(g + 1) * L, :]
        h_prev = h_s[g]
        if L % 8 == 0:
            acc_a, acc_h = a_g, u_g
            d = 1
            while d < SUBLANES:
                acc_h = acc_a * _shift_rows(acc_h, d, 0.0) + acc_h
                acc_a = acc_a * _shift_rows(acc_a, d, 1.0)
                d *= 2
            groups = []
            for r0 in range(0, L, SUBLANES):
                hs_grp = acc_a[r0:r0 + SUBLANES, :] * h_prev + acc_h[r0:r0 + SUBLANES, :]
                h_prev = hs_grp[SUBLANES - 1:SUBLANES, :]
                groups.append(hs_grp)
            hs = jnp.concatenate(groups, axis=0)
            h_s[g] = h_prev
        else:
            rows = []
            for t in range(L):
                h_prev = a_g[t:t + 1, :] * h_prev + u_g[t:t + 1, :]
                rows.append(h_prev)
            hs = jnp.concatenate(rows, axis=0)
            h_s[g] = h_prev
        out_ref[g] = (hs * jax.nn.gelu(g_ref[g * L:(g + 1) * L, off:off + MIX])).astype(out_ref.dtype)

    @pl.when(c == pl.num_programs(1) - 1)
    def _():
        hout_ref[...] = h_s[...]
        cout_ref[...] = ext_s[:, HIST + L - (CONV_W - 1):HIST + L, :]


def _block_diag(w):
    nb, bs, _ = w.shape
    eye = jnp.eye(nb, dtype=w.dtype)
    return (eye[:, None, :, None] * w[:, :, None, :]).reshape(nb * bs, nb * bs)


def _lru(proj, row0, n_seq, n_chunks, L, p, conv0, h0):
    hist = _hist_rows(conv0, n_seq, MIX)
    row = lambda vec: vec.reshape(1, MIX)
    full2 = lambda shape: pl.BlockSpec(shape, lambda b, c: (0, 0))
    G = _group_size(n_seq, n_chunks, G_LRU)
    segs = [_seg(L, n_chunks, G, row0, C_LRU + j * MIX) for j in range(2)]
    off = segs[0][1]
    assert all(s[1] == off for s in segs)
    out, h_new, conv_new = pl.pallas_call(
        functools.partial(_lru_body, L=L, G=G, off=off),
        grid=(n_seq // G, n_chunks),
        in_specs=[segs[0][0], segs[1][0],
                  _per_seq(G, (HIST, MIX)),
                  full2((CONV_W, MIX)), full2((1, MIX)),
                  full2((MIX, MIX)), full2((1, MIX)), full2((MIX, MIX)), full2((1, MIX)), full2((1, MIX)),
                  _per_seq(G, (1, MIX))],
        out_specs=[_per_chunk(G, n_chunks, (L, MIX)),
                   _per_seq(G, (1, MIX)),
                   _per_seq(G, (CONV_W - 1, MIX))],
        out_shape=[jax.ShapeDtypeStruct((n_seq * n_chunks, L, MIX), BF16),
                   jax.ShapeDtypeStruct((n_seq, 1, MIX), F32),
                   jax.ShapeDtypeStruct((n_seq, CONV_W - 1, MIX), F32)],
        scratch_shapes=[pltpu.VMEM((G, 1, MIX), F32),
                        pltpu.VMEM((G, HIST + max(L, HIST), MIX), F32),
                        pltpu.VMEM((G * L, MIX), F32)],
        compiler_params=_cparams(("parallel", "arbitrary")),
        name="rglru",
    )(proj, proj, hist, p['lru_conv_w'], row(p['lru_conv_b']),
      _block_diag(p['lru_wa']).astype(BF16), row(p['lru_ba']),
      _block_diag(p['lru_wx']).astype(BF16), row(p['lru_bx']), row(p['lru_lambda']),
      h0.reshape(n_seq, 1, MIX))
    return out, h_new.reshape(n_seq, MIX), conv_new


L_MLSTM = 256
L_DELTA = 64
L_RWKV = 64
L_LRU = 256


def _row_tile(m, cap):
    return max(t for t in range(16, cap + 1, 16) if m % t == 0)


def _chunk_len(t, want):
    return math.gcd(t, want)


def _small_views(small, L):
    m = small.shape[0]
    small_c = small.reshape(m // L, L, 32)
    small_r = jnp.swapaxes(small_c, 1, 2)
    return small_c, small_r


def _mixers(proj, row0, n_seq, t_len, p, st, slab, prev):
    ml_c, ml_n, ml_m, dn_s, dn_conv, lru_h, lru_conv, rw_s, rw_shift = st
    prev_c, prev_dn, prev_rw = prev if prev is not None else (None, None, None)
    rows = slice(row0, row0 + n_seq * t_len)
    small = jnp.concatenate([proj[rows, C_ML + 4 * MIX:C_ML + 4 * MIX + 2 * N_HEADS],
                             proj[rows, C_DN + 4 * MIX:C_DN + 4 * MIX + 2 * N_HEADS]], axis=1)

    def chunks(want):
        L = _chunk_len(t_len, want)
        return L, t_len // L

    L, nch = chunks(L_MLSTM)
    sc, sr = _small_views(small, L)
    out_ml, ml_c, ml_n, ml_m = _mlstm(proj, sc, sr, row0, n_seq, nch, L, p['ml_b_i'], p['ml_b_f'], p['ml_norm'],
                                      ml_c, ml_n, ml_m, slab, prev_c)
    L, nch = chunks(L_DELTA)
    sc, sr = _small_views(small, L)
    out_dn, dn_s, dn_conv = _delta(proj, sc, sr, row0, n_seq, nch, L, p['dn_conv'], p['dn_A_log'], p['dn_dt_bias'],
                                   p['dn_norm'], dn_conv, dn_s, slab, prev_dn)
    L, nch = chunks(L_LRU)
    out_lru, lru_h, lru_conv = _lru(proj, row0, n_seq, nch, L, p, lru_conv, lru_h)
    L, nch = chunks(L_RWKV)
    out_rw, rw_s, rw_shift = _rwkv(proj, row0, n_seq, nch, L, p, rw_shift, rw_s, slab, prev_rw)
    outs = tuple(o.reshape(n_seq * t_len, MIX) for o in (out_ml, out_dn, out_lru, out_rw))
    return outs, (ml_c, ml_n, ml_m, dn_s, dn_conv, lru_h, lru_conv, rw_s, rw_shift)


_STATE_SHAPES = ((N_HEADS, HEAD_DIM, HEAD_DIM), (N_HEADS, HEAD_DIM), (N_HEADS,), (N_HEADS, HEAD_DIM, HEAD_DIM),
                 (CONV_W - 1, 3 * MIX), (MIX,), (CONV_W - 1, MIX), (RW_HEADS, RW_HD, RW_HD), (_RW_COLS,))
_BIG_STATES = (0, 3, 7)


def kernel(x_prompt, x_sample, state_mlstm_C, state_mlstm_n, state_mlstm_m, state_delta_S, state_delta_conv, state_rglru_h, state_rglru_conv, state_rwkv_S, state_rwkv_shift, norm_mix, w_in, ml_b_i, ml_b_f, ml_norm, dn_conv, dn_A_log, dn_dt_bias, dn_norm, lru_conv_w, lru_conv_b, lru_wa, lru_ba, lru_wx, lru_bx, lru_lambda, rw_mu, rw_w0, rw_w2, rw_a0, rw_a2, rw_g2, rw_k_k, rw_k_a, rw_r_k, rw_ln_w, rw_ln_b, w_branch, w_out, norm_ffn, w_ffn_gate, w_ffn_up, w_ffn_down, norm_final):
    params = {
        'ml_b_i': ml_b_i, 'ml_b_f': ml_b_f, 'ml_norm': ml_norm,
        'dn_conv': dn_conv, 'dn_A_log': dn_A_log, 'dn_dt_bias': dn_dt_bias, 'dn_norm': dn_norm,
        'lru_conv_w': lru_conv_w, 'lru_conv_b': lru_conv_b, 'lru_wa': lru_wa, 'lru_ba': lru_ba,
        'lru_wx': lru_wx, 'lru_bx': lru_bx, 'lru_lambda': lru_lambda,
        'rw_mu': rw_mu, 'rw_w0': rw_w0, 'rw_w2': rw_w2, 'rw_a0': rw_a0, 'rw_a2': rw_a2, 'rw_g2': rw_g2,
        'rw_k_k': rw_k_k, 'rw_k_a': rw_k_a, 'rw_r_k': rw_r_k, 'rw_ln_w': rw_ln_w, 'rw_ln_b': rw_ln_b,
    }
    depth = w_in.shape[0]
    bp, tp, d = x_prompt.shape
    bs, ts, _ = x_sample.shape
    n_p, n_s = bp * tp, bs * ts
    m = n_p + n_s
    x = jnp.concatenate([x_prompt.reshape(n_p, d), x_sample.reshape(n_s, d)], axis=0)
    tm = _row_tile(m, 1088)
    tm_panel = _row_tile(m, 2176)
    tm_norm = _row_tile(m, 544)
    tm_group = _row_tile(math.gcd(n_p, n_s), 512)
    sample_states = (state_mlstm_C, state_mlstm_n, state_mlstm_m, state_delta_S, state_delta_conv,
                     state_rglru_h, state_rglru_conv, state_rwkv_S, state_rwkv_shift)
    big_p = big_s = None
    new_p = [[] for _ in _STATE_SHAPES]
    new_s = [[] for _ in _STATE_SHAPES]
    wd_b = w_ffn_down.astype(BF16)
    w_out_b = w_out.astype(BF16)
    w_branch_b = w_branch.astype(BF16)
    for l in range(depth):
        p = {name: w[l] for name, w in params.items()}
        h = _rmsnorm(x, norm_mix[l], BF16, tm_norm)
        proj = _matmul(h, w_in, l, tm_panel, 512)
        st_p = tuple(jnp.zeros((bp,) + shp, F32) for shp in _STATE_SHAPES)
        st_s = tuple((s, l) if s.ndim == 5 else s[l] for s in sample_states)
        outs_p, st_p = _mixers(proj, 0, bp, tp, p, st_p, (l, depth), big_p)
        outs_s, st_s = _mixers(proj, n_p, bs, ts, p, st_s, (l, depth), big_s)
        big_p = tuple(st_p[i] for i in _BIG_STATES)
        big_s = tuple(st_s[i] for i in _BIG_STATES)
        for lst, s in zip(new_p, st_p):
            lst.append(s)
        for lst, s in zip(new_s, st_s):
            lst.append(s)
        mix = _merge(proj, outs_p, outs_s, w_branch_b, l, tm_group, 1024)
        x = _matmul_residual(mix, w_out_b, l, x, tm, 1024, 2048)
        h2 = _rmsnorm(x, norm_ffn[l], BF16, tm_norm)
        act = _ffn_up(h2, w_ffn_gate, w_ffn_up, l, tm_panel, 256)
        x = _matmul_residual(act, wd_b, l, x, tm // 2, 512, D_FF)
    y_prompt, y_sample = _rmsnorm_split(x, norm_final, n_p, tm_group)
    y_prompt = y_prompt.reshape(bp, tp, d)
    y_sample = y_sample.reshape(bs, ts, d)
    def stacked(per_layer):
        return tuple(lst[-1] if i in _BIG_STATES else jnp.stack(lst) for i, lst in enumerate(per_layer))

    return (y_prompt, y_sample) + stacked(new_p) + stacked(new_s)
```
